```python
import math
import jax
import jax.numpy as jnp
from jax import lax
import numpy as np

D_MODEL = 1024
BATCH = 16
SEQ = 2048
DEPTH = 2

HEAD_DIM = 64
SB_HEADS = 4
MLA_HEADS = 6
MLA_Q_RANK = 256
MLA_KV_RANK = 128
MLA_NOPE = 64
MLA_ROPE = 32
MLA_V = 64
MLA_QK = MLA_NOPE + MLA_ROPE
ROPE_THETA = 10000.0
SW_HEADS = 6
SW_KV_HEADS = 2
WINDOW = 128
REL_BUCKETS = 32
REL_MAX_DIST = 128
BLOCK = 128
D_FF = 2816
CONV_W = 3
EPS = 1e-6
NEG = -1e30

D_MIX = SB_HEADS * HEAD_DIM + MLA_HEADS * MLA_V + SW_HEADS * HEAD_DIM
IN_SPLITS = (SB_HEADS * HEAD_DIM, SB_HEADS * HEAD_DIM, SB_HEADS * HEAD_DIM,
             MLA_Q_RANK, MLA_KV_RANK, MLA_ROPE,
             SW_HEADS * HEAD_DIM, SW_KV_HEADS * HEAD_DIM, SW_KV_HEADS * HEAD_DIM)
D_IN = 3 * SB_HEADS * HEAD_DIM + MLA_Q_RANK + MLA_KV_RANK + MLA_ROPE + (SW_HEADS + 2 * SW_KV_HEADS) * HEAD_DIM

kernel_name = 'hybrid_sb_mla_swa_convffn_block'


def rms_norm(x, g):
    xf = x.astype(jnp.float32)
    y = xf * lax.rsqrt(jnp.mean(xf * xf, axis=-1, keepdims=True) + EPS)
    return (y * g.astype(jnp.float32)).astype(x.dtype)


def split_cols(t, sizes):
    out = []
    start = 0
    for n in sizes:
        out.append(t[..., start:start + n])
        start += n
    return out


def apply_rope(x, positions):
    half = x.shape[-1] // 2
    inv_freq = jnp.power(ROPE_THETA, -jnp.arange(half, dtype=jnp.float32) / half)
    ang = positions.astype(jnp.float32)[..., None] * inv_freq
    cos = jnp.cos(ang)[:, :, None, :]
    sin = jnp.sin(ang)[:, :, None, :]
    x1 = x[..., :half].astype(jnp.float32)
    x2 = x[..., half:].astype(jnp.float32)
    out = jnp.concatenate([x1 * cos - x2 * sin, x1 * sin + x2 * cos], axis=-1)
    return out.astype(x.dtype)


def t5_causal_bucket(dist):
    max_exact = REL_BUCKETS // 2
    n = jnp.maximum(dist, 0)
    nf = jnp.maximum(n, 1).astype(jnp.float32)
    large = max_exact + (jnp.log(nf / max_exact) / math.log(REL_MAX_DIST / max_exact)
                         * (REL_BUCKETS - max_exact)).astype(jnp.int32)
    large = jnp.minimum(large, REL_BUCKETS - 1)
    return jnp.where(n < max_exact, n, large)


def window_rel_bias(rel_table):
    a = jnp.arange(BLOCK)[:, None]
    b = jnp.arange(2 * BLOCK)[None, :]
    bucket = t5_causal_bucket(BLOCK + a - b)
    return jnp.transpose(rel_table[bucket], (2, 0, 1))


def stick_breaking_attention(q, k, v):
    B, S, H, D = q.shape
    scale = D ** -0.5
    outs = []
    for i in range(S // BLOCK):
        t0 = i * BLOCK
        end = t0 + BLOCK
        z = jnp.einsum('bqhd,bkhd->bhqk', q[:, t0:end], k[:, :end]).astype(jnp.float32) * scale
        strict = jnp.arange(end)[None, :] < (t0 + jnp.arange(BLOCK))[:, None]
        log_keep = jnp.where(strict, -jax.nn.softplus(z), 0.0)
        suffix = lax.cumsum(log_keep, axis=log_keep.ndim - 1, reverse=True) - log_keep
        weights = jnp.where(strict, jnp.exp(jax.nn.log_sigmoid(z) + suffix), 0.0)
        outs.append(jnp.einsum('bhqk,bkhd->bqhd', weights.astype(v.dtype), v[:, :end]))
    return jnp.concatenate(outs, axis=1)


def causal_softmax_attention(q, k, v):
    B, S, H, Dk = q.shape
    scale = Dk ** -0.5
    outs = []
    for i in range(S // BLOCK):
        t0 = i * BLOCK
        end = t0 + BLOCK
        s = jnp.einsum('bqhd,bkhd->bhqk', q[:, t0:end], k[:, :end]).astype(jnp.float32) * scale
        causal = jnp.arange(end)[None, :] <= (t0 + jnp.arange(BLOCK))[:, None]
        p = jax.nn.softmax(jnp.where(causal, s, NEG), axis=-1)
        outs.append(jnp.einsum('bhqk,bkhd->bqhd', p.astype(v.dtype), v[:, :end]))
    return jnp.concatenate(outs, axis=1)


def sliding_window_sink_attention(q, k, v, sinks, rel_bias):
    B, S, H, D = q.shape
    G = k.shape[2]
    R = H // G
    nb = S // BLOCK
    qb = q.reshape(B, nb, BLOCK, G, R, D)

    def band(t):
        tp = jnp.concatenate([jnp.zeros((B, BLOCK, G, D), t.dtype), t], axis=1)
        tp = tp.reshape(B, nb + 1, BLOCK, G, D)
        return jnp.concatenate([tp[:, :-1], tp[:, 1:]], axis=2)

    kb = band(k)
    vb = band(v)
    s = jnp.einsum('bnqgrd,bnkgd->bngrqk', qb, kb).astype(jnp.float32) * (D ** -0.5)
    s = s + rel_bias.astype(jnp.float32).reshape(G, R, BLOCK, 2 * BLOCK)[None, None]
    dist = BLOCK + jnp.arange(BLOCK)[:, None] - jnp.arange(2 * BLOCK)[None, :]
    in_window = (dist >= 0) & (dist < WINDOW)
    key_pos = (jnp.arange(nb)[:, None] - 1) * BLOCK + jnp.arange(2 * BLOCK)[None, :]
    valid = in_window[None] & (key_pos >= 0)[:, None, :]
    s = jnp.where(valid[None, :, None, None], s, NEG)
    sink = jnp.broadcast_to(sinks.astype(jnp.float32).reshape(1, 1, G, R, 1, 1), s.shape[:-1] + (1,))
    p = jax.nn.softmax(jnp.concatenate([s, sink], axis=-1), axis=-1)[..., :-1]
    o = jnp.einsum('bngrqk,bnkgd->bnqgrd', p.astype(v.dtype), vb)
    return o.reshape(B, S, H, D)


def causal_depthwise_conv(u, w, b):
    C = u.shape[-1]
    y = lax.conv_general_dilated(u, w[:, None, :].astype(u.dtype), window_strides=(1,),
                                 padding=[(CONV_W - 1, 0)],
                                 dimension_numbers=('NWC', 'WIO', 'NWC'),
                                 feature_group_count=C)
    return y + b


def hybrid_layer(x, cond, positions, rel_bias, norm1_g, norm2_g, w_ada, b_ada, w_in,
                 mla_cq_g, w_uq, mla_ckv_g, w_ukv, mla_qn_g, mla_kn_g, sw_qn_g, sw_kn_g,
                 sw_sinks, w_out, w_up, conv_w, conv_b, w_down):
    B, S, _ = x.shape
    mods = jnp.einsum('bd,de->be', jax.nn.silu(cond), w_ada) + b_ada
    shift1, scale1, gate1, shift2, scale2, gate2 = jnp.split(mods[:, None, :], 6, axis=-1)

    h = rms_norm(x, norm1_g) * (1.0 + scale1) + shift1
    proj = jnp.einsum('bsd,de->bse', h, w_in)
    sb_q, sb_k, sb_v, cq, ckv, k_rope, sw_q, sw_k, sw_v = split_cols(proj, IN_SPLITS)

    sb_shape = (B, S, SB_HEADS, HEAD_DIM)
    o_a = stick_breaking_attention(sb_q.reshape(sb_shape), sb_k.reshape(sb_shape), sb_v.reshape(sb_shape))

    q_b = jnp.einsum('bsr,re->bse', rms_norm(cq, mla_cq_g), w_uq).reshape(B, S, MLA_HEADS, MLA_QK)
    kv_b = jnp.einsum('bsr,re->bse', rms_norm(ckv, mla_ckv_g), w_ukv).reshape(B, S, MLA_HEADS, MLA_NOPE + MLA_V)
    k_nope = kv_b[..., :MLA_NOPE]
    v_b = kv_b[..., MLA_NOPE:]
    k_rope_h = jnp.broadcast_to(k_rope[:, :, None, :], (B, S, MLA_HEADS, MLA_ROPE))
    k_b = jnp.concatenate([k_nope, k_rope_h], axis=-1)
    q_b = rms_norm(q_b, mla_qn_g)
    k_b = rms_norm(k_b, mla_kn_g)
    q_b = jnp.concatenate([q_b[..., :MLA_NOPE], apply_rope(q_b[..., MLA_NOPE:], positions)], axis=-1)
    k_b = jnp.concatenate([k_b[..., :MLA_NOPE], apply_rope(k_b[..., MLA_NOPE:], positions)], axis=-1)
    o_b = causal_softmax_attention(q_b, k_b, v_b)

    q_c = rms_norm(sw_q.reshape(B, S, SW_HEADS, HEAD_DIM), sw_qn_g)
    k_c = rms_norm(sw_k.reshape(B, S, SW_KV_HEADS, HEAD_DIM), sw_kn_g)
    v_c = sw_v.reshape(B, S, SW_KV_HEADS, HEAD_DIM)
    o_c = sliding_window_sink_attention(q_c, k_c, v_c, sw_sinks, rel_bias)

    mix = jnp.concatenate([o_a.reshape(B, S, -1), o_b.reshape(B, S, -1), o_c.reshape(B, S, -1)], axis=-1)
    x = x + gate1 * jnp.einsum('bse,ed->bsd', mix, w_out)

    h2 = rms_norm(x, norm2_g) * (1.0 + scale2) + shift2
    u = causal_depthwise_conv(jnp.einsum('bsd,df->bsf', h2, w_up), conv_w, conv_b)
    g = u[..., :D_FF]
    val = u[..., D_FF:]
    y = jnp.einsum('bsf,fd->bsd', jax.nn.silu(g) * val, w_down)
    return x + gate2 * y


def setup_inputs(seed: int = 0) -> dict:
    key = jax.random.key(seed)
    ks = jax.random.split(key, 24)
    f32 = jnp.float32
    L = DEPTH
    D = D_MODEL

    def nrm(k, shape, scale):
        return jax.random.normal(k, shape, f32) * scale

    def gain(k, shape):
        return 1.0 + 0.02 * jax.random.normal(k, shape, f32)

    x = nrm(ks[0], (BATCH, SEQ, D), 1.0)
    c = nrm(ks[1], (BATCH, D), 1.0)
    offsets = jax.random.randint(ks[2], (BATCH, 1), 0, SEQ, dtype=jnp.int32)
    positions = offsets + jnp.arange(SEQ, dtype=jnp.int32)[None, :]
    rel_table = nrm(ks[3], (REL_BUCKETS, SW_HEADS), 0.5)
    norm1_g = gain(ks[4], (L, D))
    norm2_g = gain(ks[5], (L, D))
    w_ada = nrm(ks[6], (L, D, 6 * D), 0.5 * D ** -0.5)
    b_ada = nrm(ks[7], (L, 6 * D), 0.02)
    w_in = nrm(ks[8], (L, D, D_IN), D ** -0.5)
    mla_cq_g = gain(ks[9], (L, MLA_Q_RANK))
    w_uq = nrm(ks[10], (L, MLA_Q_RANK, MLA_HEADS * MLA_QK), MLA_Q_RANK ** -0.5)
    mla_ckv_g = gain(ks[11], (L, MLA_KV_RANK))
    w_ukv = nrm(ks[12], (L, MLA_KV_RANK, MLA_HEADS * (MLA_NOPE + MLA_V)), MLA_KV_RANK ** -0.5)
    mla_qn_g = gain(ks[13], (L, MLA_QK))
    mla_kn_g = gain(ks[14], (L, MLA_QK))
    sw_qn_g = gain(ks[15], (L, HEAD_DIM))
    sw_kn_g = gain(ks[16], (L, HEAD_DIM))
    sw_sinks = nrm(ks[17], (L, SW_HEADS), 1.0)
    w_out = nrm(ks[18], (L, D_MIX, D), D_MIX ** -0.5)
    w_up = nrm(ks[19], (L, D, 2 * D_FF), D ** -0.5)
    conv_w = nrm(ks[20], (L, CONV_W, 2 * D_FF), CONV_W ** -0.5)
    conv_b = nrm(ks[21], (L, 2 * D_FF), 0.02)
    w_down = nrm(ks[22], (L, D_FF, D), D_FF ** -0.5)
    return {'x': x, 'c': c, 'positions': positions, 'rel_table': rel_table,
            'norm1_g': norm1_g, 'norm2_g': norm2_g, 'w_ada': w_ada, 'b_ada': b_ada,
            'w_in': w_in, 'mla_cq_g': mla_cq_g, 'w_uq': w_uq, 'mla_ckv_g': mla_ckv_g,
            'w_ukv': w_ukv, 'mla_qn_g': mla_qn_g, 'mla_kn_g': mla_kn_g,
            'sw_qn_g': sw_qn_g, 'sw_kn_g': sw_kn_g, 'sw_sinks': sw_sinks,
            'w_out': w_out, 'w_up': w_up, 'conv_w': conv_w, 'conv_b': conv_b,
            'w_down': w_down}


def reference(x, c, positions, rel_table, norm1_g, norm2_g, w_ada, b_ada, w_in,
              mla_cq_g, w_uq, mla_ckv_g, w_ukv, mla_qn_g, mla_kn_g, sw_qn_g, sw_kn_g,
              sw_sinks, w_out, w_up, conv_w, conv_b, w_down):
    rel_bias = window_rel_bias(rel_table)
    for l in range(DEPTH):
        x = hybrid_layer(x, c, positions, rel_bias, norm1_g[l], norm2_g[l], w_ada[l], b_ada[l],
                         w_in[l], mla_cq_g[l], w_uq[l], mla_ckv_g[l], w_ukv[l], mla_qn_g[l],
                         mla_kn_g[l], sw_qn_g[l], sw_kn_g[l], sw_sinks[l], w_out[l], w_up[l],
                         conv_w[l], conv_b[l], w_down[l])
    return x
```

```python
import functools
import math

import numpy as np
import jax
import jax.numpy as jnp
from jax import lax
from jax.experimental import pallas as pl
from jax.experimental.pallas import tpu as pltpu

F32 = jnp.float32
BF16 = jnp.bfloat16

D_MODEL = 1024
HEAD_DIM = 64
SB_HEADS = 4
MLA_HEADS = 6
MLA_Q_RANK = 256
MLA_KV_RANK = 128
MLA_NOPE = 64
MLA_ROPE = 32
MLA_V = 64
MLA_QK = MLA_NOPE + MLA_ROPE
ROPE_THETA = 10000.0
SW_HEADS = 6
SW_KV_HEADS = 2
WINDOW = 128
REL_BUCKETS = 32
REL_MAX_DIST = 128
D_FF = 2816
CONV_W = 3
EPS = 1e-6
NEG = -1e30

LANES = 128
SLOT = LANES
HALF = SLOT // 2

C_SBQ, C_SBK, C_SBV = 0, 256, 512
C_CQ = 768
C_CKV = 1024
C_KROPE = 1152
C_KROPE_SW = 1280
C_SWQ = 1408
C_SWK = 1792
C_SWV = 1920
N_IN = 2048

ROPE_LO = MLA_NOPE
ROPE_HALF = MLA_ROPE // 2

TM_PREP = 512
TM_MLP = 512
TQ = 256
TK = 256
FF_CHUNK = 256
CARRY_ROWS = 8

VMEM_LIMIT = 56 * 1024 * 1024


def _cparams(n_axes):
    return pltpu.CompilerParams(dimension_semantics=("arbitrary",) * n_axes,
                                vmem_limit_bytes=VMEM_LIMIT)


def _rms(x, n):
    return x * lax.rsqrt(jnp.sum(x * x, axis=-1, keepdims=True) * (1.0 / n) + EPS)


def _nt_dot(a, b):
    return lax.dot_general(a, b, (((1,), (1,)), ((), ())), preferred_element_type=F32)


def _dot(a, b):
    return jnp.dot(a, b, preferred_element_type=F32)


def _mods_kernel(c_ref, w_ref, b_ref, o_ref):
    c = c_ref[...]
    a = (c * jax.nn.sigmoid(c)).astype(BF16)
    o_ref[0] = _dot(a, w_ref[0].astype(BF16)) + b_ref[0]


def _mods(c, w_ada, b_ada):
    depth, d, n = w_ada.shape
    b = c.shape[0]
    tn = 1536
    return pl.pallas_call(
        _mods_kernel,
        out_shape=jax.ShapeDtypeStruct((depth, b, n), F32),
        grid=(depth, n // tn),
        in_specs=[pl.BlockSpec((b, d), lambda l, j: (0, 0)),
                  pl.BlockSpec((1, d, tn), lambda l, j: (l, 0, j)),
                  pl.BlockSpec((1, 1, tn), lambda l, j: (l, 0, j))],
        out_specs=pl.BlockSpec((1, b, tn), lambda l, j: (l, 0, j)),
        compiler_params=_cparams(2),
        name="adaln_mods",
    )(c, w_ada, b_ada.reshape(depth, 1, n))


def _rope_kernel(pos_ref, invf_ref, cos_ref, sin_ref):
    pos = pos_ref[0].astype(F32)
    ang = invf_ref[...] * pos
    c = jnp.cos(ang)
    s = jnp.sin(ang)
    tm = pos.shape[1]
    ones = jnp.ones((ROPE_LO, tm), F32)
    zeros = jnp.zeros((ROPE_LO, tm), F32)
    pad = SLOT - ROPE_LO - MLA_ROPE
    cos_t = jnp.concatenate([ones, c, c, jnp.ones((pad, tm), F32)], axis=0)
    sin_t = jnp.concatenate([zeros, -s, s, jnp.zeros((pad, tm), F32)], axis=0)
    cos_ref[0] = cos_t.T
    sin_ref[0] = sin_t.T


def _rope_tables(positions):
    b, s = positions.shape
    tm = 512
    half = ROPE_HALF
    inv_freq = jnp.power(ROPE_THETA, -jnp.arange(half, dtype=F32) / half).reshape(half, 1)
    out = jax.ShapeDtypeStruct((b, s, SLOT), F32)
    return pl.pallas_call(
        _rope_kernel,
        out_shape=(out, out),
        grid=(b, s // tm),
        in_specs=[pl.BlockSpec((1, 1, tm), lambda i, j: (i, 0, j)),
                  pl.BlockSpec((half, 1), lambda i, j: (0, 0))],
        out_specs=(pl.BlockSpec((1, tm, SLOT), lambda i, j: (i, j, 0)),
                   pl.BlockSpec((1, tm, SLOT), lambda i, j: (i, j, 0))),
        compiler_params=_cparams(2),
        name="rope_tables",
    )(positions.reshape(b, 1, s), inv_freq)


def _t5_bucket(dist):
    max_exact = REL_BUCKETS // 2
    n = jnp.maximum(dist, 0)
    nf = jnp.maximum(n, 1).astype(F32)
    large = max_exact + (jnp.log(nf / max_exact) / math.log(REL_MAX_DIST / max_exact)
                         * (REL_BUCKETS - max_exact)).astype(jnp.int32)
    large = jnp.minimum(large, REL_BUCKETS - 1)
    return jnp.where(n < max_exact, n, large)


def _bias_kernel(tab_ref, bucket_ref, o_ref):
    bucket = bucket_ref[...]
    for h in range(SW_HEADS):
        acc = jnp.zeros(bucket.shape, F32)
        for bkt in range(REL_BUCKETS):
            acc = jnp.where(bucket == bkt, tab_ref[bkt, h], acc)
        o_ref[h] = jnp.where(bucket >= 0, acc, NEG)


def _window_bias(rel_table):
    nk = WINDOW + TQ
    key = jnp.arange(nk)[:, None]
    qry = jnp.arange(TQ)[None, :]
    dist = qry + WINDOW - key
    valid = (dist >= 0) & (dist < WINDOW)
    bucket = jnp.where(valid, _t5_bucket(dist), -1).astype(jnp.int32)
    return pl.pallas_call(
        _bias_kernel,
        out_shape=jax.ShapeDtypeStruct((SW_HEADS, nk, TQ), F32),
        in_specs=[pl.BlockSpec(memory_space=pltpu.SMEM),
                  pl.BlockSpec(memory_space=pltpu.VMEM)],
        out_specs=pl.BlockSpec(memory_space=pltpu.VMEM),
        name="window_bias",
    )(rel_table, bucket)


def _half_rms(x, gain, scale):
    lo = lax.broadcasted_iota(jnp.int32, x.shape, 1) < HALF
    sq = x * x
    s_lo = jnp.sum(jnp.where(lo, sq, 0.0), axis=-1, keepdims=True)
    s_hi = jnp.sum(jnp.where(lo, 0.0, sq), axis=-1, keepdims=True)
    r = jnp.where(lo, lax.rsqrt(s_lo * (1.0 / HEAD_DIM) + EPS), lax.rsqrt(s_hi * (1.0 / HEAD_DIM) + EPS))
    return x * r * (gain * scale)


def _prep_kernel(x_ref, mod_ref, n1g_ref, win_ref, cqg_ref, wuq_ref, ckvg_ref, wukv_ref,
                 gq_ref, gqs_ref, gk_ref, gks_ref, swqg_ref, swkg_ref, cos_ref, sin_ref,
                 sbq_ref, sbk_ref, sbvt_ref, mq_ref, mk_ref, mvt_ref, swq_ref, swk_ref, swvt_ref):
    x = x_ref[0]
    mod = mod_ref[0]
    shift1, scale1 = mod[0:1], mod[1:2]
    h = _rms(x, D_MODEL) * n1g_ref[...] * (1.0 + scale1) + shift1
    proj = _dot(h.astype(BF16), win_ref[...])

    sbq_ref[0] = (proj[:, C_SBQ:C_SBQ + 256] * HEAD_DIM ** -0.5).astype(BF16)
    sbk_ref[0] = proj[:, C_SBK:C_SBK + 256].astype(BF16)
    sbvt_ref[0] = proj[:, C_SBV:C_SBV + 256].T.astype(BF16)

    cos = cos_ref[0]
    sin = sin_ref[0]

    cq = proj[:, C_CQ:C_CQ + MLA_Q_RANK]
    cqn = (_rms(cq, MLA_Q_RANK) * cqg_ref[...]).astype(BF16)
    qraw = _dot(cqn, wuq_ref[...])
    nq = MLA_HEADS * SLOT
    q_scale = MLA_QK ** -0.5
    for hd in range(MLA_HEADS):
        slot = qraw[:, hd * SLOT:(hd + 1) * SLOT]
        swapped = qraw[:, nq + hd * SLOT:nq + (hd + 1) * SLOT]
        r = lax.rsqrt(jnp.sum(slot * slot, axis=-1, keepdims=True) * (1.0 / MLA_QK) + EPS) * q_scale
        q = (slot * r * gq_ref[...]) * cos + (swapped * r * gqs_ref[...]) * sin
        mq_ref[0, :, hd * SLOT:(hd + 1) * SLOT] = q.astype(BF16)

    ckv = proj[:, C_CKV:C_CKV + MLA_KV_RANK]
    ckvn = (_rms(ckv, MLA_KV_RANK) * ckvg_ref[...]).astype(BF16)
    kv = _dot(ckvn, wukv_ref[...])
    mvt_ref[0] = kv[:, nq:nq + MLA_HEADS * MLA_V].T.astype(BF16)
    krope = proj[:, C_KROPE:C_KROPE + SLOT]
    krope_sw = proj[:, C_KROPE_SW:C_KROPE_SW + SLOT]
    for hd in range(MLA_HEADS):
        slot = kv[:, hd * SLOT:(hd + 1) * SLOT] + krope
        r = lax.rsqrt(jnp.sum(slot * slot, axis=-1, keepdims=True) * (1.0 / MLA_QK) + EPS)
        k = (slot * r * gk_ref[...]) * cos + (krope_sw * r * gks_ref[...]) * sin
        mk_ref[0, :, hd * SLOT:(hd + 1) * SLOT] = k.astype(BF16)

    for g in range(SW_HEADS // 2):
        xq = proj[:, C_SWQ + g * SLOT:C_SWQ + (g + 1) * SLOT]
        swq_ref[0, :, g * SLOT:(g + 1) * SLOT] = _half_rms(xq, swqg_ref[...], HEAD_DIM ** -0.5).astype(BF16)
    swk_ref[0] = _half_rms(proj[:, C_SWK:C_SWK + SLOT], swkg_ref[...], 1.0).astype(BF16)
    swvt_ref[0] = proj[:, C_SWV:C_SWV + SLOT].T.astype(BF16)


def _prep(x, mods, n1g, w_in, cqg, w_uq, ckvg, w_ukv, gq, gqs, gk, gks, swqg, swkg, cos_t, sin_t):
    b, s, d = x.shape
    tm = TM_PREP
    row = lambda w: pl.BlockSpec((1, tm, w), lambda i, j: (i, j, 0))
    colt = lambda w: pl.BlockSpec((1, w, tm), lambda i, j: (i, 0, j))
    full = lambda a: pl.BlockSpec(a.shape, lambda i, j: (0,) * a.ndim)
    act = lambda w: jax.ShapeDtypeStruct((b, s, w), BF16)
    actt = lambda w: jax.ShapeDtypeStruct((b, w, s), BF16)
    return pl.pallas_call(
        _prep_kernel,
        out_shape=(act(256), act(256), actt(256), act(768), act(768), actt(384),
                   act(384), act(128), actt(128)),
        grid=(b, s // tm),
        in_specs=[row(d), pl.BlockSpec((1, 6, d), lambda i, j: (i, 0, 0)), full(n1g), full(w_in),
                  full(cqg), full(w_uq), full(ckvg), full(w_ukv), full(gq), full(gqs), full(gk),
                  full(gks), full(swqg), full(swkg), row(SLOT), row(SLOT)],
        out_specs=(row(256), row(256), colt(256), row(768), row(768), colt(384),
                   row(384), row(128), colt(128)),
        compiler_params=_cparams(2),
        name="prep_qkv",
    )(x, mods, n1g, w_in, cqg, w_uq, ckvg, w_ukv, gq, gqs, gk, gks, swqg, swkg, cos_t, sin_t)


def _half_mask(q, half):
    lane = lax.broadcasted_iota(jnp.int32, q.shape, 1)
    keep = (lane < HALF) if half == 0 else (lane >= HALF)
    return jnp.where(keep, q, jnp.zeros_like(q))


def _store_pair(o_ref, g, out_lo, out_hi):
    pair = jnp.concatenate([out_lo, out_hi], axis=0)
    o_ref[0, :, g * SLOT:(g + 1) * SLOT] = pair.T.astype(o_ref.dtype)


def _softplus(z):
    return jnp.maximum(z, 0.0) + jnp.log1p(jnp.exp(-jnp.abs(z)))


def _suffix_sum(tri, log_keep):
    hi = log_keep.astype(BF16)
    lo = (log_keep - hi.astype(F32)).astype(BF16)
    return _dot(tri, hi) + _dot(tri, lo)


def _sb_kernel(q_ref, k_ref, vt_ref, tri_ref, o_ref):
    qi = pl.program_id(1)
    tri = tri_ref[...]
    row = lax.broadcasted_iota(jnp.int32, (TK, TQ), 0)
    col = lax.broadcasted_iota(jnp.int32, (TK, TQ), 1)
    strict = row < col
    for g in range(SB_HEADS // 2):
        outs = []
        for half in range(2):
            q = _half_mask(q_ref[0, :, g * SLOT:(g + 1) * SLOT], half)
            vrow = (2 * g + half) * HEAD_DIM

            start = pl.multiple_of(qi * TK, TK)
            k = k_ref[0, pl.ds(start, TK), g * SLOT:(g + 1) * SLOT]
            z = _nt_dot(k, q)
            log_keep = jnp.where(strict, -_softplus(z), 0.0)
            incl = _suffix_sum(tri, log_keep)
            w = jnp.where(strict, jnp.exp(z + incl), 0.0)
            vt = vt_ref[0, vrow:vrow + HEAD_DIM, pl.ds(start, TK)]
            acc = _dot(vt, w.astype(BF16))
            run = incl[0:1, :]

            def body(i, carry):
                acc, run = carry
                start = pl.multiple_of((qi - 1 - i) * TK, TK)
                k = k_ref[0, pl.ds(start, TK), g * SLOT:(g + 1) * SLOT]
                z = _nt_dot(k, q)
                incl = _suffix_sum(tri, -_softplus(z))
                w = jnp.exp(z + incl + run)
                vt = vt_ref[0, vrow:vrow + HEAD_DIM, pl.ds(start, TK)]
                return acc + _dot(vt, w.astype(BF16)), run + incl[0:1, :]

            acc, run = lax.fori_loop(0, qi, body, (acc, run))
            outs.append(acc)
        _store_pair(o_ref, g, outs[0], outs[1])


def _sb_attention(q, k, vt):
    b, s, w = q.shape
    tri = (jnp.arange(TK)[None, :] >= jnp.arange(TK)[:, None]).astype(BF16)
    return pl.pallas_call(
        _sb_kernel,
        out_shape=jax.ShapeDtypeStruct((b, s, w), BF16),
        grid=(b, s // TQ),
        in_specs=[pl.BlockSpec((1, TQ, w), lambda i, j: (i, j, 0)),
                  pl.BlockSpec((1, s, w), lambda i, j: (i, 0, 0)),
                  pl.BlockSpec((1, w, s), lambda i, j: (i, 0, 0)),
                  pl.BlockSpec((TK, TK), lambda i, j: (0, 0))],
        out_specs=pl.BlockSpec((1, TQ, w), lambda i, j: (i, j, 0)),
        compiler_params=_cparams(2),
        name="sb_attention",
    )(q, k, vt, tri)


def _mla_kernel(q_ref, k_ref, vt_ref, o_ref):
    qi = pl.program_id(1)
    row = lax.broadcasted_iota(jnp.int32, (TK, TQ), 0)
    col = lax.broadcasted_iota(jnp.int32, (TK, TQ), 1)
    causal = row <= col

    for g in range(MLA_HEADS // 2):
        outs = []
        for half in range(2):
            hd = 2 * g + half
            q = q_ref[0, :, hd * SLOT:(hd + 1) * SLOT]
            vrow = hd * MLA_V

            def step(start, carry, masked):
                m, l, acc = carry
                k = k_ref[0, pl.ds(start, TK), hd * SLOT:(hd + 1) * SLOT]
                s = _nt_dot(k, q)
                if masked:
                    s = jnp.where(causal, s, NEG)
                m_new = jnp.maximum(m, jnp.max(s, axis=0, keepdims=True))
                alpha = jnp.exp(m - m_new)
                p = jnp.exp(s - m_new)
                l = alpha * l + jnp.sum(p, axis=0, keepdims=True)
                vt = vt_ref[0, vrow:vrow + MLA_V, pl.ds(start, TK)]
                acc = alpha * acc + _dot(vt, p.astype(BF16))
                return m_new, l, acc

            init = (jnp.full((1, TQ), NEG, F32), jnp.zeros((1, TQ), F32), jnp.zeros((MLA_V, TQ), F32))
            carry = lax.fori_loop(
                0, qi, lambda j, c: step(pl.multiple_of(j * TK, TK), c, False), init)
            m, l, acc = step(pl.multiple_of(qi * TK, TK), carry, True)
            outs.append(acc * (1.0 / l))
        _store_pair(o_ref, g, outs[0], outs[1])


def _mla_attention(q, k, vt):
    b, s, w = q.shape
    wv = vt.shape[1]
    return pl.pallas_call(
        _mla_kernel,
        out_shape=jax.ShapeDtypeStruct((b, s, wv), BF16),
        grid=(b, s // TQ),
        in_specs=[pl.BlockSpec((1, TQ, w), lambda i, j: (i, j, 0)),
                  pl.BlockSpec((1, s, w), lambda i, j: (i, 0, 0)),
                  pl.BlockSpec((1, wv, s), lambda i, j: (i, 0, 0))],
        out_specs=pl.BlockSpec((1, TQ, wv), lambda i, j: (i, j, 0)),
        compiler_params=_cparams(2),
        name="mla_attention",
    )(q, k, vt)


def _swa_kernel(sink_ref, q_ref, kp_ref, kc_ref, vtp_ref, vtc_ref, bias_ref, o_ref):
    qi = pl.program_id(1)
    k = jnp.concatenate([kp_ref[0], kc_ref[0]], axis=0)
    vt = jnp.concatenate([vtp_ref[0], vtc_ref[0]], axis=1)
    nk = WINDOW + TQ
    before_start = (lax.broadcasted_iota(jnp.int32, (nk, TQ), 0) < WINDOW) & (qi == 0)
    for g in range(SW_HEADS // 2):
        outs = []
        for half in range(2):
            hd = g + half * (SW_HEADS // SW_KV_HEADS)
            q = _half_mask(q_ref[0, :, g * SLOT:(g + 1) * SLOT], half)
            s = _nt_dot(k, q) + bias_ref[hd]
            s = jnp.where(before_start, NEG, s)
            sink = sink_ref[hd]
            m = jnp.maximum(jnp.max(s, axis=0, keepdims=True), sink)
            p = jnp.exp(s - m)
            l = jnp.sum(p, axis=0, keepdims=True) + jnp.exp(sink - m)
            acc = _dot(vt[half * HEAD_DIM:(half + 1) * HEAD_DIM, :], p.astype(BF16))
            outs.append(acc * (1.0 / l))
        _store_pair(o_ref, g, outs[0], outs[1])


def _swa_attention(sinks, q, k, vt, bias):
    b, s, w = q.shape
    prev = lambda j: jnp.maximum(2 * j - 1, 0)
    return pl.pallas_call(
        _swa_kernel,
        out_shape=jax.ShapeDtypeStruct((b, s, w), BF16),
        grid=(b, s // TQ),
        in_specs=[pl.BlockSpec(memory_space=pltpu.SMEM),
                  pl.BlockSpec((1, TQ, w), lambda i, j: (i, j, 0)),
                  pl.BlockSpec((1, WINDOW, SLOT), lambda i, j: (i, prev(j), 0)),
                  pl.BlockSpec((1, TQ, SLOT), lambda i, j: (i, j, 0)),
                  pl.BlockSpec((1, SLOT, WINDOW), lambda i, j: (i, 0, prev(j))),
                  pl.BlockSpec((1, SLOT, TQ), lambda i, j: (i, 0, j)),
                  pl.BlockSpec(bias.shape, lambda i, j: (0, 0, 0))],
        out_specs=pl.BlockSpec((1, TQ, w), lambda i, j: (i, j, 0)),
        compiler_params=_cparams(2),
        name="swa_attention",
    )(sinks, q, k, k, vt, vt, bias)


def _mlp_kernel(x_ref, mod_ref, oa_ref, ob_ref, oc_ref, wo_ref, n2g_ref, wup_ref, cw_ref, cb_ref,
                wdn_ref, out_ref, carry_ref, acc_ref):
    si = pl.program_id(1)
    tm = x_ref.shape[1]
    mod = mod_ref[0]
    gate1, shift2, scale2, gate2 = mod[2:3], mod[3:4], mod[4:5], mod[5:6]
    na, nb = oa_ref.shape[2], ob_ref.shape[2]
    att = (_dot(oa_ref[0], wo_ref[0:na, :]) + _dot(ob_ref[0], wo_ref[na:na + nb, :])
           + _dot(oc_ref[0], wo_ref[na + nb:, :]))
    x1 = x_ref[0] + gate1 * att
    h2 = (_rms(x1, D_MODEL) * n2g_ref[...] * (1.0 + scale2) + shift2).astype(BF16)

    @pl.when(si == 0)
    def _():
        carry_ref[...] = jnp.zeros_like(carry_ref)

    def conv(col):
        cols = slice(col, col + FF_CHUNK)
        u = _dot(h2, wup_ref[:, cols])
        prev = carry_ref[:, cols]
        carry_ref[:, cols] = u[tm - CARRY_ROWS:, :]
        ext = jnp.concatenate([prev, u], axis=0)
        u1 = ext[CARRY_ROWS - 1:CARRY_ROWS - 1 + tm, :]
        u2 = ext[CARRY_ROWS - 2:CARRY_ROWS - 2 + tm, :]
        cw = cw_ref[:, cols]
        return u * cw[2:3] + u1 * cw[1:2] + u2 * cw[0:1] + cb_ref[:, cols]

    for c in range(D_FF // FF_CHUNK):
        gate = conv(c * FF_CHUNK)
        val = conv(D_FF + c * FF_CHUNK)
        a = (gate * jax.nn.sigmoid(gate) * val).astype(BF16)
        part = _dot(a, wdn_ref[c * FF_CHUNK:(c + 1) * FF_CHUNK, :])
        if c == 0:
            acc_ref[...] = part
        else:
            acc_ref[...] += part
    out_ref[0] = x1 + gate2 * acc_ref[...]


def _mlp(x, mods, oa, ob, oc, w_out, n2g, w_up, conv_w, conv_b, w_down):
    b, s, d = x.shape
    tm = TM_MLP
    row = lambda w: pl.BlockSpec((1, tm, w), lambda i, j: (i, j, 0))
    const = lambda a: pl.BlockSpec(a.shape, lambda i, j: (0,) * a.ndim, pipeline_mode=pl.Buffered(1))
    return pl.pallas_call(
        _mlp_kernel,
        out_shape=jax.ShapeDtypeStruct((b, s, d), F32),
        grid=(b, s // tm),
        in_specs=[row(d), pl.BlockSpec((1, 6, d), lambda i, j: (i, 0, 0)),
                  row(oa.shape[2]), row(ob.shape[2]), row(oc.shape[2]),
                  const(w_out), const(n2g), const(w_up), const(conv_w), const(conv_b), const(w_down)],
        out_specs=row(d),
        scratch_shapes=[pltpu.VMEM((CARRY_ROWS, 2 * D_FF), F32), pltpu.VMEM((tm, d), F32)],
        compiler_params=_cparams(2),
        name="outproj_mlp",
    )(x, mods, oa, ob, oc, w_out, n2g, w_up, conv_w, conv_b, w_down)


def _slot_gain(g, swap):
    z = jnp.zeros((SLOT - MLA_QK,), F32)
    lo, hi = g[ROPE_LO:ROPE_LO + ROPE_HALF], g[ROPE_LO + ROPE_HALF:MLA_QK]
    if swap:
        return jnp.concatenate([jnp.zeros((MLA_NOPE,), F32), hi, lo, z]).reshape(1, SLOT)
    return jnp.concatenate([g, z]).reshape(1, SLOT)


def _layout_w_in(w):
    d = w.shape[0]
    z = lambda n: jnp.zeros((d, n), w.dtype)
    kr = w[:, 1152:1184]
    swq = w[:, 1184:1568].reshape(d, SW_HEADS, HEAD_DIM)
    order = [0, 3, 1, 4, 2, 5]
    swq = swq[:, order, :].reshape(d, SW_HEADS * HEAD_DIM)
    cols = [w[:, 0:1152],
            z(MLA_NOPE), kr, z(SLOT - MLA_QK),
            z(MLA_NOPE), kr[:, ROPE_HALF:], kr[:, :ROPE_HALF], z(SLOT - MLA_QK),
            swq, w[:, 1568:1824]]
    return jnp.concatenate(cols, axis=1).astype(BF16)


def _layout_w_uq(w):
    r = w.shape[0]
    w = w.reshape(r, MLA_HEADS, MLA_QK)
    nope, x1, x2 = w[..., :MLA_NOPE], w[..., MLA_NOPE:MLA_NOPE + ROPE_HALF], w[..., MLA_NOPE + ROPE_HALF:]
    z = jnp.zeros((r, MLA_HEADS, SLOT - MLA_QK), w.dtype)
    plain = jnp.concatenate([nope, x1, x2, z], axis=-1).reshape(r, MLA_HEADS * SLOT)
    swapped = jnp.concatenate([jnp.zeros_like(nope), x2, x1, z], axis=-1).reshape(r, MLA_HEADS * SLOT)
    return jnp.concatenate([plain, swapped], axis=1).astype(BF16)


def _layout_w_ukv(w):
    r = w.shape[0]
    w = w.reshape(r, MLA_HEADS, MLA_NOPE + MLA_V)
    k_nope, v = w[..., :MLA_NOPE], w[..., MLA_NOPE:]
    k_slots = jnp.concatenate([k_nope, jnp.zeros((r, MLA_HEADS, SLOT - MLA_NOPE), w.dtype)], axis=-1)
    return jnp.concatenate([k_slots.reshape(r, MLA_HEADS * SLOT), v.reshape(r, MLA_HEADS * MLA_V)],
                           axis=1).astype(BF16)


def _layout_w_out(w):
    n_ab = SB_HEADS * HEAD_DIM + MLA_HEADS * MLA_V
    sw = w[n_ab:].reshape(SW_HEADS, HEAD_DIM, w.shape[1])[jnp.array([0, 3, 1, 4, 2, 5])]
    return jnp.concatenate([w[:n_ab], sw.reshape(SW_HEADS * HEAD_DIM, w.shape[1])], axis=0).astype(BF16)


def kernel(x, c, positions, rel_table, norm1_g, norm2_g, w_ada, b_ada, w_in, mla_cq_g, w_uq, mla_ckv_g,
           w_ukv, mla_qn_g, mla_kn_g, sw_qn_g, sw_kn_g, sw_sinks, w_out, w_up, conv_w, conv_b, w_down):
    depth = w_in.shape[0]
    b = x.shape[0]
    mods = _mods(c, w_ada, b_ada).reshape(depth, b, 6, D_MODEL)
    cos_t, sin_t = _rope_tables(positions)
    bias = _window_bias(rel_table)
    row = lambda v: v.reshape(1, -1).astype(F32)
    two = lambda v: jnp.concatenate([v, v]).reshape(1, SLOT).astype(F32)
    for l in range(depth):
        qkv = _prep(x, mods[l], row(norm1_g[l]), _layout_w_in(w_in[l]), row(mla_cq_g[l]),
                    _layout_w_uq(w_uq[l]), row(mla_ckv_g[l]), _layout_w_ukv(w_ukv[l]),
                    _slot_gain(mla_qn_g[l], False), _slot_gain(mla_qn_g[l], True),
                    _slot_gain(mla_kn_g[l], False), _slot_gain(mla_kn_g[l], True),
                    two(sw_qn_g[l]), two(sw_kn_g[l]), cos_t, sin_t)
        sbq, sbk, sbvt, mq, mk, mvt, swq, swk, swvt = qkv
        o_a = _sb_attention(sbq, sbk, sbvt)
        o_b = _mla_attention(mq, mk, mvt)
        o_c = _swa_attention(sw_sinks[l], swq, swk, swvt, bias)
        x = _mlp(x, mods[l], o_a, o_b, o_c, _layout_w_out(w_out[l]), row(norm2_g[l]),
                 w_up[l].astype(BF16), conv_w[l], row(conv_b[l]), w_down[l].astype(BF16))
    return x
```

```python
import functools
import math

import numpy as np
import jax
import jax.numpy as jnp
from jax import lax
from jax.experimental import pallas as pl
from jax.experimental.pallas import tpu as pltpu

F32 = jnp.float32
BF16 = jnp.bfloat16

D_MODEL = 1024
HEAD_DIM = 64
SB_HEADS = 4
MLA_HEADS = 6
MLA_Q_RANK = 256
MLA_KV_RANK = 128
MLA_NOPE = 64
MLA_ROPE = 32
MLA_V = 64
MLA_QK = MLA_NOPE + MLA_ROPE
ROPE_THETA = 10000.0
SW_HEADS = 6
SW_KV_HEADS = 2
WINDOW = 128
REL_BUCKETS = 32
REL_MAX_DIST = 128
D_FF = 2816
CONV_W = 3
EPS = 1e-6
NEG = -1e30

LANES = 128
SLOT = LANES
HALF = SLOT // 2

C_SBQ, C_SBK, C_SBV = 0, 256, 512
C_CQ = 768
C_CKV = 1024
C_KROPE = 1152
C_KROPE_SW = 1280
C_SWQ = 1408
C_SWK = 1792
C_SWV = 1920
N_IN = 2048

ROPE_LO = MLA_NOPE
ROPE_HALF = MLA_ROPE // 2

TM_PREP = 512
TM_MLP = 512
TQ = 256
TK = 256
FF_CHUNK = 256
CARRY_ROWS = 8

VMEM_LIMIT = 56 * 1024 * 1024


def _cparams(n_axes):
    return pltpu.CompilerParams(dimension_semantics=("arbitrary",) * n_axes,
                                vmem_limit_bytes=VMEM_LIMIT)


def _rms(x, n):
    return x * lax.rsqrt(jnp.sum(x * x, axis=-1, keepdims=True) * (1.0 / n) + EPS)


def _nt_dot(a, b):
    return lax.dot_general(a, b, (((1,), (1,)), ((), ())), preferred_element_type=F32)


def _dot(a, b):
    return jnp.dot(a, b, preferred_element_type=F32)


def _mods_kernel(c_ref, w_ref, b_ref, o_ref):
    c = c_ref[...]
    a = (c * jax.nn.sigmoid(c)).astype(BF16)
    o_ref[0] = _dot(a, w_ref[0].astype(BF16)) + b_ref[0]


def _mods(c, w_ada, b_ada):
    depth, d, n = w_ada.shape
    b = c.shape[0]
    tn = 1536
    return pl.pallas_call(
        _mods_kernel,
        out_shape=jax.ShapeDtypeStruct((depth, b, n), F32),
        grid=(depth, n // tn),
        in_specs=[pl.BlockSpec((b, d), lambda l, j: (0, 0)),
                  pl.BlockSpec((1, d, tn), lambda l, j: (l, 0, j)),
                  pl.BlockSpec((1, 1, tn), lambda l, j: (l, 0, j))],
        out_specs=pl.BlockSpec((1, b, tn), lambda l, j: (l, 0, j)),
        compiler_params=_cparams(2),
        name="adaln_mods",
    )(c, w_ada, b_ada.reshape(depth, 1, n))


def _rope_kernel(pos_ref, invf_ref, cos_ref, sin_ref):
    pos = pos_ref[0].astype(F32)
    ang = invf_ref[...] * pos
    c = jnp.cos(ang)
    s = jnp.sin(ang)
    tm = pos.shape[1]
    ones = jnp.ones((ROPE_LO, tm), F32)
    zeros = jnp.zeros((ROPE_LO, tm), F32)
    pad = SLOT - ROPE_LO - MLA_ROPE
    cos_t = jnp.concatenate([ones, c, c, jnp.ones((pad, tm), F32)], axis=0)
    sin_t = jnp.concatenate([zeros, -s, s, jnp.zeros((pad, tm), F32)], axis=0)
    cos_ref[0] = cos_t.T
    sin_ref[0] = sin_t.T


def _rope_tables(positions):
    b, s = positions.shape
    tm = 512
    half = ROPE_HALF
    inv_freq = jnp.power(ROPE_THETA, -jnp.arange(half, dtype=F32) / half).reshape(half, 1)
    out = jax.ShapeDtypeStruct((b, s, SLOT), F32)
    return pl.pallas_call(
        _rope_kernel,
        out_shape=(out, out),
        grid=(b, s // tm),
        in_specs=[pl.BlockSpec((1, 1, tm), lambda i, j: (i, 0, j)),
                  pl.BlockSpec((half, 1), lambda i, j: (0, 0))],
        out_specs=(pl.BlockSpec((1, tm, SLOT), lambda i, j: (i, j, 0)),
                   pl.BlockSpec((1, tm, SLOT), lambda i, j: (i, j, 0))),
        compiler_params=_cparams(2),
        name="rope_tables",
    )(positions.reshape(b, 1, s), inv_freq)


def _t5_bucket(dist):
    max_exact = REL_BUCKETS // 2
    n = jnp.maximum(dist, 0)
    nf = jnp.maximum(n, 1).astype(F32)
    large = max_exact + (jnp.log(nf / max_exact) / math.log(REL_MAX_DIST / max_exact)
                         * (REL_BUCKETS - max_exact)).astype(jnp.int32)
    large = jnp.minimum(large, REL_BUCKETS - 1)
    return jnp.where(n < max_exact, n, large)


def _bias_kernel(tab_ref, bucket_ref, o_ref):
    bucket = bucket_ref[...]
    for h in range(SW_HEADS):
        acc = jnp.zeros(bucket.shape, F32)
        for bkt in range(REL_BUCKETS):
            acc = jnp.where(bucket == bkt, tab_ref[bkt, h], acc)
        o_ref[h] = jnp.where(bucket >= 0, acc, NEG)


def _window_bias(rel_table):
    nk = WINDOW + TQ
    key = jnp.arange(nk)[:, None]
    qry = jnp.arange(TQ)[None, :]
    dist = qry + WINDOW - key
    valid = (dist >= 0) & (dist < WINDOW)
    bucket = jnp.where(valid, _t5_bucket(dist), -1).astype(jnp.int32)
    return pl.pallas_call(
        _bias_kernel,
        out_shape=jax.ShapeDtypeStruct((SW_HEADS, nk, TQ), F32),
        in_specs=[pl.BlockSpec(memory_space=pltpu.SMEM),
                  pl.BlockSpec(memory_space=pltpu.VMEM)],
        out_specs=pl.BlockSpec(memory_space=pltpu.VMEM),
        name="window_bias",
    )(rel_table, bucket)


def _half_rms(x, gain, scale):
    lo = lax.broadcasted_iota(jnp.int32, x.shape, 1) < HALF
    sq = x * x
    s_lo = jnp.sum(jnp.where(lo, sq, 0.0), axis=-1, keepdims=True)
    s_hi = jnp.sum(jnp.where(lo, 0.0, sq), axis=-1, keepdims=True)
    r = jnp.where(lo, lax.rsqrt(s_lo * (1.0 / HEAD_DIM) + EPS), lax.rsqrt(s_hi * (1.0 / HEAD_DIM) + EPS))
    return x * r * (gain * scale)


def _prep_kernel(x_ref, mod_ref, n1g_ref, win_ref, cqg_ref, wuq_ref, ckvg_ref, wukv_ref,
                 gq_ref, gqs_ref, gk_ref, gks_ref, swqg_ref, swkg_ref, cos_ref, sin_ref,
                 sbq_ref, sbk_ref, sbvt_ref, mq_ref, mk_ref, mvt_ref, swq_ref, swk_ref, swvt_ref):
    x = x_ref[0]
    mod = mod_ref[0]
    shift1, scale1 = mod[0:1], mod[1:2]
    h = _rms(x, D_MODEL) * n1g_ref[...] * (1.0 + scale1) + shift1
    proj = _dot(h.astype(BF16), win_ref[...])

    sbq_ref[0] = (proj[:, C_SBQ:C_SBQ + 256] * HEAD_DIM ** -0.5).astype(BF16)
    sbk_ref[0] = proj[:, C_SBK:C_SBK + 256].astype(BF16)
    sbvt_ref[0] = proj[:, C_SBV:C_SBV + 256].T.astype(BF16)

    cos = cos_ref[0]
    sin = sin_ref[0]

    cq = proj[:, C_CQ:C_CQ + MLA_Q_RANK]
    cqn = (_rms(cq, MLA_Q_RANK) * cqg_ref[...]).astype(BF16)
    qraw = _dot(cqn, wuq_ref[...])
    nq = MLA_HEADS * SLOT
    q_scale = MLA_QK ** -0.5
    for hd in range(MLA_HEADS):
        slot = qraw[:, hd * SLOT:(hd + 1) * SLOT]
        swapped = qraw[:, nq + hd * SLOT:nq + (hd + 1) * SLOT]
        r = lax.rsqrt(jnp.sum(slot * slot, axis=-1, keepdims=True) * (1.0 / MLA_QK) + EPS) * q_scale
        q = (slot * r * gq_ref[...]) * cos + (swapped * r * gqs_ref[...]) * sin
        mq_ref[0, :, hd * SLOT:(hd + 1) * SLOT] = q.astype(BF16)

    ckv = proj[:, C_CKV:C_CKV + MLA_KV_RANK]
    ckvn = (_rms(ckv, MLA_KV_RANK) * ckvg_ref[...]).astype(BF16)
    kv = _dot(ckvn, wukv_ref[...])
    mvt_ref[0] = kv[:, nq:nq + MLA_HEADS * MLA_V].T.astype(BF16)
    krope = proj[:, C_KROPE:C_KROPE + SLOT]
    krope_sw = proj[:, C_KROPE_SW:C_KROPE_SW + SLOT]
    for hd in range(MLA_HEADS):
        slot = kv[:, hd * SLOT:(hd + 1) * SLOT] + krope
        r = lax.rsqrt(jnp.sum(slot * slot, axis=-1, keepdims=True) * (1.0 / MLA_QK) + EPS)
        k = (slot * r * gk_ref[...]) * cos + (krope_sw * r * gks_ref[...]) * sin
        mk_ref[0, :, hd * SLOT:(hd + 1) * SLOT] = k.astype(BF16)

    for g in range(SW_HEADS // 2):
        xq = proj[:, C_SWQ + g * SLOT:C_SWQ + (g + 1) * SLOT]
        swq_ref[0, :, g * SLOT:(g + 1) * SLOT] = _half_rms(xq, swqg_ref[...], HEAD_DIM ** -0.5).astype(BF16)
    swk_ref[0] = _half_rms(proj[:, C_SWK:C_SWK + SLOT], swkg_ref[...], 1.0).astype(BF16)
    swvt_ref[0] = proj[:, C_SWV:C_SWV + SLOT].T.astype(BF16)


def _prep(x, mods, n1g, w_in, cqg, w_uq, ckvg, w_ukv, gq, gqs, gk, gks, swqg, swkg, cos_t, sin_t):
    b, s, d = x.shape
    tm = TM_PREP
    row = lambda w: pl.BlockSpec((1, tm, w), lambda i, j: (i, j, 0))
    colt = lambda w: pl.BlockSpec((1, w, tm), lambda i, j: (i, 0, j))
    full = lambda a: pl.BlockSpec(a.shape, lambda i, j: (0,) * a.ndim)
    act = lambda w: jax.ShapeDtypeStruct((b, s, w), BF16)
    actt = lambda w: jax.ShapeDtypeStruct((b, w, s), BF16)
    return pl.pallas_call(
        _prep_kernel,
        out_shape=(act(256), act(256), actt(256), act(768), act(768), actt(384),
                   act(384), act(128), actt(128)),
        grid=(b, s // tm),
        in_specs=[row(d), pl.BlockSpec((1, 6, d), lambda i, j: (i, 0, 0)), full(n1g), full(w_in),
                  full(cqg), full(w_uq), full(ckvg), full(w_ukv), full(gq), full(gqs), full(gk),
                  full(gks), full(swqg), full(swkg), row(SLOT), row(SLOT)],
        out_specs=(row(256), row(256), colt(256), row(768), row(768), colt(384),
                   row(384), row(128), colt(128)),
        compiler_params=_cparams(2),
        name="prep_qkv",
    )(x, mods, n1g, w_in, cqg, w_uq, ckvg, w_ukv, gq, gqs, gk, gks, swqg, swkg, cos_t, sin_t)


def _half_mask(q, half):
    lane = lax.broadcasted_iota(jnp.int32, q.shape, 1)
    keep = (lane < HALF) if half == 0 else (lane >= HALF)
    return jnp.where(keep, q, jnp.zeros_like(q))


def _store_pair(o_ref, g, out_lo, out_hi):
    pair = jnp.concatenate([out_lo, out_hi], axis=0)
    o_ref[0, :, g * SLOT:(g + 1) * SLOT] = pair.T.astype(o_ref.dtype)


def _softplus(z):
    return jnp.maximum(z, 0.0) + jnp.log1p(jnp.exp(-jnp.abs(z)))


def _suffix_sum(tri, log_keep):
    hi = log_keep.astype(BF16)
    lo = (log_keep - hi.astype(F32)).astype(BF16)
    return _dot(tri, hi) + _dot(tri, lo)


def _sb_kernel(q_ref, k_ref, vt_ref, tri_ref, o_ref):
    qi = pl.program_id(1)
    row = lax.broadcasted_iota(jnp.int32, (TK, TQ), 0)
    col = lax.broadcasted_iota(jnp.int32, (TK, TQ), 1)
    strict = row < col

    def step(start, carry, diagonal):
        heads = range(SB_HEADS)
        z = []
        for hd in heads:
            g, half = divmod(hd, 2)
            q = _half_mask(q_ref[0, :, g * SLOT:(g + 1) * SLOT], half)
            k = k_ref[0, pl.ds(start, TK), g * SLOT:(g + 1) * SLOT]
            z.append(_nt_dot(k, q))
        log_keep = [-_softplus(z[hd]) for hd in heads]
        if diagonal:
            log_keep = [jnp.where(strict, lk, 0.0) for lk in log_keep]
        incl = [_suffix_sum(tri_ref[...], log_keep[hd]) for hd in heads]
        w = [jnp.exp(z[hd] + incl[hd] + carry[hd][1]) for hd in heads]
        if diagonal:
            w = [jnp.where(strict, wh, 0.0) for wh in w]
        pv = [_dot(vt_ref[0, hd * HEAD_DIM:(hd + 1) * HEAD_DIM, pl.ds(start, TK)], w[hd].astype(BF16))
              for hd in heads]
        return tuple((carry[hd][0] + pv[hd], carry[hd][1] + incl[hd][0:1, :]) for hd in heads)

    init = tuple((jnp.zeros((HEAD_DIM, TQ), F32), jnp.zeros((1, TQ), F32)) for _ in range(SB_HEADS))
    carry = step(pl.multiple_of(qi * TK, TK), init, True)
    carry = lax.fori_loop(
        0, qi, lambda i, c: step(pl.multiple_of((qi - 1 - i) * TK, TK), c, False), carry)
    for g in range(SB_HEADS // 2):
        _store_pair(o_ref, g, carry[2 * g][0], carry[2 * g + 1][0])


def _sb_attention(q, k, vt):
    b, s, w = q.shape
    tri = (jnp.arange(TK)[None, :] >= jnp.arange(TK)[:, None]).astype(BF16)
    return pl.pallas_call(
        _sb_kernel,
        out_shape=jax.ShapeDtypeStruct((b, s, w), BF16),
        grid=(b, s // TQ),
        in_specs=[pl.BlockSpec((1, TQ, w), lambda i, j: (i, j, 0)),
                  pl.BlockSpec((1, s, w), lambda i, j: (i, 0, 0)),
                  pl.BlockSpec((1, w, s), lambda i, j: (i, 0, 0)),
                  pl.BlockSpec((TK, TK), lambda i, j: (0, 0))],
        out_specs=pl.BlockSpec((1, TQ, w), lambda i, j: (i, j, 0)),
        compiler_params=_cparams(2),
        name="sb_attention",
    )(q, k, vt, tri)


def _mla_kernel(q_ref, k_ref, vt_ref, o_ref):
    qi = pl.program_id(1)
    row = lax.broadcasted_iota(jnp.int32, (TK, TQ), 0)
    col = lax.broadcasted_iota(jnp.int32, (TK, TQ), 1)
    causal = row <= col

    def step(start, carry, masked):
        heads = range(MLA_HEADS)
        scores = []
        for hd in heads:
            q = q_ref[0, :, hd * SLOT:(hd + 1) * SLOT]
            k = k_ref[0, pl.ds(start, TK), hd * SLOT:(hd + 1) * SLOT]
            s = _nt_dot(k, q)
            scores.append(jnp.where(causal, s, NEG) if masked else s)
        m_new = [jnp.maximum(carry[hd][0], jnp.max(scores[hd], axis=0, keepdims=True)) for hd in heads]
        alpha = [jnp.exp(carry[hd][0] - m_new[hd]) for hd in heads]
        p = [jnp.exp(scores[hd] - m_new[hd]) for hd in heads]
        l = [alpha[hd] * carry[hd][1] + jnp.sum(p[hd], axis=0, keepdims=True) for hd in heads]
        pv = [_dot(vt_ref[0, hd * MLA_V:(hd + 1) * MLA_V, pl.ds(start, TK)], p[hd].astype(BF16))
              for hd in heads]
        return tuple((m_new[hd], l[hd], alpha[hd] * carry[hd][2] + pv[hd]) for hd in heads)

    init = tuple((jnp.full((1, TQ), NEG, F32), jnp.zeros((1, TQ), F32), jnp.zeros((MLA_V, TQ), F32))
                 for _ in range(MLA_HEADS))
    carry = lax.fori_loop(0, qi, lambda j, c: step(pl.multiple_of(j * TK, TK), c, False), init)
    carry = step(pl.multiple_of(qi * TK, TK), carry, True)
    outs = [acc * (1.0 / l) for (_, l, acc) in carry]
    for g in range(MLA_HEADS // 2):
        _store_pair(o_ref, g, outs[2 * g], outs[2 * g + 1])


def _mla_attention(q, k, vt):
    b, s, w = q.shape
    wv = vt.shape[1]
    return pl.pallas_call(
        _mla_kernel,
        out_shape=jax.ShapeDtypeStruct((b, s, wv), BF16),
        grid=(b, s // TQ),
        in_specs=[pl.BlockSpec((1, TQ, w), lambda i, j: (i, j, 0)),
                  pl.BlockSpec((1, s, w), lambda i, j: (i, 0, 0)),
                  pl.BlockSpec((1, wv, s), lambda i, j: (i, 0, 0))],
        out_specs=pl.BlockSpec((1, TQ, wv), lambda i, j: (i, j, 0)),
        compiler_params=_cparams(2),
        name="mla_attention",
    )(q, k, vt)


def _swa_kernel(sink_ref, q_ref, kp_ref, kc_ref, vtp_ref, vtc_ref, bias_ref, o_ref):
    qi = pl.program_id(1)
    k = jnp.concatenate([kp_ref[0], kc_ref[0]], axis=0)
    vt = jnp.concatenate([vtp_ref[0], vtc_ref[0]], axis=1)
    nk = WINDOW + TQ
    before_start = (lax.broadcasted_iota(jnp.int32, (nk, TQ), 0) < WINDOW) & (qi == 0)
    slots = range(SW_HEADS)
    head = [i // 2 + (i % 2) * (SW_HEADS // SW_KV_HEADS) for i in slots]
    s = []
    for i in slots:
        q = _half_mask(q_ref[0, :, (i // 2) * SLOT:(i // 2 + 1) * SLOT], i % 2)
        si = _nt_dot(k, q) + bias_ref[head[i]]
        s.append(jnp.where(before_start, NEG, si))
    m = [jnp.maximum(jnp.max(s[i], axis=0, keepdims=True), sink_ref[head[i]]) for i in slots]
    p = [jnp.exp(s[i] - m[i]) for i in slots]
    l = [jnp.sum(p[i], axis=0, keepdims=True) + jnp.exp(sink_ref[head[i]] - m[i]) for i in slots]
    acc = [_dot(vt[(i % 2) * HEAD_DIM:(i % 2 + 1) * HEAD_DIM, :], p[i].astype(BF16)) for i in slots]
    outs = [acc[i] * (1.0 / l[i]) for i in slots]
    for g in range(SW_HEADS // 2):
        _store_pair(o_ref, g, outs[2 * g], outs[2 * g + 1])


def _swa_attention(sinks, q, k, vt, bias):
    b, s, w = q.shape
    prev = lambda j: jnp.maximum(2 * j - 1, 0)
    return pl.pallas_call(
        _swa_kernel,
        out_shape=jax.ShapeDtypeStruct((b, s, w), BF16),
        grid=(b, s // TQ),
        in_specs=[pl.BlockSpec(memory_space=pltpu.SMEM),
                  pl.BlockSpec((1, TQ, w), lambda i, j: (i, j, 0)),
                  pl.BlockSpec((1, WINDOW, SLOT), lambda i, j: (i, prev(j), 0)),
                  pl.BlockSpec((1, TQ, SLOT), lambda i, j: (i, j, 0)),
                  pl.BlockSpec((1, SLOT, WINDOW), lambda i, j: (i, 0, prev(j))),
                  pl.BlockSpec((1, SLOT, TQ), lambda i, j: (i, 0, j)),
                  pl.BlockSpec(bias.shape, lambda i, j: (0, 0, 0))],
        out_specs=pl.BlockSpec((1, TQ, w), lambda i, j: (i, j, 0)),
        compiler_params=_cparams(2),
        name="swa_attention",
    )(sinks, q, k, k, vt, vt, bias)


def _mlp_kernel(x_ref, mod_ref, oa_ref, ob_ref, oc_ref, wo_ref, n2g_ref, wup_ref, cw_ref, cb_ref,
                wdn_ref, out_ref, carry_ref, acc_ref):
    si = pl.program_id(1)
    tm = x_ref.shape[1]
    mod = mod_ref[0]
    gate1, shift2, scale2, gate2 = mod[2:3], mod[3:4], mod[4:5], mod[5:6]
    na, nb = oa_ref.shape[2], ob_ref.shape[2]
    att = (_dot(oa_ref[0], wo_ref[0:na, :]) + _dot(ob_ref[0], wo_ref[na:na + nb, :])
           + _dot(oc_ref[0], wo_ref[na + nb:, :]))
    x1 = x_ref[0] + gate1 * att
    h2 = (_rms(x1, D_MODEL) * n2g_ref[...] * (1.0 + scale2) + shift2).astype(BF16)

    @pl.when(si == 0)
    def _():
        carry_ref[...] = jnp.zeros_like(carry_ref)

    def conv(col):
        cols = slice(col, col + FF_CHUNK)
        u = _dot(h2, wup_ref[:, cols])
        prev = carry_ref[:, cols]
        carry_ref[:, cols] = u[tm - CARRY_ROWS:, :]
        ext = jnp.concatenate([prev, u], axis=0)
        u1 = ext[CARRY_ROWS - 1:CARRY_ROWS - 1 + tm, :]
        u2 = ext[CARRY_ROWS - 2:CARRY_ROWS - 2 + tm, :]
        cw = cw_ref[:, cols]
        return u * cw[2:3] + u1 * cw[1:2] + u2 * cw[0:1] + cb_ref[:, cols]

    for c in range(D_FF // FF_CHUNK):
        gate = conv(c * FF_CHUNK)
        val = conv(D_FF + c * FF_CHUNK)
        a = (gate * jax.nn.sigmoid(gate) * val).astype(BF16)
        part = _dot(a, wdn_ref[c * FF_CHUNK:(c + 1) * FF_CHUNK, :])
        if c == 0:
            acc_ref[...] = part
        else:
            acc_ref[...] += part
    out_ref[0] = x1 + gate2 * acc_ref[...]


def _mlp(x, mods, oa, ob, oc, w_out, n2g, w_up, conv_w, conv_b, w_down):
    b, s, d = x.shape
    tm = TM_MLP
    row = lambda w: pl.BlockSpec((1, tm, w), lambda i, j: (i, j, 0))
    const = lambda a: pl.BlockSpec(a.shape, lambda i, j: (0,) * a.ndim, pipeline_mode=pl.Buffered(1))
    return pl.pallas_call(
        _mlp_kernel,
        out_shape=jax.ShapeDtypeStruct((b, s, d), F32),
        grid=(b, s // tm),
        in_specs=[row(d), pl.BlockSpec((1, 6, d), lambda i, j: (i, 0, 0)),
                  row(oa.shape[2]), row(ob.shape[2]), row(oc.shape[2]),
                  const(w_out), const(n2g), const(w_up), const(conv_w), const(conv_b), const(w_down)],
        out_specs=row(d),
        scratch_shapes=[pltpu.VMEM((CARRY_ROWS, 2 * D_FF), F32), pltpu.VMEM((tm, d), F32)],
        compiler_params=_cparams(2),
        name="outproj_mlp",
    )(x, mods, oa, ob, oc, w_out, n2g, w_up, conv_w, conv_b, w_down)


def _slot_gain(g, swap):
    z = jnp.zeros((SLOT - MLA_QK,), F32)
    lo, hi = g[ROPE_LO:ROPE_LO + ROPE_HALF], g[ROPE_LO + ROPE_HALF:MLA_QK]
    if swap:
        return jnp.concatenate([jnp.zeros((MLA_NOPE,), F32), hi, lo, z]).reshape(1, SLOT)
    return jnp.concatenate([g, z]).reshape(1, SLOT)


def _layout_w_in(w):
    d = w.shape[0]
    z = lambda n: jnp.zeros((d, n), w.dtype)
    kr = w[:, 1152:1184]
    swq = w[:, 1184:1568].reshape(d, SW_HEADS, HEAD_DIM)
    order = [0, 3, 1, 4, 2, 5]
    swq = swq[:, order, :].reshape(d, SW_HEADS * HEAD_DIM)
    cols = [w[:, 0:1152],
            z(MLA_NOPE), kr, z(SLOT - MLA_QK),
            z(MLA_NOPE), kr[:, ROPE_HALF:], kr[:, :ROPE_HALF], z(SLOT - MLA_QK),
            swq, w[:, 1568:1824]]
    return jnp.concatenate(cols, axis=1).astype(BF16)


def _layout_w_uq(w):
    r = w.shape[0]
    w = w.reshape(r, MLA_HEADS, MLA_QK)
    nope, x1, x2 = w[..., :MLA_NOPE], w[..., MLA_NOPE:MLA_NOPE + ROPE_HALF], w[..., MLA_NOPE + ROPE_HALF:]
    z = jnp.zeros((r, MLA_HEADS, SLOT - MLA_QK), w.dtype)
    plain = jnp.concatenate([nope, x1, x2, z], axis=-1).reshape(r, MLA_HEADS * SLOT)
    swapped = jnp.concatenate([jnp.zeros_like(nope), x2, x1, z], axis=-1).reshape(r, MLA_HEADS * SLOT)
    return jnp.concatenate([plain, swapped], axis=1).astype(BF16)


def _layout_w_ukv(w):
    r = w.shape[0]
    w = w.reshape(r, MLA_HEADS, MLA_NOPE + MLA_V)
    k_nope, v = w[..., :MLA_NOPE], w[..., MLA_NOPE:]
    k_slots = jnp.concatenate([k_nope, jnp.zeros((r, MLA_HEADS, SLOT - MLA_NOPE), w.dtype)], axis=-1)
    return jnp.concatenate([k_slots.reshape(r, MLA_HEADS * SLOT), v.reshape(r, MLA_HEADS * MLA_V)],
                           axis=1).astype(BF16)


def _layout_w_out(w):
    n_ab = SB_HEADS * HEAD_DIM + MLA_HEADS * MLA_V
    sw = w[n_ab:].reshape(SW_HEADS, HEAD_DIM, w.shape[1])[jnp.array([0, 3, 1, 4, 2, 5])]
    return jnp.concatenate([w[:n_ab], sw.reshape(SW_HEADS * HEAD_DIM, w.shape[1])], axis=0).astype(BF16)


def kernel(x, c, positions, rel_table, norm1_g, norm2_g, w_ada, b_ada, w_in, mla_cq_g, w_uq, mla_ckv_g,
           w_ukv, mla_qn_g, mla_kn_g, sw_qn_g, sw_kn_g, sw_sinks, w_out, w_up, conv_w, conv_b, w_down):
    depth = w_in.shape[0]
    b = x.shape[0]
    mods = _mods(c, w_ada, b_ada).reshape(depth, b, 6, D_MODEL)
    cos_t, sin_t = _rope_tables(positions)
    bias = _window_bias(rel_table)
    row = lambda v: v.reshape(1, -1).astype(F32)
    two = lambda v: jnp.concatenate([v, v]).reshape(1, SLOT).astype(F32)
    for l in range(depth):
        qkv = _prep(x, mods[l], row(norm1_g[l]), _layout_w_in(w_in[l]), row(mla_cq_g[l]),
                    _layout_w_uq(w_uq[l]), row(mla_ckv_g[l]), _layout_w_ukv(w_ukv[l]),
                    _slot_gain(mla_qn_g[l], False), _slot_gain(mla_qn_g[l], True),
                    _slot_gain(mla_kn_g[l], False), _slot_gain(mla_kn_g[l], True),
                    two(sw_qn_g[l]), two(sw_kn_g[l]), cos_t, sin_t)
        sbq, sbk, sbvt, mq, mk, mvt, swq, swk, swvt = qkv
        o_a = _sb_attention(sbq, sbk, sbvt)
        o_b = _mla_attention(mq, mk, mvt)
        o_c = _swa_attention(sw_sinks[l], swq, swk, swvt, bias)
        x = _mlp(x, mods[l], o_a, o_b, o_c, _layout_w_out(w_out[l]), row(norm2_g[l]),
                 w_up[l].astype(BF16), conv_w[l], row(conv_b[l]), w_down[l].astype(BF16))
    return x
```

```python
import functools
import math

import numpy as np
import jax
import jax.numpy as jnp
from jax import lax
from jax.experimental import pallas as pl
from jax.experimental.pallas import tpu as pltpu

F32 = jnp.float32
BF16 = jnp.bfloat16

D_MODEL = 1024
HEAD_DIM = 64
SB_HEADS = 4
MLA_HEADS = 6
MLA_Q_RANK = 256
MLA_KV_RANK = 128
MLA_NOPE = 64
MLA_ROPE = 32
MLA_V = 64
MLA_QK = MLA_NOPE + MLA_ROPE
ROPE_THETA = 10000.0
SW_HEADS = 6
SW_KV_HEADS = 2
WINDOW = 128
REL_BUCKETS = 32
REL_MAX_DIST = 128
D_FF = 2816
CONV_W = 3
EPS = 1e-6
NEG = -1e30

LANES = 128
SLOT = LANES
HALF = SLOT // 2

C_SBQ, C_SBK, C_SBV = 0, 256, 512
C_CQ = 768
C_CKV = 1024
C_KROPE = 1152
C_KROPE_SW = 1280
C_SWK = 1408
C_SWQ = 1536
C_SWV = 1920
N_IN = 2048

ROPE_LO = MLA_NOPE
ROPE_HALF = MLA_ROPE // 2

TM_PREP = 512
TM_MLP = 512
TQ = 512
TK = 256
TQ_SW = 256
LOG2E = math.log2(math.e)
FF_CHUNK = 256
CARRY_ROWS = 8

VMEM_LIMIT = 56 * 1024 * 1024


def _cparams(n_axes):
    return pltpu.CompilerParams(dimension_semantics=("arbitrary",) * n_axes,
                                vmem_limit_bytes=VMEM_LIMIT)


def _rms(x, n):
    return x * lax.rsqrt(jnp.sum(x * x, axis=-1, keepdims=True) * (1.0 / n) + EPS)


def _nt_dot(a, b):
    return lax.dot_general(a, b, (((1,), (1,)), ((), ())), preferred_element_type=F32)


def _dot(a, b):
    return jnp.dot(a, b, preferred_element_type=F32)


def _mods_kernel(c_ref, w_ref, b_ref, o_ref):
    c = c_ref[...]
    a = (c * jax.nn.sigmoid(c)).astype(BF16)
    o_ref[0] = _dot(a, w_ref[0].astype(BF16)) + b_ref[0]


def _mods(c, w_ada, b_ada):
    depth, d, n = w_ada.shape
    b = c.shape[0]
    tn = 1536
    return pl.pallas_call(
        _mods_kernel,
        out_shape=jax.ShapeDtypeStruct((depth, b, n), F32),
        grid=(depth, n // tn),
        in_specs=[pl.BlockSpec((b, d), lambda l, j: (0, 0)),
                  pl.BlockSpec((1, d, tn), lambda l, j: (l, 0, j)),
                  pl.BlockSpec((1, 1, tn), lambda l, j: (l, 0, j))],
        out_specs=pl.BlockSpec((1, b, tn), lambda l, j: (l, 0, j)),
        compiler_params=_cparams(2),
        name="adaln_mods",
    )(c, w_ada, b_ada.reshape(depth, 1, n))


def _rope_kernel(pos_ref, invf_ref, cos_ref, sin_ref):
    pos = pos_ref[0].astype(F32)
    ang = invf_ref[...] * pos
    c = jnp.cos(ang)
    s = jnp.sin(ang)
    tm = pos.shape[1]
    ones = jnp.ones((ROPE_LO, tm), F32)
    zeros = jnp.zeros((ROPE_LO, tm), F32)
    pad = SLOT - ROPE_LO - MLA_ROPE
    cos_t = jnp.concatenate([ones, c, c, jnp.ones((pad, tm), F32)], axis=0)
    sin_t = jnp.concatenate([zeros, -s, s, jnp.zeros((pad, tm), F32)], axis=0)
    cos_ref[0] = cos_t.T
    sin_ref[0] = sin_t.T


def _rope_tables(positions):
    b, s = positions.shape
    tm = 512
    half = ROPE_HALF
    inv_freq = jnp.power(ROPE_THETA, -jnp.arange(half, dtype=F32) / half).reshape(half, 1)
    out = jax.ShapeDtypeStruct((b, s, SLOT), F32)
    return pl.pallas_call(
        _rope_kernel,
        out_shape=(out, out),
        grid=(b, s // tm),
        in_specs=[pl.BlockSpec((1, 1, tm), lambda i, j: (i, 0, j)),
                  pl.BlockSpec((half, 1), lambda i, j: (0, 0))],
        out_specs=(pl.BlockSpec((1, tm, SLOT), lambda i, j: (i, j, 0)),
                   pl.BlockSpec((1, tm, SLOT), lambda i, j: (i, j, 0))),
        compiler_params=_cparams(2),
        name="rope_tables",
    )(positions.reshape(b, 1, s), inv_freq)


def _t5_bucket(dist):
    max_exact = REL_BUCKETS // 2
    n = jnp.maximum(dist, 0)
    nf = jnp.maximum(n, 1).astype(F32)
    large = max_exact + (jnp.log(nf / max_exact) / math.log(REL_MAX_DIST / max_exact)
                         * (REL_BUCKETS - max_exact)).astype(jnp.int32)
    large = jnp.minimum(large, REL_BUCKETS - 1)
    return jnp.where(n < max_exact, n, large)


def _bias_kernel(tab_ref, bucket_ref, o_ref):
    bucket = bucket_ref[...]
    for h in range(SW_HEADS):
        acc = jnp.zeros(bucket.shape, F32)
        for bkt in range(REL_BUCKETS):
            acc = jnp.where(bucket == bkt, tab_ref[bkt, h], acc)
        o_ref[h] = jnp.where(bucket >= 0, acc * LOG2E, NEG)


def _window_bias(rel_table):
    nk = WINDOW + TQ_SW
    key = jnp.arange(nk)[:, None]
    qry = jnp.arange(TQ_SW)[None, :]
    dist = qry + WINDOW - key
    valid = (dist >= 0) & (dist < WINDOW)
    bucket = jnp.where(valid, _t5_bucket(dist), -1).astype(jnp.int32)
    return pl.pallas_call(
        _bias_kernel,
        out_shape=jax.ShapeDtypeStruct((SW_HEADS, nk, TQ_SW), F32),
        in_specs=[pl.BlockSpec(memory_space=pltpu.SMEM),
                  pl.BlockSpec(memory_space=pltpu.VMEM)],
        out_specs=pl.BlockSpec(memory_space=pltpu.VMEM),
        name="window_bias",
    )(rel_table, bucket)


def _half_rms(x, gain, scale):
    lo = lax.broadcasted_iota(jnp.int32, x.shape, 1) < HALF
    sq = x * x
    s_lo = jnp.sum(jnp.where(lo, sq, 0.0), axis=-1, keepdims=True)
    s_hi = jnp.sum(jnp.where(lo, 0.0, sq), axis=-1, keepdims=True)
    r = jnp.where(lo, lax.rsqrt(s_lo * (1.0 / HEAD_DIM) + EPS), lax.rsqrt(s_hi * (1.0 / HEAD_DIM) + EPS))
    return x * r * (gain * scale)


def _prep_kernel(x_ref, mod_ref, n1g_ref, win_ref, cqg_ref, wuq_ref, ckvg_ref, wukv_ref,
                 gq_ref, gqs_ref, gk_ref, gks_ref, swqg_ref, swkg_ref, cos_ref, sin_ref,
                 sbq_ref, sbk_ref, sbvt_ref, mq_ref, mk_ref, mvt_ref, swq_ref, swk_ref, swvt_ref):
    x = x_ref[0]
    mod = mod_ref[0]
    shift1, scale1 = mod[0:1], mod[1:2]
    h = (_rms(x, D_MODEL) * n1g_ref[...] * (1.0 + scale1) + shift1).astype(BF16)

    proj_mla = _dot(h, win_ref[:, C_CQ:C_SWQ])
    proj_sb = _dot(h, win_ref[:, C_SBQ:C_CQ])
    cq = proj_mla[:, 0:MLA_Q_RANK]
    ckv = proj_mla[:, C_CKV - C_CQ:C_CKV - C_CQ + MLA_KV_RANK]
    krope = proj_mla[:, C_KROPE - C_CQ:C_KROPE - C_CQ + SLOT]
    krope_sw = proj_mla[:, C_KROPE_SW - C_CQ:C_KROPE_SW - C_CQ + SLOT]
    cqn = (_rms(cq, MLA_Q_RANK) * cqg_ref[...]).astype(BF16)
    ckvn = (_rms(ckv, MLA_KV_RANK) * ckvg_ref[...]).astype(BF16)
    qraw = _dot(cqn, wuq_ref[...])
    kv = _dot(ckvn, wukv_ref[...])
    proj_sw = _dot(h, win_ref[:, C_SWQ:N_IN])

    sbq_ref[0] = (proj_sb[:, C_SBQ:C_SBQ + 256] * (HEAD_DIM ** -0.5 * LOG2E)).astype(BF16)
    sbk_ref[0] = proj_sb[:, C_SBK:C_SBK + 256].astype(BF16)
    sbvt_ref[0] = proj_sb[:, C_SBV:C_SBV + 256].T.astype(BF16)

    cos = cos_ref[0]
    sin = sin_ref[0]

    nq = MLA_HEADS * SLOT
    q_scale = MLA_QK ** -0.5 * LOG2E
    q_cos, q_sin = gq_ref[...] * cos, gqs_ref[...] * sin
    for hd in range(MLA_HEADS):
        slot = qraw[:, hd * SLOT:(hd + 1) * SLOT]
        swapped = qraw[:, nq + hd * SLOT:nq + (hd + 1) * SLOT]
        r = lax.rsqrt(jnp.sum(slot * slot, axis=-1, keepdims=True) * (1.0 / MLA_QK) + EPS) * q_scale
        mq_ref[0, :, hd * SLOT:(hd + 1) * SLOT] = ((slot * q_cos + swapped * q_sin) * r).astype(BF16)

    mvt_ref[0] = kv[:, nq:nq + MLA_HEADS * MLA_V].T.astype(BF16)
    k_cos = gk_ref[...] * cos
    k_rot = krope_sw * (gks_ref[...] * sin)
    for hd in range(MLA_HEADS):
        slot = kv[:, hd * SLOT:(hd + 1) * SLOT] + krope
        r = lax.rsqrt(jnp.sum(slot * slot, axis=-1, keepdims=True) * (1.0 / MLA_QK) + EPS)
        mk_ref[0, :, hd * SLOT:(hd + 1) * SLOT] = ((slot * k_cos + k_rot) * r).astype(BF16)

    for g in range(SW_HEADS // 2):
        xq = proj_sw[:, g * SLOT:(g + 1) * SLOT]
        swq_ref[0, :, g * SLOT:(g + 1) * SLOT] = _half_rms(
            xq, swqg_ref[...], HEAD_DIM ** -0.5 * LOG2E).astype(BF16)
    swk_ref[0] = _half_rms(proj_mla[:, C_SWK - C_CQ:C_SWK - C_CQ + SLOT], swkg_ref[...], 1.0).astype(BF16)
    swvt_ref[0] = proj_sw[:, C_SWV - C_SWQ:C_SWV - C_SWQ + SLOT].T.astype(BF16)


def _prep(x, mods, n1g, w_in, cqg, w_uq, ckvg, w_ukv, gq, gqs, gk, gks, swqg, swkg, cos_t, sin_t):
    b, s, d = x.shape
    tm = TM_PREP
    row = lambda w: pl.BlockSpec((1, tm, w), lambda i, j: (i, j, 0))
    colt = lambda w: pl.BlockSpec((1, w, tm), lambda i, j: (i, 0, j))
    full = lambda a: pl.BlockSpec(a.shape, lambda i, j: (0,) * a.ndim)
    act = lambda w: jax.ShapeDtypeStruct((b, s, w), BF16)
    actt = lambda w: jax.ShapeDtypeStruct((b, w, s), BF16)
    return pl.pallas_call(
        _prep_kernel,
        out_shape=(act(256), act(256), actt(256), act(768), act(768), actt(384),
                   act(384), act(128), actt(128)),
        grid=(b, s // tm),
        in_specs=[row(d), pl.BlockSpec((1, 6, d), lambda i, j: (i, 0, 0)), full(n1g), full(w_in),
                  full(cqg), full(w_uq), full(ckvg), full(w_ukv), full(gq), full(gqs), full(gk),
                  full(gks), full(swqg), full(swkg), row(SLOT), row(SLOT)],
        out_specs=(row(256), row(256), colt(256), row(768), row(768), colt(384),
                   row(384), row(128), colt(128)),
        compiler_params=_cparams(2),
        name="prep_qkv",
    )(x, mods, n1g, w_in, cqg, w_uq, ckvg, w_ukv, gq, gqs, gk, gks, swqg, swkg, cos_t, sin_t)


def _half_mask(q, half):
    lane = lax.broadcasted_iota(jnp.int32, q.shape, 1)
    keep = (lane < HALF) if half == 0 else (lane >= HALF)
    return jnp.where(keep, q, jnp.zeros_like(q))


def _store_pair(o_ref, g, out_lo, out_hi):
    pair = jnp.concatenate([out_lo, out_hi], axis=0)
    o_ref[0, :, g * SLOT:(g + 1) * SLOT] = pair.T.astype(o_ref.dtype)


def _softplus2(z):
    return jnp.maximum(z, 0.0) + jnp.log2(1.0 + jnp.exp2(-jnp.abs(z)))


def _neg_suffix_sum(neg_tri, x):
    hi = x.astype(BF16)
    lo = (x - hi.astype(F32)).astype(BF16)
    return _dot(neg_tri, hi) + _dot(neg_tri, lo)


def _sb_kernel(q_ref, k_ref, vt_ref, tri_ref, o_ref):
    qi = pl.program_id(1)
    row = lax.broadcasted_iota(jnp.int32, (TK, TQ), 0)
    col = lax.broadcasted_iota(jnp.int32, (TK, TQ), 1)
    n_diag = TQ // TK

    def step(start, carry, diag):
        heads = range(SB_HEADS)
        strict = None if diag is None else (row + diag * TK) < col
        z = []
        for hd in heads:
            g, half = divmod(hd, 2)
            q = _half_mask(q_ref[0, :, g * SLOT:(g + 1) * SLOT], half)
            k = k_ref[0, pl.ds(start, TK), g * SLOT:(g + 1) * SLOT]
            z.append(_nt_dot(k, q))
        drop = [_softplus2(z[hd]) for hd in heads]
        if diag is not None:
            drop = [jnp.where(strict, d, 0.0) for d in drop]
        incl = [_neg_suffix_sum(tri_ref[...], drop[hd]) for hd in heads]
        w = [jnp.exp2(z[hd] + incl[hd] + carry[hd][1]) for hd in heads]
        if diag is not None:
            w = [jnp.where(strict, wh, 0.0) for wh in w]
        pv = [_dot(vt_ref[0, hd * HEAD_DIM:(hd + 1) * HEAD_DIM, pl.ds(start, TK)], w[hd].astype(BF16))
              for hd in heads]
        return tuple((carry[hd][0] + pv[hd], carry[hd][1] + incl[hd][0:1, :]) for hd in heads)

    carry = tuple((jnp.zeros((HEAD_DIM, TQ), F32), jnp.zeros((1, TQ), F32)) for _ in range(SB_HEADS))
    n_past = qi * n_diag
    for d in reversed(range(n_diag)):
        carry = step(pl.multiple_of((n_past + d) * TK, TK), carry, d)
    carry = lax.fori_loop(
        0, n_past, lambda i, c: step(pl.multiple_of((n_past - 1 - i) * TK, TK), c, None), carry)
    for g in range(SB_HEADS // 2):
        _store_pair(o_ref, g, carry[2 * g][0], carry[2 * g + 1][0])


def _sb_attention(q, k, vt):
    b, s, w = q.shape
    tri = -(jnp.arange(TK)[None, :] >= jnp.arange(TK)[:, None]).astype(BF16)
    return pl.pallas_call(
        _sb_kernel,
        out_shape=jax.ShapeDtypeStruct((b, s, w), BF16),
        grid=(b, s // TQ),
        in_specs=[pl.BlockSpec((1, TQ, w), lambda i, j: (i, j, 0)),
                  pl.BlockSpec((1, s, w), lambda i, j: (i, 0, 0)),
                  pl.BlockSpec((1, w, s), lambda i, j: (i, 0, 0)),
                  pl.BlockSpec((TK, TK), lambda i, j: (0, 0))],
        out_specs=pl.BlockSpec((1, TQ, w), lambda i, j: (i, j, 0)),
        compiler_params=_cparams(2),
        name="sb_attention",
    )(q, k, vt, tri)


def _mla_kernel(q_ref, k_ref, vt_ref, o_ref):
    qi = pl.program_id(1)
    row = lax.broadcasted_iota(jnp.int32, (TK, TQ), 0)
    col = lax.broadcasted_iota(jnp.int32, (TK, TQ), 1)
    n_diag = TQ // TK

    def step(start, carry, diag):
        heads = range(MLA_HEADS)
        causal = None if diag is None else (row + diag * TK) <= col
        scores = []
        for hd in heads:
            q = q_ref[0, :, hd * SLOT:(hd + 1) * SLOT]
            k = k_ref[0, pl.ds(start, TK), hd * SLOT:(hd + 1) * SLOT]
            s = _nt_dot(k, q)
            scores.append(s if diag is None else jnp.where(causal, s, NEG))
        m_new = [jnp.maximum(carry[hd][0], jnp.max(scores[hd], axis=0, keepdims=True)) for hd in heads]
        alpha = [jnp.exp2(carry[hd][0] - m_new[hd]) for hd in heads]
        p = [jnp.exp2(scores[hd] - m_new[hd]) for hd in heads]
        l = [alpha[hd] * carry[hd][1] + jnp.sum(p[hd], axis=0, keepdims=True) for hd in heads]
        pv = [_dot(vt_ref[0, hd * MLA_V:(hd + 1) * MLA_V, pl.ds(start, TK)], p[hd].astype(BF16))
              for hd in heads]
        return tuple((m_new[hd], l[hd], alpha[hd] * carry[hd][2] + pv[hd]) for hd in heads)

    carry = tuple((jnp.full((1, TQ), NEG, F32), jnp.zeros((1, TQ), F32), jnp.zeros((MLA_V, TQ), F32))
                  for _ in range(MLA_HEADS))
    n_past = qi * n_diag
    for d in range(n_diag):
        carry = step(pl.multiple_of((n_past + d) * TK, TK), carry, d)
    carry = lax.fori_loop(0, n_past, lambda j, c: step(pl.multiple_of(j * TK, TK), c, None), carry)
    outs = [acc * (1.0 / l) for (_, l, acc) in carry]
    for g in range(MLA_HEADS // 2):
        _store_pair(o_ref, g, outs[2 * g], outs[2 * g + 1])


def _mla_attention(q, k, vt):
    b, s, w = q.shape
    wv = vt.shape[1]
    return pl.pallas_call(
        _mla_kernel,
        out_shape=jax.ShapeDtypeStruct((b, s, wv), BF16),
        grid=(b, s // TQ),
        in_specs=[pl.BlockSpec((1, TQ, w), lambda i, j: (i, j, 0)),
                  pl.BlockSpec((1, s, w), lambda i, j: (i, 0, 0)),
                  pl.BlockSpec((1, wv, s), lambda i, j: (i, 0, 0))],
        out_specs=pl.BlockSpec((1, TQ, wv), lambda i, j: (i, j, 0)),
        compiler_params=_cparams(2),
        name="mla_attention",
    )(q, k, vt)


def _swa_kernel(sink_ref, q_ref, kp_ref, kc_ref, vtp_ref, vtc_ref, bias_ref, o_ref):
    qi = pl.program_id(1)
    k = jnp.concatenate([kp_ref[0], kc_ref[0]], axis=0)
    vt = jnp.concatenate([vtp_ref[0], vtc_ref[0]], axis=1)
    nk = WINDOW + TQ_SW
    before_start = (lax.broadcasted_iota(jnp.int32, (nk, TQ_SW), 0) < WINDOW) & (qi == 0)
    slots = range(SW_HEADS)
    head = [i // 2 + (i % 2) * (SW_HEADS // SW_KV_HEADS) for i in slots]
    sink = [sink_ref[head[i]] * LOG2E for i in slots]
    s = []
    for i in slots:
        q = _half_mask(q_ref[0, :, (i // 2) * SLOT:(i // 2 + 1) * SLOT], i % 2)
        si = _nt_dot(k, q) + bias_ref[head[i]]
        s.append(jnp.where(before_start, NEG, si))
    m = [jnp.maximum(jnp.max(s[i], axis=0, keepdims=True), sink[i]) for i in slots]
    p = [jnp.exp2(s[i] - m[i]) for i in slots]
    l = [jnp.sum(p[i], axis=0, keepdims=True) + jnp.exp2(sink[i] - m[i]) for i in slots]
    acc = [_dot(vt[(i % 2) * HEAD_DIM:(i % 2 + 1) * HEAD_DIM, :], p[i].astype(BF16)) for i in slots]
    outs = [acc[i] * (1.0 / l[i]) for i in slots]
    for g in range(SW_HEADS // 2):
        _store_pair(o_ref, g, outs[2 * g], outs[2 * g + 1])


def _swa_attention(sinks, q, k, vt, bias):
    b, s, w = q.shape
    prev = lambda j: jnp.maximum(2 * j - 1, 0)
    return pl.pallas_call(
        _swa_kernel,
        out_shape=jax.ShapeDtypeStruct((b, s, w), BF16),
        grid=(b, s // TQ_SW),
        in_specs=[pl.BlockSpec(memory_space=pltpu.SMEM),
                  pl.BlockSpec((1, TQ_SW, w), lambda i, j: (i, j, 0)),
                  pl.BlockSpec((1, WINDOW, SLOT), lambda i, j: (i, prev(j), 0)),
                  pl.BlockSpec((1, TQ_SW, SLOT), lambda i, j: (i, j, 0)),
                  pl.BlockSpec((1, SLOT, WINDOW), lambda i, j: (i, 0, prev(j))),
                  pl.BlockSpec((1, SLOT, TQ_SW), lambda i, j: (i, 0, j)),
                  pl.BlockSpec(bias.shape, lambda i, j: (0, 0, 0))],
        out_specs=pl.BlockSpec((1, TQ_SW, w), lambda i, j: (i, j, 0)),
        compiler_params=_cparams(2),
        name="swa_attention",
    )(sinks, q, k, k, vt, vt, bias)


def _mlp_kernel(x_ref, mod_ref, oa_ref, ob_ref, oc_ref, wo_ref, n2g_ref, wup_ref, cw_ref, cb_ref,
                wdn_ref, out_ref, carry_ref, acc_ref):
    si = pl.program_id(1)
    tm = x_ref.shape[1]
    mod = mod_ref[0]
    gate1, shift2, scale2, gate2 = mod[2:3], mod[3:4], mod[4:5], mod[5:6]
    na, nb = oa_ref.shape[2], ob_ref.shape[2]
    att = (_dot(oa_ref[0], wo_ref[0:na, :]) + _dot(ob_ref[0], wo_ref[na:na + nb, :])
           + _dot(oc_ref[0], wo_ref[na + nb:, :]))
    x1 = x_ref[0] + gate1 * att
    h2 = (_rms(x1, D_MODEL) * n2g_ref[...] * (1.0 + scale2) + shift2).astype(BF16)

    @pl.when(si == 0)
    def _():
        carry_ref[...] = jnp.zeros_like(carry_ref)

    def up(c):
        lo = c * FF_CHUNK
        return (_dot(h2, wup_ref[:, lo:lo + FF_CHUNK]),
                _dot(h2, wup_ref[:, D_FF + lo:D_FF + lo + FF_CHUNK]))

    def conv(u, col):
        cols = slice(col, col + FF_CHUNK)
        prev = carry_ref[:, cols]
        carry_ref[:, cols] = u[tm - CARRY_ROWS:, :]
        ext = jnp.concatenate([prev, u], axis=0)
        u1 = ext[CARRY_ROWS - 1:CARRY_ROWS - 1 + tm, :]
        u2 = ext[CARRY_ROWS - 2:CARRY_ROWS - 2 + tm, :]
        cw = cw_ref[:, cols]
        return u * cw[2:3] + u1 * cw[1:2] + u2 * cw[0:1] + cb_ref[:, cols]

    n_chunks = D_FF // FF_CHUNK
    u_next = up(0)
    for c in range(n_chunks):
        u_gate, u_val = u_next
        if c + 1 < n_chunks:
            u_next = up(c + 1)
        gate = conv(u_gate, c * FF_CHUNK)
        val = conv(u_val, D_FF + c * FF_CHUNK)
        a = (gate * jax.nn.sigmoid(gate) * val).astype(BF16)
        part = _dot(a, wdn_ref[c * FF_CHUNK:(c + 1) * FF_CHUNK, :])
        if c == 0:
            acc_ref[...] = part
        else:
            acc_ref[...] += part
    out_ref[0] = x1 + gate2 * acc_ref[...]


def _mlp(x, mods, oa, ob, oc, w_out, n2g, w_up, conv_w, conv_b, w_down):
    b, s, d = x.shape
    tm = TM_MLP
    row = lambda w: pl.BlockSpec((1, tm, w), lambda i, j: (i, j, 0))
    const = lambda a: pl.BlockSpec(a.shape, lambda i, j: (0,) * a.ndim, pipeline_mode=pl.Buffered(1))
    return pl.pallas_call(
        _mlp_kernel,
        out_shape=jax.ShapeDtypeStruct((b, s, d), F32),
        grid=(b, s // tm),
        in_specs=[row(d), pl.BlockSpec((1, 6, d), lambda i, j: (i, 0, 0)),
                  row(oa.shape[2]), row(ob.shape[2]), row(oc.shape[2]),
                  const(w_out), const(n2g), const(w_up), const(conv_w), const(conv_b), const(w_down)],
        out_specs=row(d),
        scratch_shapes=[pltpu.VMEM((CARRY_ROWS, 2 * D_FF), F32), pltpu.VMEM((tm, d), F32)],
        compiler_params=_cparams(2),
        name="outproj_mlp",
    )(x, mods, oa, ob, oc, w_out, n2g, w_up, conv_w, conv_b, w_down)


def _slot_gain(g, swap):
    z = jnp.zeros((SLOT - MLA_QK,), F32)
    lo, hi = g[ROPE_LO:ROPE_LO + ROPE_HALF], g[ROPE_LO + ROPE_HALF:MLA_QK]
    if swap:
        return jnp.concatenate([jnp.zeros((MLA_NOPE,), F32), hi, lo, z]).reshape(1, SLOT)
    return jnp.concatenate([g, z]).reshape(1, SLOT)


def _layout_w_in(w):
    d = w.shape[0]
    z = lambda n: jnp.zeros((d, n), w.dtype)
    kr = w[:, 1152:1184]
    swq = w[:, 1184:1568].reshape(d, SW_HEADS, HEAD_DIM)
    order = [0, 3, 1, 4, 2, 5]
    swq = swq[:, order, :].reshape(d, SW_HEADS * HEAD_DIM)
    cols = [w[:, 0:1152],
            z(MLA_NOPE), kr, z(SLOT - MLA_QK),
            z(MLA_NOPE), kr[:, ROPE_HALF:], kr[:, :ROPE_HALF], z(SLOT - MLA_QK),
            w[:, 1568:1696], swq, w[:, 1696:1824]]
    return jnp.concatenate(cols, axis=1).astype(BF16)


def _layout_w_uq(w):
    r = w.shape[0]
    w = w.reshape(r, MLA_HEADS, MLA_QK)
    nope, x1, x2 = w[..., :MLA_NOPE], w[..., MLA_NOPE:MLA_NOPE + ROPE_HALF], w[..., MLA_NOPE + ROPE_HALF:]
    z = jnp.zeros((r, MLA_HEADS, SLOT - MLA_QK), w.dtype)
    plain = jnp.concatenate([nope, x1, x2, z], axis=-1).reshape(r, MLA_HEADS * SLOT)
    swapped = jnp.concatenate([jnp.zeros_like(nope), x2, x1, z], axis=-1).reshape(r, MLA_HEADS * SLOT)
    return jnp.concatenate([plain, swapped], axis=1).astype(BF16)


def _layout_w_ukv(w):
    r = w.shape[0]
    w = w.reshape(r, MLA_HEADS, MLA_NOPE + MLA_V)
    k_nope, v = w[..., :MLA_NOPE], w[..., MLA_NOPE:]
    k_slots = jnp.concatenate([k_nope, jnp.zeros((r, MLA_HEADS, SLOT - MLA_NOPE), w.dtype)], axis=-1)
    return jnp.concatenate([k_slots.reshape(r, MLA_HEADS * SLOT), v.reshape(r, MLA_HEADS * MLA_V)],
                           axis=1).astype(BF16)


def _layout_w_out(w):
    n_ab = SB_HEADS * HEAD_DIM + MLA_HEADS * MLA_V
    sw = w[n_ab:].reshape(SW_HEADS, HEAD_DIM, w.shape[1])[jnp.array([0, 3, 1, 4, 2, 5])]
    return jnp.concatenate([w[:n_ab], sw.reshape(SW_HEADS * HEAD_DIM, w.shape[1])], axis=0).astype(BF16)


def kernel(x, c, positions, rel_table, norm1_g, norm2_g, w_ada, b_ada, w_in, mla_cq_g, w_uq, mla_ckv_g,
           w_ukv, mla_qn_g, mla_kn_g, sw_qn_g, sw_kn_g, sw_sinks, w_out, w_up, conv_w, conv_b, w_down):
    depth = w_in.shape[0]
    b = x.shape[0]
    mods = _mods(c, w_ada, b_ada).reshape(depth, b, 6, D_MODEL)
    cos_t, sin_t = _rope_tables(positions)
    bias = _window_bias(rel_table)
    row = lambda v: v.reshape(1, -1).astype(F32)
    two = lambda v: jnp.concatenate([v, v]).reshape(1, SLOT).astype(F32)
    for l in range(depth):
        qkv = _prep(x, mods[l], row(norm1_g[l]), _layout_w_in(w_in[l]), row(mla_cq_g[l]),
                    _layout_w_uq(w_uq[l]), row(mla_ckv_g[l]), _layout_w_ukv(w_ukv[l]),
                    _slot_gain(mla_qn_g[l], False), _slot_gain(mla_qn_g[l], True),
                    _slot_gain(mla_kn_g[l], False), _slot_gain(mla_kn_g[l], True),
                    two(sw_qn_g[l]), two(sw_kn_g[l]), cos_t, sin_t)
        sbq, sbk, sbvt, mq, mk, mvt, swq, swk, swvt = qkv
        o_a = _sb_attention(sbq, sbk, sbvt)
        o_b = _mla_attention(mq, mk, mvt)
        o_c = _swa_attention(sw_sinks[l], swq, swk, swvt, bias)
        x = _mlp(x, mods[l], o_a, o_b, o_c, _layout_w_out(w_out[l]), row(norm2_g[l]),
                 w_up[l].astype(BF16), conv_w[l], row(conv_b[l]), w_down[l].astype(BF16))
    return x
```

```python
import functools
import math

import numpy as np
import jax
import jax.numpy as jnp
from jax import lax
from jax.experimental import pallas as pl
from jax.experimental.pallas import tpu as pltpu

F32 = jnp.float32
BF16 = jnp.bfloat16

D_MODEL = 1024
HEAD_DIM = 64
SB_HEADS = 4
MLA_HEADS = 6
MLA_Q_RANK = 256
MLA_KV_RANK = 128
MLA_NOPE = 64
MLA_ROPE = 32
MLA_V = 64
MLA_QK = MLA_NOPE + MLA_ROPE
ROPE_THETA = 10000.0
SW_HEADS = 6
SW_KV_HEADS = 2
WINDOW = 128
REL_BUCKETS = 32
REL_MAX_DIST = 128
D_FF = 2816
CONV_W = 3
EPS = 1e-6
NEG = -1e30

LANES = 128
SLOT = LANES
HALF = SLOT // 2

C_SBQ, C_SBK, C_SBV = 0, 256, 512
C_CQ = 768
C_CKV = 1024
C_KROPE = 1152
C_KROPE_SW = 1280
C_SWK = 1408
C_SWQ = 1536
C_SWV = 1920
N_IN = 2048

ROPE_LO = MLA_NOPE
ROPE_HALF = MLA_ROPE // 2

TM_PREP = 512
TM_MLP = 512
TQ = 512
TK = 256
TQ_SW = 256
LOG2E = math.log2(math.e)
FF_CHUNK = 256
CARRY_ROWS = 8

VMEM_LIMIT = 56 * 1024 * 1024


def _cparams(n_axes):
    return pltpu.CompilerParams(dimension_semantics=("arbitrary",) * n_axes,
                                vmem_limit_bytes=VMEM_LIMIT)


def _rms(x, n):
    return x * lax.rsqrt(jnp.sum(x * x, axis=-1, keepdims=True) * (1.0 / n) + EPS)


def _nt_dot(a, b):
    return lax.dot_general(a, b, (((1,), (1,)), ((), ())), preferred_element_type=F32)


def _dot(a, b):
    return jnp.dot(a, b, preferred_element_type=F32)


def _mods_kernel(c_ref, w_ref, b_ref, o_ref):
    c = c_ref[...]
    a = (c * jax.nn.sigmoid(c)).astype(BF16)
    o_ref[0] = _dot(a, w_ref[0].astype(BF16)) + b_ref[0]


def _mods(c, w_ada, b_ada):
    depth, d, n = w_ada.shape
    b = c.shape[0]
    tn = 1536
    return pl.pallas_call(
        _mods_kernel,
        out_shape=jax.ShapeDtypeStruct((depth, b, n), F32),
        grid=(depth, n // tn),
        in_specs=[pl.BlockSpec((b, d), lambda l, j: (0, 0)),
                  pl.BlockSpec((1, d, tn), lambda l, j: (l, 0, j)),
                  pl.BlockSpec((1, 1, tn), lambda l, j: (l, 0, j))],
        out_specs=pl.BlockSpec((1, b, tn), lambda l, j: (l, 0, j)),
        compiler_params=_cparams(2),
        name="adaln_mods",
    )(c, w_ada, b_ada.reshape(depth, 1, n))


def _rope_kernel(pos_ref, invf_ref, cos_ref, sin_ref):
    pos = pos_ref[0].astype(F32)
    ang = invf_ref[...] * pos
    c = jnp.cos(ang)
    s = jnp.sin(ang)
    tm = pos.shape[1]
    ones = jnp.ones((ROPE_LO, tm), F32)
    zeros = jnp.zeros((ROPE_LO, tm), F32)
    pad = SLOT - ROPE_LO - MLA_ROPE
    cos_t = jnp.concatenate([ones, c, c, jnp.ones((pad, tm), F32)], axis=0)
    sin_t = jnp.concatenate([zeros, -s, s, jnp.zeros((pad, tm), F32)], axis=0)
    cos_ref[0] = cos_t.T
    sin_ref[0] = sin_t.T


def _rope_tables(positions):
    b, s = positions.shape
    tm = s
    half = ROPE_HALF
    inv_freq = jnp.power(ROPE_THETA, -jnp.arange(half, dtype=F32) / half).reshape(half, 1)
    out = jax.ShapeDtypeStruct((b, s, SLOT), F32)
    return pl.pallas_call(
        _rope_kernel,
        out_shape=(out, out),
        grid=(b, s // tm),
        in_specs=[pl.BlockSpec((1, 1, tm), lambda i, j: (i, 0, j)),
                  pl.BlockSpec((half, 1), lambda i, j: (0, 0))],
        out_specs=(pl.BlockSpec((1, tm, SLOT), lambda i, j: (i, j, 0)),
                   pl.BlockSpec((1, tm, SLOT), lambda i, j: (i, j, 0))),
        compiler_params=_cparams(2),
        name="rope_tables",
    )(positions.reshape(b, 1, s), inv_freq)


def _t5_bucket(dist):
    max_exact = REL_BUCKETS // 2
    n = jnp.maximum(dist, 0)
    nf = jnp.maximum(n, 1).astype(F32)
    large = max_exact + (jnp.log(nf / max_exact) / math.log(REL_MAX_DIST / max_exact)
                         * (REL_BUCKETS - max_exact)).astype(jnp.int32)
    large = jnp.minimum(large, REL_BUCKETS - 1)
    return jnp.where(n < max_exact, n, large)


def _bias_kernel(tab_ref, bucket_ref, o_ref):
    bucket = bucket_ref[...]
    for h in range(SW_HEADS):
        acc = jnp.zeros(bucket.shape, F32)
        for bkt in range(REL_BUCKETS):
            acc = jnp.where(bucket == bkt, tab_ref[bkt, h], acc)
        o_ref[h] = jnp.where(bucket >= 0, acc * LOG2E, NEG)


def _window_bias(rel_table):
    nk = WINDOW + TQ_SW
    key = jnp.arange(nk)[:, None]
    qry = jnp.arange(TQ_SW)[None, :]
    dist = qry + WINDOW - key
    valid = (dist >= 0) & (dist < WINDOW)
    bucket = jnp.where(valid, _t5_bucket(dist), -1).astype(jnp.int32)
    return pl.pallas_call(
        _bias_kernel,
        out_shape=jax.ShapeDtypeStruct((SW_HEADS, nk, TQ_SW), F32),
        in_specs=[pl.BlockSpec(memory_space=pltpu.SMEM),
                  pl.BlockSpec(memory_space=pltpu.VMEM)],
        out_specs=pl.BlockSpec(memory_space=pltpu.VMEM),
        name="window_bias",
    )(rel_table, bucket)


def _half_rms(x, gain, scale):
    lo = lax.broadcasted_iota(jnp.int32, x.shape, 1) < HALF
    sq = x * x
    s_lo = jnp.sum(jnp.where(lo, sq, 0.0), axis=-1, keepdims=True)
    s_hi = jnp.sum(jnp.where(lo, 0.0, sq), axis=-1, keepdims=True)
    r = jnp.where(lo, lax.rsqrt(s_lo * (1.0 / HEAD_DIM) + EPS), lax.rsqrt(s_hi * (1.0 / HEAD_DIM) + EPS))
    return x * r * (gain * scale)


def _prep_kernel(x_ref, mod_ref, n1g_ref, win_ref, cqg_ref, wuq_ref, ckvg_ref, wukv_ref,
                 gq_ref, gqs_ref, gk_ref, gks_ref, swqg_ref, swkg_ref, cos_ref, sin_ref,
                 sbq_ref, sbk_ref, sbvt_ref, mq_ref, mk_ref, mvt_ref, swq_ref, swk_ref, swvt_ref):
    x = x_ref[0]
    mod = mod_ref[0]
    shift1, scale1 = mod[0:1], mod[1:2]
    h = (_rms(x, D_MODEL) * n1g_ref[...] * (1.0 + scale1) + shift1).astype(BF16)

    proj_mla = _dot(h, win_ref[:, C_CQ:C_SWQ])
    proj_sb = _dot(h, win_ref[:, C_SBQ:C_CQ])
    cq = proj_mla[:, 0:MLA_Q_RANK]
    ckv = proj_mla[:, C_CKV - C_CQ:C_CKV - C_CQ + MLA_KV_RANK]
    krope = proj_mla[:, C_KROPE - C_CQ:C_KROPE - C_CQ + SLOT]
    krope_sw = proj_mla[:, C_KROPE_SW - C_CQ:C_KROPE_SW - C_CQ + SLOT]
    cqn = (_rms(cq, MLA_Q_RANK) * cqg_ref[...]).astype(BF16)
    ckvn = (_rms(ckv, MLA_KV_RANK) * ckvg_ref[...]).astype(BF16)
    qraw = _dot(cqn, wuq_ref[...])
    kv = _dot(ckvn, wukv_ref[...])
    proj_sw = _dot(h, win_ref[:, C_SWQ:N_IN])

    sbq_ref[0] = (proj_sb[:, C_SBQ:C_SBQ + 256] * (HEAD_DIM ** -0.5 * LOG2E)).astype(BF16)
    sbk_ref[0] = proj_sb[:, C_SBK:C_SBK + 256].astype(BF16)
    sbvt_ref[0] = proj_sb[:, C_SBV:C_SBV + 256].T.astype(BF16)

    cos = cos_ref[0]
    sin = sin_ref[0]

    nq = MLA_HEADS * SLOT
    q_scale = MLA_QK ** -0.5 * LOG2E
    q_cos, q_sin = gq_ref[...] * cos, gqs_ref[...] * sin
    for hd in range(MLA_HEADS):
        slot = qraw[:, hd * SLOT:(hd + 1) * SLOT]
        swapped = qraw[:, nq + hd * SLOT:nq + (hd + 1) * SLOT]
        r = lax.rsqrt(jnp.sum(slot * slot, axis=-1, keepdims=True) * (1.0 / MLA_QK) + EPS) * q_scale
        mq_ref[0, :, hd * SLOT:(hd + 1) * SLOT] = ((slot * q_cos + swapped * q_sin) * r).astype(BF16)

    mvt_ref[0] = kv[:, nq:nq + MLA_HEADS * MLA_V].T.astype(BF16)
    k_cos = gk_ref[...] * cos
    k_rot = krope_sw * (gks_ref[...] * sin)
    for hd in range(MLA_HEADS):
        slot = kv[:, hd * SLOT:(hd + 1) * SLOT] + krope
        r = lax.rsqrt(jnp.sum(slot * slot, axis=-1, keepdims=True) * (1.0 / MLA_QK) + EPS)
        mk_ref[0, :, hd * SLOT:(hd + 1) * SLOT] = ((slot * k_cos + k_rot) * r).astype(BF16)

    for g in range(SW_HEADS // 2):
        xq = proj_sw[:, g * SLOT:(g + 1) * SLOT]
        swq_ref[0, :, g * SLOT:(g + 1) * SLOT] = _half_rms(
            xq, swqg_ref[...], HEAD_DIM ** -0.5 * LOG2E).astype(BF16)
    swk_ref[0] = _half_rms(proj_mla[:, C_SWK - C_CQ:C_SWK - C_CQ + SLOT], swkg_ref[...], 1.0).astype(BF16)
    swvt_ref[0] = proj_sw[:, C_SWV - C_SWQ:C_SWV - C_SWQ + SLOT].T.astype(BF16)


def _prep(x, mods, n1g, w_in, cqg, w_uq, ckvg, w_ukv, gq, gqs, gk, gks, swqg, swkg, cos_t, sin_t):
    b, s, d = x.shape
    tm = TM_PREP
    row = lambda w: pl.BlockSpec((1, tm, w), lambda i, j: (i, j, 0))
    colt = lambda w: pl.BlockSpec((1, w, tm), lambda i, j: (i, 0, j))
    full = lambda a: pl.BlockSpec(a.shape, lambda i, j: (0,) * a.ndim)
    act = lambda w: jax.ShapeDtypeStruct((b, s, w), BF16)
    actt = lambda w: jax.ShapeDtypeStruct((b, w, s), BF16)
    return pl.pallas_call(
        _prep_kernel,
        out_shape=(act(256), act(256), actt(256), act(768), act(768), actt(384),
                   act(384), act(128), actt(128)),
        grid=(b, s // tm),
        in_specs=[row(d), pl.BlockSpec((1, 6, d), lambda i, j: (i, 0, 0)), full(n1g), full(w_in),
                  full(cqg), full(w_uq), full(ckvg), full(w_ukv), full(gq), full(gqs), full(gk),
                  full(gks), full(swqg), full(swkg), row(SLOT), row(SLOT)],
        out_specs=(row(256), row(256), colt(256), row(768), row(768), colt(384),
                   row(384), row(128), colt(128)),
        compiler_params=_cparams(2),
        name="prep_qkv",
    )(x, mods, n1g, w_in, cqg, w_uq, ckvg, w_ukv, gq, gqs, gk, gks, swqg, swkg, cos_t, sin_t)


def _half_mask(q, half):
    lane = lax.broadcasted_iota(jnp.int32, q.shape, 1)
    keep = (lane < HALF) if half == 0 else (lane >= HALF)
    return jnp.where(keep, q, jnp.zeros_like(q))


def _store_pair(o_ref, g, out_lo, out_hi):
    pair = jnp.concatenate([out_lo, out_hi], axis=0)
    o_ref[0, :, g * SLOT:(g + 1) * SLOT] = pair.T.astype(o_ref.dtype)


SIGN_BIT = -2 ** 31
JOBS = (("a", 0), ("b", 0), ("b", 1), ("a", 1), ("b", 2), ("b", 3), ("a", 2), ("b", 4), ("b", 5), ("a", 3))


def _softplus2(z):
    neg_abs = lax.bitcast_convert_type(lax.bitcast_convert_type(z, jnp.int32) | SIGN_BIT, F32)
    return jnp.maximum(z, 0.0) + jnp.log(1.0 + jnp.exp2(neg_abs)) * LOG2E


def _hi_lo(x):
    hi = x.astype(BF16)
    lo = (x - hi.astype(F32)).astype(BF16)
    return jnp.concatenate([hi, lo], axis=0)


def _causal_kernel(aq_ref, ak_ref, avt_ref, tri_ref, bq_ref, bk_ref, bvt_ref, oa_ref, ob_ref):
    qi = pl.program_id(1)
    row = lax.broadcasted_iota(jnp.int32, (TK, TQ), 0)
    col = lax.broadcasted_iota(jnp.int32, (TK, TQ), 1)
    n_diag = TQ // TK
    a_heads, b_heads = range(SB_HEADS), range(MLA_HEADS)

    def run_jobs(carry, blocks):
        ca, cb = list(carry[0]), list(carry[1])
        jobs = [(mixer, hd) + ((a_start, a_diag) if mixer == "a" else (b_start, b_diag))
                for (a_start, a_diag, b_start, b_diag) in blocks for (mixer, hd) in JOBS]
        st = [None] * len(jobs)

        def scores(t):
            mixer, hd, start, diag = jobs[t]
            if mixer == "a":
                g, half = divmod(hd, 2)
                q = _half_mask(aq_ref[0, :, g * SLOT:(g + 1) * SLOT], half)
                st[t] = _nt_dot(ak_ref[0, pl.ds(start, TK), g * SLOT:(g + 1) * SLOT], q)
            else:
                sc = _nt_dot(bk_ref[0, pl.ds(start, TK), hd * SLOT:(hd + 1) * SLOT],
                             bq_ref[0, :, hd * SLOT:(hd + 1) * SLOT])
                st[t] = sc if diag is None else jnp.where((row + diag * TK) <= col, sc, NEG)

        def second(t):
            mixer, hd, start, diag = jobs[t]
            if mixer == "a":
                drop = _softplus2(st[t])
                if diag is not None:
                    drop = jnp.where((row + diag * TK) < col, drop, 0.0)
                st[t] = (st[t], _dot(tri_ref[...], _hi_lo(drop)))
            else:
                m_old, l_old, acc = cb[hd]
                m_new = jnp.maximum(m_old, jnp.max(st[t], axis=0, keepdims=True))
                alpha = jnp.exp2(m_old - m_new)
                p = jnp.exp2(st[t] - m_new)
                l_new = alpha * l_old + jnp.sum(p, axis=0, keepdims=True)
                pv = _dot(bvt_ref[0, hd * MLA_V:(hd + 1) * MLA_V, pl.ds(start, TK)], p.astype(BF16))
                cb[hd] = (m_new, l_new, alpha * acc + pv)

        def third(t):
            mixer, hd, start, diag = jobs[t]
            if mixer == "a":
                z, incl = st[t]
                acc, run = ca[hd]
                w = jnp.exp2(z + incl + run)
                if diag is not None:
                    w = jnp.where((row + diag * TK) < col, w, 0.0)
                pv = _dot(avt_ref[0, hd * HEAD_DIM:(hd + 1) * HEAD_DIM, pl.ds(start, TK)], w.astype(BF16))
                ca[hd] = (acc + pv, run + incl[0:1, :])
            st[t] = None

        n = len(jobs)
        for t in range(n + 2):
            if t < n:
                scores(t)
            if 0 <= t - 1 < n:
                second(t - 1)
            if 0 <= t - 2 < n:
                third(t - 2)
        return tuple(ca), tuple(cb)

    carry = (tuple((jnp.zeros((HEAD_DIM, TQ), F32), jnp.zeros((1, TQ), F32)) for _ in a_heads),
             tuple((jnp.full((1, TQ), NEG, F32), jnp.zeros((1, TQ), F32), jnp.zeros((MLA_V, TQ), F32))
                   for _ in b_heads))
    n_past = qi * n_diag
    blk = lambda j: pl.multiple_of(j * TK, TK)
    carry = run_jobs(carry, [(blk(n_past + n_diag - 1 - i), n_diag - 1 - i, blk(n_past + i), i)
                             for i in range(n_diag)])

    def past(i, c):
        first = n_past - 1 - i * n_diag
        return run_jobs(c, [(blk(first - j), None, blk(first - j), None) for j in range(n_diag)])

    ca, cb = lax.fori_loop(0, qi, past, carry)
    for g in range(SB_HEADS // 2):
        _store_pair(oa_ref, g, ca[2 * g][0], ca[2 * g + 1][0])
    outs = [acc * (1.0 / l) for (_, l, acc) in cb]
    for g in range(MLA_HEADS // 2):
        _store_pair(ob_ref, g, outs[2 * g], outs[2 * g + 1])


def _causal_attention(aq, ak, avt, bq, bk, bvt):
    b, s, wa = aq.shape
    wb, wbv = bq.shape[2], bvt.shape[1]
    tri = -(jnp.arange(TK)[None, :] >= jnp.arange(TK)[:, None]).astype(BF16)
    tri = jnp.concatenate([tri, tri], axis=1)
    qblk = lambda w: pl.BlockSpec((1, TQ, w), lambda i, j: (i, j, 0))
    seq = lambda w: pl.BlockSpec((1, s, w), lambda i, j: (i, 0, 0))
    seqt = lambda w: pl.BlockSpec((1, w, s), lambda i, j: (i, 0, 0))
    return pl.pallas_call(
        _causal_kernel,
        out_shape=(jax.ShapeDtypeStruct((b, s, wa), BF16), jax.ShapeDtypeStruct((b, s, wbv), BF16)),
        grid=(b, s // TQ),
        in_specs=[qblk(wa), seq(wa), seqt(wa), pl.BlockSpec(tri.shape, lambda i, j: (0, 0)),
                  qblk(wb), seq(wb), seqt(wbv)],
        out_specs=(qblk(wa), qblk(wbv)),
        compiler_params=_cparams(2),
        name="causal_attention",
    )(aq, ak, avt, tri, bq, bk, bvt)


def _swa_kernel(sink_ref, q_ref, kp_ref, kc_ref, vtp_ref, vtc_ref, bias_ref, o_ref):
    qi = pl.program_id(1)
    k = jnp.concatenate([kp_ref[0], kc_ref[0]], axis=0)
    vt = jnp.concatenate([vtp_ref[0], vtc_ref[0]], axis=1)
    nk = WINDOW + TQ_SW
    before_start = (lax.broadcasted_iota(jnp.int32, (nk, TQ_SW), 0) < WINDOW) & (qi == 0)
    slots = range(SW_HEADS)
    head = [i // 2 + (i % 2) * (SW_HEADS // SW_KV_HEADS) for i in slots]
    sink = [sink_ref[head[i]] * LOG2E for i in slots]
    s = []
    for i in slots:
        q = _half_mask(q_ref[0, :, (i // 2) * SLOT:(i // 2 + 1) * SLOT], i % 2)
        si = _nt_dot(k, q) + bias_ref[head[i]]
        s.append(jnp.where(before_start, NEG, si))
    m = [jnp.maximum(jnp.max(s[i], axis=0, keepdims=True), sink[i]) for i in slots]
    p = [jnp.exp2(s[i] - m[i]) for i in slots]
    l = [jnp.sum(p[i], axis=0, keepdims=True) + jnp.exp2(sink[i] - m[i]) for i in slots]
    acc = [_dot(vt[(i % 2) * HEAD_DIM:(i % 2 + 1) * HEAD_DIM, :], p[i].astype(BF16)) for i in slots]
    outs = [acc[i] * (1.0 / l[i]) for i in slots]
    for g in range(SW_HEADS // 2):
        _store_pair(o_ref, g, outs[2 * g], outs[2 * g + 1])


def _swa_attention(sinks, q, k, vt, bias):
    b, s, w = q.shape
    prev = lambda j: jnp.maximum(2 * j - 1, 0)
    return pl.pallas_call(
        _swa_kernel,
        out_shape=jax.ShapeDtypeStruct((b, s, w), BF16),
        grid=(b, s // TQ_SW),
        in_specs=[pl.BlockSpec(memory_space=pltpu.SMEM),
                  pl.BlockSpec((1, TQ_SW, w), lambda i, j: (i, j, 0)),
                  pl.BlockSpec((1, WINDOW, SLOT), lambda i, j: (i, prev(j), 0)),
                  pl.BlockSpec((1, TQ_SW, SLOT), lambda i, j: (i, j, 0)),
                  pl.BlockSpec((1, SLOT, WINDOW), lambda i, j: (i, 0, prev(j))),
                  pl.BlockSpec((1, SLOT, TQ_SW), lambda i, j: (i, 0, j)),
                  pl.BlockSpec(bias.shape, lambda i, j: (0, 0, 0))],
        out_specs=pl.BlockSpec((1, TQ_SW, w), lambda i, j: (i, j, 0)),
        compiler_params=_cparams(2),
        name="swa_attention",
    )(sinks, q, k, k, vt, vt, bias)


def _mlp_kernel(x_ref, mod_ref, oa_ref, ob_ref, oc_ref, wo_ref, n2g_ref, wup_ref, cw_ref, cb_ref,
                wdn_ref, out_ref, carry_ref):
    si = pl.program_id(1)
    tm = x_ref.shape[1]
    mod = mod_ref[0]
    gate1, shift2, scale2, gate2 = mod[2:3], mod[3:4], mod[4:5], mod[5:6]
    na, nb = oa_ref.shape[2], ob_ref.shape[2]
    att = (_dot(oa_ref[0], wo_ref[0:na, :]) + _dot(ob_ref[0], wo_ref[na:na + nb, :])
           + _dot(oc_ref[0], wo_ref[na + nb:, :]))
    x1 = x_ref[0] + gate1 * att
    h2 = (_rms(x1, D_MODEL) * n2g_ref[...] * (1.0 + scale2) + shift2).astype(BF16)

    @pl.when(si == 0)
    def _():
        carry_ref[...] = jnp.zeros_like(carry_ref)

    def up(c):
        lo = c * FF_CHUNK
        return (_dot(h2, wup_ref[:, lo:lo + FF_CHUNK]),
                _dot(h2, wup_ref[:, D_FF + lo:D_FF + lo + FF_CHUNK]))

    def conv(u, col):
        cols = slice(col, col + FF_CHUNK)
        prev = carry_ref[:, cols]
        carry_ref[:, cols] = u[tm - CARRY_ROWS:, :]
        ext = jnp.concatenate([prev, u], axis=0)
        u1 = ext[CARRY_ROWS - 1:CARRY_ROWS - 1 + tm, :]
        u2 = ext[CARRY_ROWS - 2:CARRY_ROWS - 2 + tm, :]
        cw = cw_ref[:, cols]
        return u * cw[2:3] + u1 * cw[1:2] + u2 * cw[0:1] + cb_ref[:, cols]

    n_chunks = D_FF // FF_CHUNK
    u_next = up(0)
    for c in range(n_chunks):
        u_gate, u_val = u_next
        if c + 1 < n_chunks:
            u_next = up(c + 1)
        gate = conv(u_gate, c * FF_CHUNK)
        val = conv(u_val, D_FF + c * FF_CHUNK)
        a = (gate * jax.nn.sigmoid(gate) * val).astype(BF16)
        part = _dot(a, wdn_ref[c * FF_CHUNK:(c + 1) * FF_CHUNK, :])
        acc = part if c == 0 else acc + part
    out_ref[0] = x1 + gate2 * acc


def _mlp(x, mods, oa, ob, oc, w_out, n2g, w_up, conv_w, conv_b, w_down):
    b, s, d = x.shape
    tm = TM_MLP
    row = lambda w: pl.BlockSpec((1, tm, w), lambda i, j: (i, j, 0))
    const = lambda a: pl.BlockSpec(a.shape, lambda i, j: (0,) * a.ndim, pipeline_mode=pl.Buffered(1))
    return pl.pallas_call(
        _mlp_kernel,
        out_shape=jax.ShapeDtypeStruct((b, s, d), F32),
        grid=(b, s // tm),
        in_specs=[row(d), pl.BlockSpec((1, 6, d), lambda i, j: (i, 0, 0)),
                  row(oa.shape[2]), row(ob.shape[2]), row(oc.shape[2]),
                  const(w_out), const(n2g), const(w_up), const(conv_w), const(conv_b), const(w_down)],
        out_specs=row(d),
        scratch_shapes=[pltpu.VMEM((CARRY_ROWS, 2 * D_FF), F32)],
        compiler_params=_cparams(2),
        name="outproj_mlp",
    )(x, mods, oa, ob, oc, w_out, n2g, w_up, conv_w, conv_b, w_down)


def _slot_gain(g, swap):
    z = jnp.zeros((SLOT - MLA_QK,), F32)
    lo, hi = g[ROPE_LO:ROPE_LO + ROPE_HALF], g[ROPE_LO + ROPE_HALF:MLA_QK]
    if swap:
        return jnp.concatenate([jnp.zeros((MLA_NOPE,), F32), hi, lo, z]).reshape(1, SLOT)
    return jnp.concatenate([g, z]).reshape(1, SLOT)


def _layout_w_in(w):
    d = w.shape[0]
    z = lambda n: jnp.zeros((d, n), w.dtype)
    kr = w[:, 1152:1184]
    swq = w[:, 1184:1568].reshape(d, SW_HEADS, HEAD_DIM)
    order = [0, 3, 1, 4, 2, 5]
    swq = swq[:, order, :].reshape(d, SW_HEADS * HEAD_DIM)
    cols = [w[:, 0:1152],
            z(MLA_NOPE), kr, z(SLOT - MLA_QK),
            z(MLA_NOPE), kr[:, ROPE_HALF:], kr[:, :ROPE_HALF], z(SLOT - MLA_QK),
            w[:, 1568:1696], swq, w[:, 1696:1824]]
    return jnp.concatenate(cols, axis=1).astype(BF16)


def _layout_w_uq(w):
    r = w.shape[0]
    w = w.reshape(r, MLA_HEADS, MLA_QK)
    nope, x1, x2 = w[..., :MLA_NOPE], w[..., MLA_NOPE:MLA_NOPE + ROPE_HALF], w[..., MLA_NOPE + ROPE_HALF:]
    z = jnp.zeros((r, MLA_HEADS, SLOT - MLA_QK), w.dtype)
    plain = jnp.concatenate([nope, x1, x2, z], axis=-1).reshape(r, MLA_HEADS * SLOT)
    swapped = jnp.concatenate([jnp.zeros_like(nope), x2, x1, z], axis=-1).reshape(r, MLA_HEADS * SLOT)
    return jnp.concatenate([plain, swapped], axis=1).astype(BF16)


def _layout_w_ukv(w):
    r = w.shape[0]
    w = w.reshape(r, MLA_HEADS, MLA_NOPE + MLA_V)
    k_nope, v = w[..., :MLA_NOPE], w[..., MLA_NOPE:]
    k_slots = jnp.concatenate([k_nope, jnp.zeros((r, MLA_HEADS, SLOT - MLA_NOPE), w.dtype)], axis=-1)
    return jnp.concatenate([k_slots.reshape(r, MLA_HEADS * SLOT), v.reshape(r, MLA_HEADS * MLA_V)],
                           axis=1).astype(BF16)


def _layout_w_out(w):
    n_ab = SB_HEADS * HEAD_DIM + MLA_HEADS * MLA_V
    sw = w[n_ab:].reshape(SW_HEADS, HEAD_DIM, w.shape[1])[jnp.array([0, 3, 1, 4, 2, 5])]
    return jnp.concatenate([w[:n_ab], sw.reshape(SW_HEADS * HEAD_DIM, w.shape[1])], axis=0).astype(BF16)


def kernel(x, c, positions, rel_table, norm1_g, norm2_g, w_ada, b_ada, w_in, mla_cq_g, w_uq, mla_ckv_g,
           w_ukv, mla_qn_g, mla_kn_g, sw_qn_g, sw_kn_g, sw_sinks, w_out, w_up, conv_w, conv_b, w_down):
    depth = w_in.shape[0]
    b = x.shape[0]
    mods = _mods(c, w_ada, b_ada).reshape(depth, b, 6, D_MODEL)
    cos_t, sin_t = _rope_tables(positions)
    bias = _window_bias(rel_table)
    row = lambda v: v.reshape(1, -1).astype(F32)
    two = lambda v: jnp.concatenate([v, v]).reshape(1, SLOT).astype(F32)
    for l in range(depth):
        qkv = _prep(x, mods[l], row(norm1_g[l]), _layout_w_in(w_in[l]), row(mla_cq_g[l]),
                    _layout_w_uq(w_uq[l]), row(mla_ckv_g[l]), _layout_w_ukv(w_ukv[l]),
                    _slot_gain(mla_qn_g[l], False), _slot_gain(mla_qn_g[l], True),
                    _slot_gain(mla_kn_g[l], False), _slot_gain(mla_kn_g[l], True),
                    two(sw_qn_g[l]), two(sw_kn_g[l]), cos_t, sin_t)
        sbq, sbk, sbvt, mq, mk, mvt, swq, swk, swvt = qkv
        o_a, o_b = _causal_attention(sbq, sbk, sbvt, mq, mk, mvt)
        o_c = _swa_attention(sw_sinks[l], swq, swk, swvt, bias)
        x = _mlp(x, mods[l], o_a, o_b, o_c, _layout_w_out(w_out[l]), row(norm2_g[l]),
                 w_up[l].astype(BF16), conv_w[l], row(conv_b[l]), w_down[l].astype(BF16))
    return x
```

```python
import functools
import math

import numpy as np
import jax
import jax.numpy as jnp
from jax import lax
from jax.experimental import pallas as pl
from jax.experimental.pallas import tpu as pltpu

F32 = jnp.float32
BF16 = jnp.bfloat16

D_MODEL = 1024
HEAD_DIM = 64
SB_HEADS = 4
MLA_HEADS = 6
MLA_Q_RANK = 256
MLA_KV_RANK = 128
MLA_NOPE = 64
MLA_ROPE = 32
MLA_V = 64
MLA_QK = MLA_NOPE + MLA_ROPE
ROPE_THETA = 10000.0
SW_HEADS = 6
SW_KV_HEADS = 2
WINDOW = 128
REL_BUCKETS = 32
REL_MAX_DIST = 128
D_FF = 2816
CONV_W = 3
EPS = 1e-6
NEG = -1e30

LANES = 128
SLOT = LANES
HALF = SLOT // 2

C_SBQ, C_SBK, C_SBV = 0, 256, 512
C_CQ = 768
C_CKV = 1024
C_KROPE = 1152
C_KROPE_SW = 1280
C_SWK = 1408
C_SWQ = 1536
C_SWV = 1920
N_IN = 2048

ROPE_LO = MLA_NOPE
ROPE_HALF = MLA_ROPE // 2

TM_PREP = 512
TM_MLP = 512
TQ = 512
TK = 256
TQ_SW = 256
LOG2E = math.log2(math.e)
FF_CHUNK = 256
CARRY_ROWS = 8

VMEM_LIMIT = 56 * 1024 * 1024


def _cparams(n_axes):
    return pltpu.CompilerParams(dimension_semantics=("arbitrary",) * n_axes,
                                vmem_limit_bytes=VMEM_LIMIT)


def _rms(x, n):
    return x * lax.rsqrt(jnp.sum(x * x, axis=-1, keepdims=True) * (1.0 / n) + EPS)


def _nt_dot(a, b):
    return lax.dot_general(a, b, (((1,), (1,)), ((), ())), preferred_element_type=F32)


def _dot(a, b):
    return jnp.dot(a, b, preferred_element_type=F32)


def _mods_kernel(c_ref, w_ref, b_ref, o_ref):
    c = c_ref[...]
    a = (c * jax.nn.sigmoid(c)).astype(BF16)
    o_ref[0] = _dot(a, w_ref[0].astype(BF16)) + b_ref[0]


def _mods(c, w_ada, b_ada):
    depth, d, n = w_ada.shape
    b = c.shape[0]
    tn = 1536
    return pl.pallas_call(
        _mods_kernel,
        out_shape=jax.ShapeDtypeStruct((depth, b, n), F32),
        grid=(depth, n // tn),
        in_specs=[pl.BlockSpec((b, d), lambda l, j: (0, 0)),
                  pl.BlockSpec((1, d, tn), lambda l, j: (l, 0, j)),
                  pl.BlockSpec((1, 1, tn), lambda l, j: (l, 0, j))],
        out_specs=pl.BlockSpec((1, b, tn), lambda l, j: (l, 0, j)),
        compiler_params=_cparams(2),
        name="adaln_mods",
    )(c, w_ada, b_ada.reshape(depth, 1, n))


def _rope_kernel(pos_ref, invf_ref, cos_ref, sin_ref):
    pos = pos_ref[0].astype(F32)
    ang = invf_ref[...] * pos
    c = jnp.cos(ang)
    s = jnp.sin(ang)
    tm = pos.shape[1]
    ones = jnp.ones((ROPE_LO, tm), F32)
    zeros = jnp.zeros((ROPE_LO, tm), F32)
    pad = SLOT - ROPE_LO - MLA_ROPE
    cos_t = jnp.concatenate([ones, c, c, jnp.ones((pad, tm), F32)], axis=0)
    sin_t = jnp.concatenate([zeros, -s, s, jnp.zeros((pad, tm), F32)], axis=0)
    cos_ref[0] = cos_t.T
    sin_ref[0] = sin_t.T


def _rope_tables(positions):
    b, s = positions.shape
    tm = s
    half = ROPE_HALF
    inv_freq = jnp.power(ROPE_THETA, -jnp.arange(half, dtype=F32) / half).reshape(half, 1)
    out = jax.ShapeDtypeStruct((b, s, SLOT), F32)
    return pl.pallas_call(
        _rope_kernel,
        out_shape=(out, out),
        grid=(b, s // tm),
        in_specs=[pl.BlockSpec((1, 1, tm), lambda i, j: (i, 0, j)),
                  pl.BlockSpec((half, 1), lambda i, j: (0, 0))],
        out_specs=(pl.BlockSpec((1, tm, SLOT), lambda i, j: (i, j, 0)),
                   pl.BlockSpec((1, tm, SLOT), lambda i, j: (i, j, 0))),
        compiler_params=_cparams(2),
        name="rope_tables",
    )(positions.reshape(b, 1, s), inv_freq)


def _t5_bucket(dist):
    max_exact = REL_BUCKETS // 2
    n = jnp.maximum(dist, 0)
    nf = jnp.maximum(n, 1).astype(F32)
    large = max_exact + (jnp.log(nf / max_exact) / math.log(REL_MAX_DIST / max_exact)
                         * (REL_BUCKETS - max_exact)).astype(jnp.int32)
    large = jnp.minimum(large, REL_BUCKETS - 1)
    return jnp.where(n < max_exact, n, large)


def _bias_kernel(tab_ref, bucket_ref, o_ref):
    bucket = bucket_ref[...]
    for h in range(SW_HEADS):
        acc = jnp.zeros(bucket.shape, F32)
        for bkt in range(REL_BUCKETS):
            acc = jnp.where(bucket == bkt, tab_ref[bkt, h], acc)
        o_ref[h] = jnp.where(bucket >= 0, acc * LOG2E, NEG)


def _window_bias(rel_table):
    nk = WINDOW + TQ_SW
    key = jnp.arange(nk)[:, None]
    qry = jnp.arange(TQ_SW)[None, :]
    dist = qry + WINDOW - key
    valid = (dist >= 0) & (dist < WINDOW)
    bucket = jnp.where(valid, _t5_bucket(dist), -1).astype(jnp.int32)
    return pl.pallas_call(
        _bias_kernel,
        out_shape=jax.ShapeDtypeStruct((SW_HEADS, nk, TQ_SW), F32),
        in_specs=[pl.BlockSpec(memory_space=pltpu.SMEM),
                  pl.BlockSpec(memory_space=pltpu.VMEM)],
        out_specs=pl.BlockSpec(memory_space=pltpu.VMEM),
        name="window_bias",
    )(rel_table, bucket)


def _half_rms(x, gain, scale):
    lo = lax.broadcasted_iota(jnp.int32, x.shape, 1) < HALF
    sq = x * x
    s_lo = jnp.sum(jnp.where(lo, sq, 0.0), axis=-1, keepdims=True)
    s_hi = jnp.sum(jnp.where(lo, 0.0, sq), axis=-1, keepdims=True)
    r = jnp.where(lo, lax.rsqrt(s_lo * (1.0 / HEAD_DIM) + EPS), lax.rsqrt(s_hi * (1.0 / HEAD_DIM) + EPS))
    return x * r * (gain * scale)


def _prep_kernel(x_ref, mod_ref, n1g_ref, win_ref, cqg_ref, wuq_ref, ckvg_ref, wukv_ref,
                 gq_ref, gqs_ref, gk_ref, gks_ref, swqg_ref, swkg_ref, cos_ref, sin_ref,
                 sbq_ref, sbk_ref, sbvt_ref, mq_ref, mk_ref, mvt_ref, swq_ref, swk_ref, swvt_ref):
    x = x_ref[0]
    mod = mod_ref[0]
    shift1, scale1 = mod[0:1], mod[1:2]
    h = (_rms(x, D_MODEL) * n1g_ref[...] * (1.0 + scale1) + shift1).astype(BF16)

    proj_mla = _dot(h, win_ref[:, C_CQ:C_SWQ])
    proj_sb = _dot(h, win_ref[:, C_SBQ:C_CQ])
    cq = proj_mla[:, 0:MLA_Q_RANK]
    ckv = proj_mla[:, C_CKV - C_CQ:C_CKV - C_CQ + MLA_KV_RANK]
    krope = proj_mla[:, C_KROPE - C_CQ:C_KROPE - C_CQ + SLOT]
    krope_sw = proj_mla[:, C_KROPE_SW - C_CQ:C_KROPE_SW - C_CQ + SLOT]
    cqn = (_rms(cq, MLA_Q_RANK) * cqg_ref[...]).astype(BF16)
    ckvn = (_rms(ckv, MLA_KV_RANK) * ckvg_ref[...]).astype(BF16)
    qraw = _dot(cqn, wuq_ref[...])
    kv = _dot(ckvn, wukv_ref[...])
    proj_sw = _dot(h, win_ref[:, C_SWQ:N_IN])

    sbq_ref[0] = (proj_sb[:, C_SBQ:C_SBQ + 256] * (HEAD_DIM ** -0.5 * LOG2E)).astype(BF16)
    sbk_ref[0] = proj_sb[:, C_SBK:C_SBK + 256].astype(BF16)
    sbvt_ref[0] = proj_sb[:, C_SBV:C_SBV + 256].T.astype(BF16)

    cos = cos_ref[0]
    sin = sin_ref[0]

    nq = MLA_HEADS * SLOT
    q_scale = MLA_QK ** -0.5 * LOG2E
    q_cos, q_sin = gq_ref[...] * cos, gqs_ref[...] * sin
    for hd in range(MLA_HEADS):
        slot = qraw[:, hd * SLOT:(hd + 1) * SLOT]
        swapped = qraw[:, nq + hd * SLOT:nq + (hd + 1) * SLOT]
        r = lax.rsqrt(jnp.sum(slot * slot, axis=-1, keepdims=True) * (1.0 / MLA_QK) + EPS) * q_scale
        mq_ref[0, :, hd * SLOT:(hd + 1) * SLOT] = ((slot * q_cos + swapped * q_sin) * r).astype(BF16)

    mvt_ref[0] = kv[:, nq:nq + MLA_HEADS * MLA_V].T.astype(BF16)
    k_cos = gk_ref[...] * cos
    k_rot = krope_sw * (gks_ref[...] * sin)
    for hd in range(MLA_HEADS):
        slot = kv[:, hd * SLOT:(hd + 1) * SLOT] + krope
        r = lax.rsqrt(jnp.sum(slot * slot, axis=-1, keepdims=True) * (1.0 / MLA_QK) + EPS)
        mk_ref[0, :, hd * SLOT:(hd + 1) * SLOT] = ((slot * k_cos + k_rot) * r).astype(BF16)

    for g in range(SW_HEADS // 2):
        xq = proj_sw[:, g * SLOT:(g + 1) * SLOT]
        swq_ref[0, :, g * SLOT:(g + 1) * SLOT] = _half_rms(
            xq, swqg_ref[...], HEAD_DIM ** -0.5 * LOG2E).astype(BF16)
    swk_ref[0] = _half_rms(proj_mla[:, C_SWK - C_CQ:C_SWK - C_CQ + SLOT], swkg_ref[...], 1.0).astype(BF16)
    swvt_ref[0] = proj_sw[:, C_SWV - C_SWQ:C_SWV - C_SWQ + SLOT].T.astype(BF16)


def _prep(x, mods, n1g, w_in, cqg, w_uq, ckvg, w_ukv, gq, gqs, gk, gks, swqg, swkg, cos_t, sin_t):
    b, s, d = x.shape
    tm = TM_PREP
    row = lambda w: pl.BlockSpec((1, tm, w), lambda i, j: (i, j, 0))
    colt = lambda w: pl.BlockSpec((1, w, tm), lambda i, j: (i, 0, j))
    full = lambda a: pl.BlockSpec(a.shape, lambda i, j: (0,) * a.ndim)
    act = lambda w: jax.ShapeDtypeStruct((b, s, w), BF16)
    actt = lambda w: jax.ShapeDtypeStruct((b, w, s), BF16)
    return pl.pallas_call(
        _prep_kernel,
        out_shape=(act(256), act(256), actt(256), act(768), act(768), actt(384),
                   act(384), act(128), actt(128)),
        grid=(b, s // tm),
        in_specs=[row(d), pl.BlockSpec((1, 6, d), lambda i, j: (i, 0, 0)), full(n1g), full(w_in),
                  full(cqg), full(w_uq), full(ckvg), full(w_ukv), full(gq), full(gqs), full(gk),
                  full(gks), full(swqg), full(swkg), row(SLOT), row(SLOT)],
        out_specs=(row(256), row(256), colt(256), row(768), row(768), colt(384),
                   row(384), row(128), colt(128)),
        compiler_params=_cparams(2),
        name="prep_qkv",
    )(x, mods, n1g, w_in, cqg, w_uq, ckvg, w_ukv, gq, gqs, gk, gks, swqg, swkg, cos_t, sin_t)


def _half_mask(q, half):
    lane = lax.broadcasted_iota(jnp.int32, q.shape, 1)
    keep = (lane < HALF) if half == 0 else (lane >= HALF)
    return jnp.where(keep, q, jnp.zeros_like(q))


def _store_pair(o_ref, g, out_lo, out_hi, row0=0):
    pair = jnp.concatenate([out_lo, out_hi], axis=0)
    o_ref[0, row0:row0 + pair.shape[1], g * SLOT:(g + 1) * SLOT] = pair.T.astype(o_ref.dtype)


SIGN_BIT = -2 ** 31
JOBS = (("a", 0), ("b", 0), ("b", 1), ("a", 1), ("b", 2), ("b", 3), ("a", 2), ("b", 4), ("b", 5), ("a", 3))


def _softplus2(z):
    neg_abs = lax.bitcast_convert_type(lax.bitcast_convert_type(z, jnp.int32) | SIGN_BIT, F32)
    return jnp.maximum(z, 0.0) + jnp.log(1.0 + jnp.exp2(neg_abs)) * LOG2E


def _hi_lo(x):
    hi = x.astype(BF16)
    lo = (x - hi.astype(F32)).astype(BF16)
    return jnp.concatenate([hi, lo], axis=0)


def _causal_kernel(sink_ref, aq_ref, ak_ref, avt_ref, tri_ref, bq_ref, bk_ref, bvt_ref,
                   cq_ref, ck_ref, cvt_ref, bias_ref, oa_ref, ob_ref, oc_ref):
    qi = pl.program_id(1)
    n_diag = TQ // TK
    a_heads, b_heads = range(SB_HEADS), range(MLA_HEADS)

    def put(full, part, c0):
        return part if c0 == 0 else jnp.concatenate([full[:, :c0], part], axis=1)

    def run_jobs(jobs):
        ca = [(jnp.zeros((HEAD_DIM, TQ), F32), jnp.zeros((1, TQ), F32)) for _ in a_heads]
        cb = [(jnp.full((1, TQ), NEG, F32), jnp.zeros((1, TQ), F32), jnp.zeros((MLA_V, TQ), F32))
              for _ in b_heads]
        st = [None] * len(jobs)
        out_c = {}

        def geometry(t):
            mixer, hd, blk, diag = jobs[t]
            c0 = 0 if diag is None else diag * TK
            row = lax.broadcasted_iota(jnp.int32, (TK, TQ - c0), 0)
            col = lax.broadcasted_iota(jnp.int32, (TK, TQ - c0), 1)
            return mixer, hd, blk * TK, diag is not None, c0, row, col

        def window(t):
            _, slot, sub, key0 = jobs[t]
            lo = max(key0, 0)
            head = slot // 2 + (slot % 2) * (SW_HEADS // SW_KV_HEADS)
            return slot, sub, head, lo, key0 + WINDOW + TQ_SW, lo - key0

        def scores(t):
            if jobs[t][0] == "c":
                slot, sub, head, lo, hi, skip = window(t)
                q = _half_mask(cq_ref[0, sub * TQ_SW:(sub + 1) * TQ_SW, (slot // 2) * SLOT:(slot // 2 + 1) * SLOT],
                               slot % 2)
                st[t] = _nt_dot(ck_ref[0, lo:hi, :], q) + bias_ref[head, skip:, :]
                return
            mixer, hd, start, masked, c0, row, col = geometry(t)
            if mixer == "a":
                g, half = divmod(hd, 2)
                q = _half_mask(aq_ref[0, c0:, g * SLOT:(g + 1) * SLOT], half)
                st[t] = _nt_dot(ak_ref[0, start:start + TK, g * SLOT:(g + 1) * SLOT], q)
            else:
                sc = _nt_dot(bk_ref[0, start:start + TK, hd * SLOT:(hd + 1) * SLOT],
                             bq_ref[0, c0:, hd * SLOT:(hd + 1) * SLOT])
                st[t] = jnp.where(row <= col, sc, NEG) if masked else sc

        def second(t):
            if jobs[t][0] == "c":
                slot, sub, head, lo, hi, skip = window(t)
                sink = sink_ref[head] * LOG2E
                m = jnp.maximum(jnp.max(st[t], axis=0, keepdims=True), sink)
                p = jnp.exp2(st[t] - m)
                denom = jnp.sum(p, axis=0, keepdims=True) + jnp.exp2(sink - m)
                pv = _dot(cvt_ref[0, (slot % 2) * HEAD_DIM:(slot % 2 + 1) * HEAD_DIM, lo:hi], p.astype(BF16))
                out_c[(sub, slot)] = pv * (1.0 / denom)
                st[t] = None
                return
            mixer, hd, start, masked, c0, row, col = geometry(t)
            if mixer == "a":
                drop = _softplus2(st[t])
                if masked:
                    drop = jnp.where(row < col, drop, 0.0)
                st[t] = (st[t], _dot(tri_ref[...], _hi_lo(drop)))
            else:
                m_all, l_all, acc_all = cb[hd]
                m_old = m_all[:, c0:]
                m_new = jnp.maximum(m_old, jnp.max(st[t], axis=0, keepdims=True))
                alpha = jnp.exp2(m_old - m_new)
                p = jnp.exp2(st[t] - m_new)
                l_new = alpha * l_all[:, c0:] + jnp.sum(p, axis=0, keepdims=True)
                pv = _dot(bvt_ref[0, hd * MLA_V:(hd + 1) * MLA_V, start:start + TK], p.astype(BF16))
                cb[hd] = (put(m_all, m_new, c0), put(l_all, l_new, c0),
                          put(acc_all, alpha * acc_all[:, c0:] + pv, c0))

        def third(t):
            if jobs[t][0] == "c":
                return
            mixer, hd, start, masked, c0, row, col = geometry(t)
            if mixer == "a":
                z, incl = st[t]
                acc, run = ca[hd]
                w = jnp.exp2(z + incl + run[:, c0:])
                if masked:
                    w = jnp.where(row < col, w, 0.0)
                pv = _dot(avt_ref[0, hd * HEAD_DIM:(hd + 1) * HEAD_DIM, start:start + TK], w.astype(BF16))
                ca[hd] = (put(acc, acc[:, c0:] + pv, c0), put(run, run[:, c0:] + incl[0:1, :], c0))
            st[t] = None

        n = len(jobs)
        for t in range(n + 2):
            if t < n:
                scores(t)
            if 0 <= t - 1 < n:
                second(t - 1)
            if 0 <= t - 2 < n:
                third(t - 2)
        for g in range(SB_HEADS // 2):
            _store_pair(oa_ref, g, ca[2 * g][0], ca[2 * g + 1][0])
        outs = [acc * (1.0 / l) for (_, l, acc) in cb]
        for g in range(MLA_HEADS // 2):
            _store_pair(ob_ref, g, outs[2 * g], outs[2 * g + 1])
        for sub in range(TQ // TQ_SW):
            for g in range(SW_HEADS // 2):
                _store_pair(oc_ref, g, out_c[(sub, 2 * g)], out_c[(sub, 2 * g + 1)], sub * TQ_SW)

    def jobs_for(qb):
        n_past = qb * n_diag
        past = [(j, None) for j in reversed(range(n_past))]
        a_blocks = [(n_past + d, d) for d in reversed(range(n_diag))] + past
        b_blocks = [(n_past + d, d) for d in range(n_diag)] + past
        sweep = [(mixer, hd) + (a_blocks[i] if mixer == "a" else b_blocks[i])
                 for i in range(len(a_blocks)) for (mixer, hd) in JOBS]
        local = [("c", slot, sub, qb * TQ + sub * TQ_SW - WINDOW)
                 for sub in range(TQ // TQ_SW) for slot in range(SW_HEADS)]
        jobs = []
        for job in sweep:
            jobs.append(job)
            if local:
                jobs.append(local.pop(0))
        return jobs

    for qb in range(ak_ref.shape[1] // TQ):
        pl.when(qi == qb)(functools.partial(run_jobs, jobs_for(qb)))


def _attention(sinks, aq, ak, avt, bq, bk, bvt, cq, ck, cvt, bias):
    b, s, wa = aq.shape
    wb, wbv, wc = bq.shape[2], bvt.shape[1], cq.shape[2]
    tri = -(jnp.arange(TK)[None, :] >= jnp.arange(TK)[:, None]).astype(BF16)
    tri = jnp.concatenate([tri, tri], axis=1)
    qblk = lambda w: pl.BlockSpec((1, TQ, w), lambda i, j: (i, j, 0))
    seq = lambda w: pl.BlockSpec((1, s, w), lambda i, j: (i, 0, 0))
    seqt = lambda w: pl.BlockSpec((1, w, s), lambda i, j: (i, 0, 0))
    const = lambda a: pl.BlockSpec(a.shape, lambda i, j: (0,) * a.ndim)
    out = lambda w: jax.ShapeDtypeStruct((b, s, w), BF16)
    return pl.pallas_call(
        _causal_kernel,
        out_shape=(out(wa), out(wbv), out(wc)),
        grid=(b, s // TQ),
        in_specs=[pl.BlockSpec(memory_space=pltpu.SMEM),
                  qblk(wa), seq(wa), seqt(wa), const(tri),
                  qblk(wb), seq(wb), seqt(wbv),
                  qblk(wc), seq(ck.shape[2]), seqt(cvt.shape[1]), const(bias)],
        out_specs=(qblk(wa), qblk(wbv), qblk(wc)),
        compiler_params=_cparams(2),
        name="attention",
    )(sinks, aq, ak, avt, tri, bq, bk, bvt, cq, ck, cvt, bias)


def _mlp_kernel(x_ref, mod_ref, oa_ref, ob_ref, oc_ref, wo_ref, n2g_ref, wup_ref, cw_ref, cb_ref,
                wdn_ref, out_ref, carry_ref):
    si = pl.program_id(1)
    tm = x_ref.shape[1]
    mod = mod_ref[0]
    gate1, shift2, scale2, gate2 = mod[2:3], mod[3:4], mod[4:5], mod[5:6]
    na, nb = oa_ref.shape[2], ob_ref.shape[2]
    att = (_dot(oa_ref[0], wo_ref[0:na, :]) + _dot(ob_ref[0], wo_ref[na:na + nb, :])
           + _dot(oc_ref[0], wo_ref[na + nb:, :]))
    x1 = x_ref[0] + gate1 * att
    h2 = (_rms(x1, D_MODEL) * n2g_ref[...] * (1.0 + scale2) + shift2).astype(BF16)

    @pl.when(si == 0)
    def _():
        carry_ref[...] = jnp.zeros_like(carry_ref)

    def up(c):
        lo = c * FF_CHUNK
        return (_dot(h2, wup_ref[:, lo:lo + FF_CHUNK]),
                _dot(h2, wup_ref[:, D_FF + lo:D_FF + lo + FF_CHUNK]))

    def conv(u, col):
        cols = slice(col, col + FF_CHUNK)
        prev = carry_ref[:, cols]
        carry_ref[:, cols] = u[tm - CARRY_ROWS:, :]
        ext = jnp.concatenate([prev, u], axis=0)
        u1 = ext[CARRY_ROWS - 1:CARRY_ROWS - 1 + tm, :]
        u2 = ext[CARRY_ROWS - 2:CARRY_ROWS - 2 + tm, :]
        cw = cw_ref[:, cols]
        return u * cw[2:3] + u1 * cw[1:2] + u2 * cw[0:1] + cb_ref[:, cols]

    n_chunks = D_FF // FF_CHUNK
    u_next = up(0)
    for c in range(n_chunks):
        u_gate, u_val = u_next
        if c + 1 < n_chunks:
            u_next = up(c + 1)
        gate = conv(u_gate, c * FF_CHUNK)
        val = conv(u_val, D_FF + c * FF_CHUNK)
        a = (gate * jax.nn.sigmoid(gate) * val).astype(BF16)
        part = _dot(a, wdn_ref[c * FF_CHUNK:(c + 1) * FF_CHUNK, :])
        acc = part if c == 0 else acc + part
    out_ref[0] = x1 + gate2 * acc


def _mlp(x, mods, oa, ob, oc, w_out, n2g, w_up, conv_w, conv_b, w_down):
    b, s, d = x.shape
    tm = TM_MLP
    row = lambda w: pl.BlockSpec((1, tm, w), lambda i, j: (i, j, 0))
    const = lambda a: pl.BlockSpec(a.shape, lambda i, j: (0,) * a.ndim, pipeline_mode=pl.Buffered(1))
    return pl.pallas_call(
        _mlp_kernel,
        out_shape=jax.ShapeDtypeStruct((b, s, d), F32),
        grid=(b, s // tm),
        in_specs=[row(d), pl.BlockSpec((1, 6, d), lambda i, j: (i, 0, 0)),
                  row(oa.shape[2]), row(ob.shape[2]), row(oc.shape[2]),
                  const(w_out), const(n2g), const(w_up), const(conv_w), const(conv_b), const(w_down)],
        out_specs=row(d),
        scratch_shapes=[pltpu.VMEM((CARRY_ROWS, 2 * D_FF), F32)],
        compiler_params=_cparams(2),
        name="outproj_mlp",
    )(x, mods, oa, ob, oc, w_out, n2g, w_up, conv_w, conv_b, w_down)


def _slot_gain(g, swap):
    z = jnp.zeros((SLOT - MLA_QK,), F32)
    lo, hi = g[ROPE_LO:ROPE_LO + ROPE_HALF], g[ROPE_LO + ROPE_HALF:MLA_QK]
    if swap:
        return jnp.concatenate([jnp.zeros((MLA_NOPE,), F32), hi, lo, z]).reshape(1, SLOT)
    return jnp.concatenate([g, z]).reshape(1, SLOT)


def _layout_w_in(w):
    d = w.shape[0]
    z = lambda n: jnp.zeros((d, n), w.dtype)
    kr = w[:, 1152:1184]
    swq = w[:, 1184:1568].reshape(d, SW_HEADS, HEAD_DIM)
    order = [0, 3, 1, 4, 2, 5]
    swq = swq[:, order, :].reshape(d, SW_HEADS * HEAD_DIM)
    cols = [w[:, 0:1152],
            z(MLA_NOPE), kr, z(SLOT - MLA_QK),
            z(MLA_NOPE), kr[:, ROPE_HALF:], kr[:, :ROPE_HALF], z(SLOT - MLA_QK),
            w[:, 1568:1696], swq, w[:, 1696:1824]]
    return jnp.concatenate(cols, axis=1).astype(BF16)


def _layout_w_uq(w):
    r = w.shape[0]
    w = w.reshape(r, MLA_HEADS, MLA_QK)
    nope, x1, x2 = w[..., :MLA_NOPE], w[..., MLA_NOPE:MLA_NOPE + ROPE_HALF], w[..., MLA_NOPE + ROPE_HALF:]
    z = jnp.zeros((r, MLA_HEADS, SLOT - MLA_QK), w.dtype)
    plain = jnp.concatenate([nope, x1, x2, z], axis=-1).reshape(r, MLA_HEADS * SLOT)
    swapped = jnp.concatenate([jnp.zeros_like(nope), x2, x1, z], axis=-1).reshape(r, MLA_HEADS * SLOT)
    return jnp.concatenate([plain, swapped], axis=1).astype(BF16)


def _layout_w_ukv(w):
    r = w.shape[0]
    w = w.reshape(r, MLA_HEADS, MLA_NOPE + MLA_V)
    k_nope, v = w[..., :MLA_NOPE], w[..., MLA_NOPE:]
    k_slots = jnp.concatenate([k_nope, jnp.zeros((r, MLA_HEADS, SLOT - MLA_NOPE), w.dtype)], axis=-1)
    return jnp.concatenate([k_slots.reshape(r, MLA_HEADS * SLOT), v.reshape(r, MLA_HEADS * MLA_V)],
                           axis=1).astype(BF16)


def _layout_w_out(w):
    n_ab = SB_HEADS * HEAD_DIM + MLA_HEADS * MLA_V
    sw = w[n_ab:].reshape(SW_HEADS, HEAD_DIM, w.shape[1])[jnp.array([0, 3, 1, 4, 2, 5])]
    return jnp.concatenate([w[:n_ab], sw.reshape(SW_HEADS * HEAD_DIM, w.shape[1])], axis=0).astype(BF16)


def kernel(x, c, positions, rel_table, norm1_g, norm2_g, w_ada, b_ada, w_in, mla_cq_g, w_uq, mla_ckv_g,
           w_ukv, mla_qn_g, mla_kn_g, sw_qn_g, sw_kn_g, sw_sinks, w_out, w_up, conv_w, conv_b, w_down):
    depth = w_in.shape[0]
    b = x.shape[0]
    mods = _mods(c, w_ada, b_ada).reshape(depth, b, 6, D_MODEL)
    cos_t, sin_t = _rope_tables(positions)
    bias = _window_bias(rel_table)
    row = lambda v: v.reshape(1, -1).astype(F32)
    two = lambda v: jnp.concatenate([v, v]).reshape(1, SLOT).astype(F32)
    for l in range(depth):
        qkv = _prep(x, mods[l], row(norm1_g[l]), _layout_w_in(w_in[l]), row(mla_cq_g[l]),
                    _layout_w_uq(w_uq[l]), row(mla_ckv_g[l]), _layout_w_ukv(w_ukv[l]),
                    _slot_gain(mla_qn_g[l], False), _slot_gain(mla_qn_g[l], True),
                    _slot_gain(mla_kn_g[l], False), _slot_gain(mla_kn_g[l], True),
                    two(sw_qn_g[l]), two(sw_kn_g[l]), cos_t, sin_t)
        sbq, sbk, sbvt, mq, mk, mvt, swq, swk, swvt = qkv
        o_a, o_b, o_c = _attention(sw_sinks[l], sbq, sbk, sbvt, mq, mk, mvt, swq, swk, swvt, bias)
        x = _mlp(x, mods[l], o_a, o_b, o_c, _layout_w_out(w_out[l]), row(norm2_g[l]),
                 w_up[l].astype(BF16), conv_w[l], row(conv_b[l]), w_down[l].astype(BF16))
    return x
```

```python
import functools
import math

import numpy as np
import jax
import jax.numpy as jnp
from jax import lax
from jax.experimental import pallas as pl
from jax.experimental.pallas import tpu as pltpu

F32 = jnp.float32
BF16 = jnp.bfloat16

D_MODEL = 1024
HEAD_DIM = 64
SB_HEADS = 4
MLA_HEADS = 6
MLA_Q_RANK = 256
MLA_KV_RANK = 128
MLA_NOPE = 64
MLA_ROPE = 32
MLA_V = 64
MLA_QK = MLA_NOPE + MLA_ROPE
ROPE_THETA = 10000.0
SW_HEADS = 6
SW_KV_HEADS = 2
WINDOW = 128
REL_BUCKETS = 32
REL_MAX_DIST = 128
D_FF = 2816
CONV_W = 3
EPS = 1e-6
NEG = -1e30

LANES = 128
SLOT = LANES
HALF = SLOT // 2

C_SBQ, C_SBK, C_SBV = 0, 256, 512
C_CQ = 768
C_CKV = 1024
C_KROPE = 1152
C_KROPE_SW = 1280
C_SWK = 1408
C_SWQ = 1536
C_SWV = 1920
N_IN = 2048

ROPE_LO = MLA_NOPE
ROPE_HALF = MLA_ROPE // 2

TM_PREP = 512
TM_MLP = 512
TQ = 512
TK = 256
TQ_SW = 256
LOG2E = math.log2(math.e)
FF_CHUNK = 256
CARRY_ROWS = 8

VMEM_LIMIT = 56 * 1024 * 1024


def _cparams(n_axes):
    return pltpu.CompilerParams(dimension_semantics=("arbitrary",) * n_axes,
                                vmem_limit_bytes=VMEM_LIMIT)


def _rms(x, n):
    return x * lax.rsqrt(jnp.sum(x * x, axis=-1, keepdims=True) * (1.0 / n) + EPS)


def _nt_dot(a, b):
    return lax.dot_general(a, b, (((1,), (1,)), ((), ())), preferred_element_type=F32)


def _dot(a, b):
    return jnp.dot(a, b, preferred_element_type=F32)


def _mods_kernel(c_ref, w_ref, b_ref, o_ref):
    c = c_ref[...]
    a = (c * jax.nn.sigmoid(c)).astype(BF16)
    o_ref[0] = _dot(a, w_ref[0].astype(BF16)) + b_ref[0]


def _mods(c, w_ada, b_ada):
    depth, d, n = w_ada.shape
    b = c.shape[0]
    tn = 1536
    return pl.pallas_call(
        _mods_kernel,
        out_shape=jax.ShapeDtypeStruct((depth, b, n), F32),
        grid=(depth, n // tn),
        in_specs=[pl.BlockSpec((b, d), lambda l, j: (0, 0)),
                  pl.BlockSpec((1, d, tn), lambda l, j: (l, 0, j)),
                  pl.BlockSpec((1, 1, tn), lambda l, j: (l, 0, j))],
        out_specs=pl.BlockSpec((1, b, tn), lambda l, j: (l, 0, j)),
        compiler_params=_cparams(2),
        name="adaln_mods",
    )(c, w_ada, b_ada.reshape(depth, 1, n))


def _rope_kernel(pos_ref, invf_ref, cos_ref, sin_ref):
    pos = pos_ref[0].astype(F32)
    ang = invf_ref[...] * pos
    c = jnp.cos(ang)
    s = jnp.sin(ang)
    tm = pos.shape[1]
    ones = jnp.ones((ROPE_LO, tm), F32)
    zeros = jnp.zeros((ROPE_LO, tm), F32)
    pad = SLOT - ROPE_LO - MLA_ROPE
    cos_t = jnp.concatenate([ones, c, c, jnp.ones((pad, tm), F32)], axis=0)
    sin_t = jnp.concatenate([zeros, -s, s, jnp.zeros((pad, tm), F32)], axis=0)
    cos_ref[0] = cos_t.T
    sin_ref[0] = sin_t.T


def _rope_tables(positions):
    b, s = positions.shape
    tm = s
    half = ROPE_HALF
    inv_freq = jnp.power(ROPE_THETA, -jnp.arange(half, dtype=F32) / half).reshape(half, 1)
    out = jax.ShapeDtypeStruct((b, s, SLOT), F32)
    return pl.pallas_call(
        _rope_kernel,
        out_shape=(out, out),
        grid=(b, s // tm),
        in_specs=[pl.BlockSpec((1, 1, tm), lambda i, j: (i, 0, j)),
                  pl.BlockSpec((half, 1), lambda i, j: (0, 0))],
        out_specs=(pl.BlockSpec((1, tm, SLOT), lambda i, j: (i, j, 0)),
                   pl.BlockSpec((1, tm, SLOT), lambda i, j: (i, j, 0))),
        compiler_params=_cparams(2),
        name="rope_tables",
    )(positions.reshape(b, 1, s), inv_freq)


def _t5_bucket(dist):
    max_exact = REL_BUCKETS // 2
    n = jnp.maximum(dist, 0)
    nf = jnp.maximum(n, 1).astype(F32)
    large = max_exact + (jnp.log(nf / max_exact) / math.log(REL_MAX_DIST / max_exact)
                         * (REL_BUCKETS - max_exact)).astype(jnp.int32)
    large = jnp.minimum(large, REL_BUCKETS - 1)
    return jnp.where(n < max_exact, n, large)


def _bias_kernel(tab_ref, bucket_ref, o_ref):
    bucket = bucket_ref[...]
    for h in range(SW_HEADS):
        acc = jnp.zeros(bucket.shape, F32)
        for bkt in range(REL_BUCKETS):
            acc = jnp.where(bucket == bkt, tab_ref[bkt, h], acc)
        o_ref[h] = jnp.where(bucket >= 0, acc * LOG2E, NEG)


def _window_bias(rel_table):
    nk = WINDOW + TQ_SW
    key = jnp.arange(nk)[:, None]
    qry = jnp.arange(TQ_SW)[None, :]
    dist = qry + WINDOW - key
    valid = (dist >= 0) & (dist < WINDOW)
    bucket = jnp.where(valid, _t5_bucket(dist), -1).astype(jnp.int32)
    return pl.pallas_call(
        _bias_kernel,
        out_shape=jax.ShapeDtypeStruct((SW_HEADS, nk, TQ_SW), F32),
        in_specs=[pl.BlockSpec(memory_space=pltpu.SMEM),
                  pl.BlockSpec(memory_space=pltpu.VMEM)],
        out_specs=pl.BlockSpec(memory_space=pltpu.VMEM),
        name="window_bias",
    )(rel_table, bucket)


def _half_rms(x, gain, scale):
    lo = lax.broadcasted_iota(jnp.int32, x.shape, 1) < HALF
    sq = x * x
    s_lo = jnp.sum(jnp.where(lo, sq, 0.0), axis=-1, keepdims=True)
    s_hi = jnp.sum(jnp.where(lo, 0.0, sq), axis=-1, keepdims=True)
    r = jnp.where(lo, lax.rsqrt(s_lo * (1.0 / HEAD_DIM) + EPS), lax.rsqrt(s_hi * (1.0 / HEAD_DIM) + EPS))
    return x * r * (gain * scale)


def _prep_kernel(x_ref, mod_ref, n1g_ref, win_ref, cqg_ref, wuq_ref, ckvg_ref, wukv_ref,
                 gq_ref, gqs_ref, gk_ref, gks_ref, swqg_ref, swkg_ref, cos_ref, sin_ref,
                 sbq_ref, sbk_ref, sbvt_ref, mq_ref, mk_ref, mvt_ref, swq_ref, swk_ref, swvt_ref):
    x = x_ref[0]
    mod = mod_ref[0]
    shift1, scale1 = mod[0:1], mod[1:2]
    h = (_rms(x, D_MODEL) * n1g_ref[...] * (1.0 + scale1) + shift1).astype(BF16)

    proj_mla = _dot(h, win_ref[:, C_CQ:C_SWQ])
    proj_sb = _dot(h, win_ref[:, C_SBQ:C_CQ])
    cq = proj_mla[:, 0:MLA_Q_RANK]
    ckv = proj_mla[:, C_CKV - C_CQ:C_CKV - C_CQ + MLA_KV_RANK]
    krope = proj_mla[:, C_KROPE - C_CQ:C_KROPE - C_CQ + SLOT]
    krope_sw = proj_mla[:, C_KROPE_SW - C_CQ:C_KROPE_SW - C_CQ + SLOT]
    cqn = (_rms(cq, MLA_Q_RANK) * cqg_ref[...]).astype(BF16)
    ckvn = (_rms(ckv, MLA_KV_RANK) * ckvg_ref[...]).astype(BF16)
    qraw = _dot(cqn, wuq_ref[...])
    kv = _dot(ckvn, wukv_ref[...])
    proj_sw = _dot(h, win_ref[:, C_SWQ:N_IN])

    sbq_ref[0] = (proj_sb[:, C_SBQ:C_SBQ + 256] * (HEAD_DIM ** -0.5 * LOG2E)).astype(BF16)
    sbk_ref[0] = proj_sb[:, C_SBK:C_SBK + 256].astype(BF16)
    sbvt_ref[0] = proj_sb[:, C_SBV:C_SBV + 256].T.astype(BF16)

    cos = cos_ref[0]
    sin = sin_ref[0]

    nq = MLA_HEADS * SLOT
    q_scale = MLA_QK ** -0.5 * LOG2E
    q_cos, q_sin = gq_ref[...] * cos, gqs_ref[...] * sin
    for hd in range(MLA_HEADS):
        slot = qraw[:, hd * SLOT:(hd + 1) * SLOT]
        swapped = qraw[:, nq + hd * SLOT:nq + (hd + 1) * SLOT]
        r = lax.rsqrt(jnp.sum(slot * slot, axis=-1, keepdims=True) * (1.0 / MLA_QK) + EPS) * q_scale
        mq_ref[0, :, hd * SLOT:(hd + 1) * SLOT] = ((slot * q_cos + swapped * q_sin) * r).astype(BF16)

    mvt_ref[0] = kv[:, nq:nq + MLA_HEADS * MLA_V].T.astype(BF16)
    k_cos = gk_ref[...] * cos
    k_rot = krope_sw * (gks_ref[...] * sin)
    for hd in range(MLA_HEADS):
        slot = kv[:, hd * SLOT:(hd + 1) * SLOT] + krope
        r = lax.rsqrt(jnp.sum(slot * slot, axis=-1, keepdims=True) * (1.0 / MLA_QK) + EPS)
        mk_ref[0, :, hd * SLOT:(hd + 1) * SLOT] = ((slot * k_cos + k_rot) * r).astype(BF16)

    for g in range(SW_HEADS // 2):
        xq = proj_sw[:, g * SLOT:(g + 1) * SLOT]
        swq_ref[0, :, g * SLOT:(g + 1) * SLOT] = _half_rms(
            xq, swqg_ref[...], HEAD_DIM ** -0.5 * LOG2E).astype(BF16)
    swk_ref[0] = _half_rms(proj_mla[:, C_SWK - C_CQ:C_SWK - C_CQ + SLOT], swkg_ref[...], 1.0).astype(BF16)
    swvt_ref[0] = proj_sw[:, C_SWV - C_SWQ:C_SWV - C_SWQ + SLOT].T.astype(BF16)


def _prep(x, mods, n1g, w_in, cqg, w_uq, ckvg, w_ukv, gq, gqs, gk, gks, swqg, swkg, cos_t, sin_t):
    b, s, d = x.shape
    tm = TM_PREP
    row = lambda w: pl.BlockSpec((1, tm, w), lambda i, j: (i, j, 0))
    colt = lambda w: pl.BlockSpec((1, w, tm), lambda i, j: (i, 0, j))
    full = lambda a: pl.BlockSpec(a.shape, lambda i, j: (0,) * a.ndim)
    act = lambda w: jax.ShapeDtypeStruct((b, s, w), BF16)
    actt = lambda w: jax.ShapeDtypeStruct((b, w, s), BF16)
    return pl.pallas_call(
        _prep_kernel,
        out_shape=(act(256), act(256), actt(256), act(768), act(768), actt(384),
                   act(384), act(128), actt(128)),
        grid=(b, s // tm),
        in_specs=[row(d), pl.BlockSpec((1, 6, d), lambda i, j: (i, 0, 0)), full(n1g), full(w_in),
                  full(cqg), full(w_uq), full(ckvg), full(w_ukv), full(gq), full(gqs), full(gk),
                  full(gks), full(swqg), full(swkg), row(SLOT), row(SLOT)],
        out_specs=(row(256), row(256), colt(256), row(768), row(768), colt(384),
                   row(384), row(128), colt(128)),
        compiler_params=_cparams(2),
        name="prep_qkv",
    )(x, mods, n1g, w_in, cqg, w_uq, ckvg, w_ukv, gq, gqs, gk, gks, swqg, swkg, cos_t, sin_t)


def _half_mask(q, half):
    lane = lax.broadcasted_iota(jnp.int32, q.shape, 1)
    keep = (lane < HALF) if half == 0 else (lane >= HALF)
    return jnp.where(keep, q, jnp.zeros_like(q))


def _store_pair(o_ref, g, out_lo, out_hi, row0=0):
    pair = jnp.concatenate([out_lo, out_hi], axis=0)
    o_ref[0, row0:row0 + pair.shape[1], g * SLOT:(g + 1) * SLOT] = pair.T.astype(o_ref.dtype)


SIGN_BIT = -2 ** 31
JOBS = (("a", 0), ("b", 0), ("b", 1), ("a", 1), ("b", 2), ("b", 3), ("a", 2), ("b", 4), ("b", 5), ("a", 3))


def _softplus2(z):
    neg_abs = lax.bitcast_convert_type(lax.bitcast_convert_type(z, jnp.int32) | SIGN_BIT, F32)
    return jnp.maximum(z, 0.0) + jnp.log(1.0 + jnp.exp2(neg_abs)) * LOG2E


def _hi_lo(x):
    hi = x.astype(BF16)
    lo = (x - hi.astype(F32)).astype(BF16)
    return jnp.concatenate([hi, lo], axis=0)


def _causal_kernel(sink_ref, aq_ref, ak_ref, avt_ref, tri_ref, bq_ref, bk_ref, bvt_ref,
                   cq_ref, ck_ref, cvt_ref, bias_ref, oa_ref, ob_ref, oc_ref):
    qi = pl.program_id(0)
    n_diag = TQ // TK
    a_heads, b_heads = range(SB_HEADS), range(MLA_HEADS)

    def put(full, part, c0):
        return part if c0 == 0 else jnp.concatenate([full[:, :c0], part], axis=1)

    def run_jobs(jobs):
        ca = [(jnp.zeros((HEAD_DIM, TQ), F32), jnp.zeros((1, TQ), F32)) for _ in a_heads]
        cb = [(jnp.full((1, TQ), NEG, F32), jnp.zeros((1, TQ), F32), jnp.zeros((MLA_V, TQ), F32))
              for _ in b_heads]
        st = [None] * len(jobs)
        out_c = {}

        def geometry(t):
            mixer, hd, blk, diag = jobs[t]
            c0 = 0 if diag is None else diag * TK
            row = lax.broadcasted_iota(jnp.int32, (TK, TQ - c0), 0)
            col = lax.broadcasted_iota(jnp.int32, (TK, TQ - c0), 1)
            return mixer, hd, blk * TK, diag is not None, c0, row, col

        def window(t):
            _, slot, sub, key0 = jobs[t]
            lo = max(key0, 0)
            head = slot // 2 + (slot % 2) * (SW_HEADS // SW_KV_HEADS)
            return slot, sub, head, lo, key0 + WINDOW + TQ_SW, lo - key0

        def scores(t):
            if jobs[t][0] == "c":
                slot, sub, head, lo, hi, skip = window(t)
                q = _half_mask(cq_ref[0, sub * TQ_SW:(sub + 1) * TQ_SW, (slot // 2) * SLOT:(slot // 2 + 1) * SLOT],
                               slot % 2)
                st[t] = _nt_dot(ck_ref[0, lo:hi, :], q) + bias_ref[head, skip:, :]
                return
            mixer, hd, start, masked, c0, row, col = geometry(t)
            if mixer == "a":
                g, half = divmod(hd, 2)
                q = _half_mask(aq_ref[0, c0:, g * SLOT:(g + 1) * SLOT], half)
                st[t] = _nt_dot(ak_ref[0, start:start + TK, g * SLOT:(g + 1) * SLOT], q)
            else:
                sc = _nt_dot(bk_ref[0, start:start + TK, hd * SLOT:(hd + 1) * SLOT],
                             bq_ref[0, c0:, hd * SLOT:(hd + 1) * SLOT])
                st[t] = jnp.where(row <= col, sc, NEG) if masked else sc

        def second(t):
            if jobs[t][0] == "c":
                slot, sub, head, lo, hi, skip = window(t)
                sink = sink_ref[head] * LOG2E
                m = jnp.maximum(jnp.max(st[t], axis=0, keepdims=True), sink)
                p = jnp.exp2(st[t] - m)
                denom = jnp.sum(p, axis=0, keepdims=True) + jnp.exp2(sink - m)
                pv = _dot(cvt_ref[0, (slot % 2) * HEAD_DIM:(slot % 2 + 1) * HEAD_DIM, lo:hi], p.astype(BF16))
                out_c[(sub, slot)] = pv * (1.0 / denom)
                st[t] = None
                return
            mixer, hd, start, masked, c0, row, col = geometry(t)
            if mixer == "a":
                drop = _softplus2(st[t])
                if masked:
                    drop = jnp.where(row < col, drop, 0.0)
                st[t] = (st[t], _dot(tri_ref[...], _hi_lo(drop)))
            else:
                m_all, l_all, acc_all = cb[hd]
                m_old = m_all[:, c0:]
                m_new = jnp.maximum(m_old, jnp.max(st[t], axis=0, keepdims=True))
                alpha = jnp.exp2(m_old - m_new)
                p = jnp.exp2(st[t] - m_new)
                l_new = alpha * l_all[:, c0:] + jnp.sum(p, axis=0, keepdims=True)
                pv = _dot(bvt_ref[0, hd * MLA_V:(hd + 1) * MLA_V, start:start + TK], p.astype(BF16))
                cb[hd] = (put(m_all, m_new, c0), put(l_all, l_new, c0),
                          put(acc_all, alpha * acc_all[:, c0:] + pv, c0))

        def third(t):
            if jobs[t][0] == "c":
                return
            mixer, hd, start, masked, c0, row, col = geometry(t)
            if mixer == "a":
                z, incl = st[t]
                acc, run = ca[hd]
                w = jnp.exp2(z + incl + run[:, c0:])
                if masked:
                    w = jnp.where(row < col, w, 0.0)
                pv = _dot(avt_ref[0, hd * HEAD_DIM:(hd + 1) * HEAD_DIM, start:start + TK], w.astype(BF16))
                ca[hd] = (put(acc, acc[:, c0:] + pv, c0), put(run, run[:, c0:] + incl[0:1, :], c0))
            st[t] = None

        n = len(jobs)
        for t in range(n + 2):
            if t < n:
                scores(t)
            if 0 <= t - 1 < n:
                second(t - 1)
            if 0 <= t - 2 < n:
                third(t - 2)
        for g in range(SB_HEADS // 2):
            _store_pair(oa_ref, g, ca[2 * g][0], ca[2 * g + 1][0])
        outs = [acc * (1.0 / l) for (_, l, acc) in cb]
        for g in range(MLA_HEADS // 2):
            _store_pair(ob_ref, g, outs[2 * g], outs[2 * g + 1])
        for sub in range(TQ // TQ_SW):
            for g in range(SW_HEADS // 2):
                _store_pair(oc_ref, g, out_c[(sub, 2 * g)], out_c[(sub, 2 * g + 1)], sub * TQ_SW)

    def jobs_for(qb):
        n_past = qb * n_diag
        past = [(j, None) for j in reversed(range(n_past))]
        a_blocks = [(n_past + d, d) for d in reversed(range(n_diag))] + past
        b_blocks = [(n_past + d, d) for d in range(n_diag)] + past
        sweep = [(mixer, hd) + (a_blocks[i] if mixer == "a" else b_blocks[i])
                 for i in range(len(a_blocks)) for (mixer, hd) in JOBS]
        local = [("c", slot, sub, qb * TQ + sub * TQ_SW - WINDOW)
                 for sub in range(TQ // TQ_SW) for slot in range(SW_HEADS)]
        jobs = []
        for job in sweep:
            jobs.append(job)
            if local:
                jobs.append(local.pop(0))
        return jobs

    for qb in range(ak_ref.shape[1] // TQ):
        pl.when(qi == qb)(functools.partial(run_jobs, jobs_for(qb)))


def _attention(sinks, aq, ak, avt, bq, bk, bvt, cq, ck, cvt, bias):
    b, s, wa = aq.shape
    wb, wbv, wc = bq.shape[2], bvt.shape[1], cq.shape[2]
    tri = -(jnp.arange(TK)[None, :] >= jnp.arange(TK)[:, None]).astype(BF16)
    tri = jnp.concatenate([tri, tri], axis=1)
    qblk = lambda w: pl.BlockSpec((1, TQ, w), lambda i, j: (j, i, 0))
    seq = lambda w: pl.BlockSpec((1, s, w), lambda i, j: (j, 0, 0))
    seqt = lambda w: pl.BlockSpec((1, w, s), lambda i, j: (j, 0, 0))
    const = lambda a: pl.BlockSpec(a.shape, lambda i, j: (0,) * a.ndim)
    out = lambda w: jax.ShapeDtypeStruct((b, s, w), BF16)
    return pl.pallas_call(
        _causal_kernel,
        out_shape=(out(wa), out(wbv), out(wc)),
        grid=(s // TQ, b),
        in_specs=[pl.BlockSpec(memory_space=pltpu.SMEM),
                  qblk(wa), seq(wa), seqt(wa), const(tri),
                  qblk(wb), seq(wb), seqt(wbv),
                  qblk(wc), seq(ck.shape[2]), seqt(cvt.shape[1]), const(bias)],
        out_specs=(qblk(wa), qblk(wbv), qblk(wc)),
        compiler_params=_cparams(2),
        name="attention",
    )(sinks, aq, ak, avt, tri, bq, bk, bvt, cq, ck, cvt, bias)


def _mlp_kernel(x_ref, mod_ref, oa_ref, ob_ref, oc_ref, wo_ref, n2g_ref, wup_ref, cw_ref, cb_ref,
                wdn_ref, out_ref, carry_ref):
    si = pl.program_id(1)
    tm = x_ref.shape[1]
    mod = mod_ref[0]
    gate1, shift2, scale2, gate2 = mod[2:3], mod[3:4], mod[4:5], mod[5:6]
    na, nb = oa_ref.shape[2], ob_ref.shape[2]
    att = (_dot(oa_ref[0], wo_ref[0:na, :]) + _dot(ob_ref[0], wo_ref[na:na + nb, :])
           + _dot(oc_ref[0], wo_ref[na + nb:, :]))
    x1 = x_ref[0] + gate1 * att
    h2 = (_rms(x1, D_MODEL) * n2g_ref[...] * (1.0 + scale2) + shift2).astype(BF16)

    @pl.when(si == 0)
    def _():
        carry_ref[...] = jnp.zeros_like(carry_ref)

    def up(c):
        lo = c * FF_CHUNK
        return (_dot(h2, wup_ref[:, lo:lo + FF_CHUNK]),
                _dot(h2, wup_ref[:, D_FF + lo:D_FF + lo + FF_CHUNK]))

    def conv(u, col):
        cols = slice(col, col + FF_CHUNK)
        prev = carry_ref[:, cols]
        carry_ref[:, cols] = u[tm - CARRY_ROWS:, :]
        ext = jnp.concatenate([prev, u], axis=0)
        u1 = ext[CARRY_ROWS - 1:CARRY_ROWS - 1 + tm, :]
        u2 = ext[CARRY_ROWS - 2:CARRY_ROWS - 2 + tm, :]
        cw = cw_ref[:, cols]
        return u * cw[2:3] + u1 * cw[1:2] + u2 * cw[0:1] + cb_ref[:, cols]

    n_chunks = D_FF // FF_CHUNK
    u_next = up(0)
    for c in range(n_chunks):
        u_gate, u_val = u_next
        if c + 1 < n_chunks:
            u_next = up(c + 1)
        gate = conv(u_gate, c * FF_CHUNK)
        val = conv(u_val, D_FF + c * FF_CHUNK)
        a = (gate * jax.nn.sigmoid(gate) * val).astype(BF16)
        part = _dot(a, wdn_ref[c * FF_CHUNK:(c + 1) * FF_CHUNK, :])
        acc = part if c == 0 else acc + part
    out_ref[0] = x1 + gate2 * acc


def _mlp(x, mods, oa, ob, oc, w_out, n2g, w_up, conv_w, conv_b, w_down):
    b, s, d = x.shape
    tm = TM_MLP
    row = lambda w: pl.BlockSpec((1, tm, w), lambda i, j: (i, j, 0))
    const = lambda a: pl.BlockSpec(a.shape, lambda i, j: (0,) * a.ndim, pipeline_mode=pl.Buffered(1))
    return pl.pallas_call(
        _mlp_kernel,
        out_shape=jax.ShapeDtypeStruct((b, s, d), F32),
        grid=(b, s // tm),
        in_specs=[row(d), pl.BlockSpec((1, 6, d), lambda i, j: (i, 0, 0)),
                  row(oa.shape[2]), row(ob.shape[2]), row(oc.shape[2]),
                  const(w_out), const(n2g), const(w_up), const(conv_w), const(conv_b), const(w_down)],
        out_specs=row(d),
        scratch_shapes=[pltpu.VMEM((CARRY_ROWS, 2 * D_FF), F32)],
        compiler_params=_cparams(2),
        name="outproj_mlp",
    )(x, mods, oa, ob, oc, w_out, n2g, w_up, conv_w, conv_b, w_down)


def _slot_gain(g, swap):
    z = jnp.zeros((SLOT - MLA_QK,), F32)
    lo, hi = g[ROPE_LO:ROPE_LO + ROPE_HALF], g[ROPE_LO + ROPE_HALF:MLA_QK]
    if swap:
        return jnp.concatenate([jnp.zeros((MLA_NOPE,), F32), hi, lo, z]).reshape(1, SLOT)
    return jnp.concatenate([g, z]).reshape(1, SLOT)


def _layout_w_in(w):
    d = w.shape[0]
    z = lambda n: jnp.zeros((d, n), w.dtype)
    kr = w[:, 1152:1184]
    swq = w[:, 1184:1568].reshape(d, SW_HEADS, HEAD_DIM)
    order = [0, 3, 1, 4, 2, 5]
    swq = swq[:, order, :].reshape(d, SW_HEADS * HEAD_DIM)
    cols = [w[:, 0:1152],
            z(MLA_NOPE), kr, z(SLOT - MLA_QK),
            z(MLA_NOPE), kr[:, ROPE_HALF:], kr[:, :ROPE_HALF], z(SLOT - MLA_QK),
            w[:, 1568:1696], swq, w[:, 1696:1824]]
    return jnp.concatenate(cols, axis=1).astype(BF16)


def _layout_w_uq(w):
    r = w.shape[0]
    w = w.reshape(r, MLA_HEADS, MLA_QK)
    nope, x1, x2 = w[..., :MLA_NOPE], w[..., MLA_NOPE:MLA_NOPE + ROPE_HALF], w[..., MLA_NOPE + ROPE_HALF:]
    z = jnp.zeros((r, MLA_HEADS, SLOT - MLA_QK), w.dtype)
    plain = jnp.concatenate([nope, x1, x2, z], axis=-1).reshape(r, MLA_HEADS * SLOT)
    swapped = jnp.concatenate([jnp.zeros_like(nope), x2, x1, z], axis=-1).reshape(r, MLA_HEADS * SLOT)
    return jnp.concatenate([plain, swapped], axis=1).astype(BF16)


def _layout_w_ukv(w):
    r = w.shape[0]
    w = w.reshape(r, MLA_HEADS, MLA_NOPE + MLA_V)
    k_nope, v = w[..., :MLA_NOPE], w[..., MLA_NOPE:]
    k_slots = jnp.concatenate([k_nope, jnp.zeros((r, MLA_HEADS, SLOT - MLA_NOPE), w.dtype)], axis=-1)
    return jnp.concatenate([k_slots.reshape(r, MLA_HEADS * SLOT), v.reshape(r, MLA_HEADS * MLA_V)],
                           axis=1).astype(BF16)


def _layout_w_out(w):
    n_ab = SB_HEADS * HEAD_DIM + MLA_HEADS * MLA_V
    sw = w[n_ab:].reshape(SW_HEADS, HEAD_DIM, w.shape[1])[jnp.array([0, 3, 1, 4, 2, 5])]
    return jnp.concatenate([w[:n_ab], sw.reshape(SW_HEADS * HEAD_DIM, w.shape[1])], axis=0).astype(BF16)


def kernel(x, c, positions, rel_table, norm1_g, norm2_g, w_ada, b_ada, w_in, mla_cq_g, w_uq, mla_ckv_g,
           w_ukv, mla_qn_g, mla_kn_g, sw_qn_g, sw_kn_g, sw_sinks, w_out, w_up, conv_w, conv_b, w_down):
    depth = w_in.shape[0]
    b = x.shape[0]
    mods = _mods(c, w_ada, b_ada).reshape(depth, b, 6, D_MODEL)
    cos_t, sin_t = _rope_tables(positions)
    bias = _window_bias(rel_table)
    row = lambda v: v.reshape(1, -1).astype(F32)
    two = lambda v: jnp.concatenate([v, v]).reshape(1, SLOT).astype(F32)
    for l in range(depth):
        qkv = _prep(x, mods[l], row(norm1_g[l]), _layout_w_in(w_in[l]), row(mla_cq_g[l]),
                    _layout_w_uq(w_uq[l]), row(mla_ckv_g[l]), _layout_w_ukv(w_ukv[l]),
                    _slot_gain(mla_qn_g[l], False), _slot_gain(mla_qn_g[l], True),
                    _slot_gain(mla_kn_g[l], False), _slot_gain(mla_kn_g[l], True),
                    two(sw_qn_g[l]), two(sw_kn_g[l]), cos_t, sin_t)
        sbq, sbk, sbvt, mq, mk, mvt, swq, swk, swvt = qkv
        o_a, o_b, o_c = _attention(sw_sinks[l], sbq, sbk, sbvt, mq, mk, mvt, swq, swk, swvt, bias)
        x = _mlp(x, mods[l], o_a, o_b, o_c, _layout_w_out(w_out[l]), row(norm2_g[l]),
                 w_up[l].astype(BF16), conv_w[l], row(conv_b[l]), w_down[l].astype(BF16))
    return x
```

```python
import functools
import math

import numpy as np
import jax
import jax.numpy as jnp
from jax import lax
from jax.experimental import pallas as pl
from jax.experimental.pallas import tpu as pltpu

F32 = jnp.float32
BF16 = jnp.bfloat16

D_MODEL = 1024
HEAD_DIM = 64
SB_HEADS = 4
MLA_HEADS = 6
MLA_Q_RANK = 256
MLA_KV_RANK = 128
MLA_NOPE = 64
MLA_ROPE = 32
MLA_V = 64
MLA_QK = MLA_NOPE + MLA_ROPE
ROPE_THETA = 10000.0
SW_HEADS = 6
SW_KV_HEADS = 2
WINDOW = 128
REL_BUCKETS = 32
REL_MAX_DIST = 128
D_FF = 2816
CONV_W = 3
EPS = 1e-6
NEG = -1e30

LANES = 128
SLOT = LANES
HALF = SLOT // 2

C_SBQ, C_SBK, C_SBV = 0, 256, 512
C_CQ = 768
C_CKV = 1024
C_KROPE = 1152
C_KROPE_SW = 1280
C_SWK = 1408
C_SWQ = 1536
C_SWV = 1920
N_IN = 2048

ROPE_LO = MLA_NOPE
ROPE_HALF = MLA_ROPE // 2

TM_PREP = 512
TM_MLP = 512
TQ = 512
TK = 256
TQ_SW = 256
LOG2E = math.log2(math.e)
FF_CHUNK = 256
CARRY_ROWS = 8

VMEM_LIMIT = 56 * 1024 * 1024


def _cparams(n_axes):
    return pltpu.CompilerParams(dimension_semantics=("arbitrary",) * n_axes,
                                vmem_limit_bytes=VMEM_LIMIT)


def _rms(x, n):
    return x * lax.rsqrt(jnp.sum(x * x, axis=-1, keepdims=True) * (1.0 / n) + EPS)


def _nt_dot(a, b):
    return lax.dot_general(a, b, (((1,), (1,)), ((), ())), preferred_element_type=F32)


def _dot(a, b):
    return jnp.dot(a, b, preferred_element_type=F32)


def _mods_kernel(c_ref, w_ref, b_ref, o_ref):
    c = c_ref[...]
    a = (c * jax.nn.sigmoid(c)).astype(BF16)
    o_ref[0] = _dot(a, w_ref[0].astype(BF16)) + b_ref[0]


def _mods(c, w_ada, b_ada):
    depth, d, n = w_ada.shape
    b = c.shape[0]
    tn = 1536
    return pl.pallas_call(
        _mods_kernel,
        out_shape=jax.ShapeDtypeStruct((depth, b, n), F32),
        grid=(depth, n // tn),
        in_specs=[pl.BlockSpec((b, d), lambda l, j: (0, 0)),
                  pl.BlockSpec((1, d, tn), lambda l, j: (l, 0, j)),
                  pl.BlockSpec((1, 1, tn), lambda l, j: (l, 0, j))],
        out_specs=pl.BlockSpec((1, b, tn), lambda l, j: (l, 0, j)),
        compiler_params=_cparams(2),
        name="adaln_mods",
    )(c, w_ada, b_ada.reshape(depth, 1, n))


def _rope_kernel(pos_ref, invf_ref, cos_ref, sin_ref):
    pos = pos_ref[0].astype(F32)
    ang = invf_ref[...] * pos
    c = jnp.cos(ang)
    s = jnp.sin(ang)
    tm = pos.shape[1]
    ones = jnp.ones((ROPE_LO, tm), F32)
    zeros = jnp.zeros((ROPE_LO, tm), F32)
    pad = SLOT - ROPE_LO - MLA_ROPE
    cos_t = jnp.concatenate([ones, c, c, jnp.ones((pad, tm), F32)], axis=0)
    sin_t = jnp.concatenate([zeros, -s, s, jnp.zeros((pad, tm), F32)], axis=0)
    cos_ref[0] = cos_t.T
    sin_ref[0] = sin_t.T


def _rope_tables(positions):
    b, s = positions.shape
    tm = s
    half = ROPE_HALF
    inv_freq = jnp.power(ROPE_THETA, -jnp.arange(half, dtype=F32) / half).reshape(half, 1)
    out = jax.ShapeDtypeStruct((b, s, SLOT), F32)
    return pl.pallas_call(
        _rope_kernel,
        out_shape=(out, out),
        grid=(b, s // tm),
        in_specs=[pl.BlockSpec((1, 1, tm), lambda i, j: (i, 0, j)),
                  pl.BlockSpec((half, 1), lambda i, j: (0, 0))],
        out_specs=(pl.BlockSpec((1, tm, SLOT), lambda i, j: (i, j, 0)),
                   pl.BlockSpec((1, tm, SLOT), lambda i, j: (i, j, 0))),
        compiler_params=_cparams(2),
        name="rope_tables",
    )(positions.reshape(b, 1, s), inv_freq)


def _t5_bucket(dist):
    max_exact = REL_BUCKETS // 2
    n = jnp.maximum(dist, 0)
    nf = jnp.maximum(n, 1).astype(F32)
    large = max_exact + (jnp.log(nf / max_exact) / math.log(REL_MAX_DIST / max_exact)
                         * (REL_BUCKETS - max_exact)).astype(jnp.int32)
    large = jnp.minimum(large, REL_BUCKETS - 1)
    return jnp.where(n < max_exact, n, large)


def _bias_kernel(tab_ref, bucket_ref, o_ref):
    bucket = bucket_ref[...]
    for h in range(SW_HEADS):
        acc = jnp.zeros(bucket.shape, F32)
        for bkt in range(REL_BUCKETS):
            acc = jnp.where(bucket == bkt, tab_ref[bkt, h], acc)
        o_ref[h] = jnp.where(bucket >= 0, acc * LOG2E, NEG)


def _window_bias(rel_table):
    nk = WINDOW + TQ_SW
    key = jnp.arange(nk)[:, None]
    qry = jnp.arange(TQ_SW)[None, :]
    dist = qry + WINDOW - key
    valid = (dist >= 0) & (dist < WINDOW)
    bucket = jnp.where(valid, _t5_bucket(dist), -1).astype(jnp.int32)
    return pl.pallas_call(
        _bias_kernel,
        out_shape=jax.ShapeDtypeStruct((SW_HEADS, nk, TQ_SW), F32),
        in_specs=[pl.BlockSpec(memory_space=pltpu.SMEM),
                  pl.BlockSpec(memory_space=pltpu.VMEM)],
        out_specs=pl.BlockSpec(memory_space=pltpu.VMEM),
        name="window_bias",
    )(rel_table, bucket)


def _half_rms(x, gain, scale):
    lo = lax.broadcasted_iota(jnp.int32, x.shape, 1) < HALF
    sq = x * x
    s_lo = jnp.sum(jnp.where(lo, sq, 0.0), axis=-1, keepdims=True)
    s_hi = jnp.sum(jnp.where(lo, 0.0, sq), axis=-1, keepdims=True)
    r = jnp.where(lo, lax.rsqrt(s_lo * (1.0 / HEAD_DIM) + EPS), lax.rsqrt(s_hi * (1.0 / HEAD_DIM) + EPS))
    return x * r * (gain * scale)


def _prep_kernel(x_ref, mod_ref, n1g_ref, win_ref, cqg_ref, wuq_ref, ckvg_ref, wukv_ref,
                 gq_ref, gqs_ref, gk_ref, gks_ref, swqg_ref, swkg_ref, cos_ref, sin_ref,
                 sbq_ref, sbk_ref, sbvt_ref, mq_ref, mk_ref, mvt_ref, swq_ref, swk_ref, swvt_ref):
    x = x_ref[0]
    mod = mod_ref[0]
    shift1, scale1 = mod[0:1], mod[1:2]
    h = (_rms(x, D_MODEL) * n1g_ref[...] * (1.0 + scale1) + shift1).astype(BF16)

    proj_mla = _dot(h, win_ref[:, C_CQ:C_SWQ])
    proj_sb = _dot(h, win_ref[:, C_SBQ:C_CQ])
    cq = proj_mla[:, 0:MLA_Q_RANK]
    ckv = proj_mla[:, C_CKV - C_CQ:C_CKV - C_CQ + MLA_KV_RANK]
    krope = proj_mla[:, C_KROPE - C_CQ:C_KROPE - C_CQ + SLOT]
    krope_sw = proj_mla[:, C_KROPE_SW - C_CQ:C_KROPE_SW - C_CQ + SLOT]
    cqn = (_rms(cq, MLA_Q_RANK) * cqg_ref[...]).astype(BF16)
    ckvn = (_rms(ckv, MLA_KV_RANK) * ckvg_ref[...]).astype(BF16)
    qraw = _dot(cqn, wuq_ref[...])
    kv = _dot(ckvn, wukv_ref[...])
    proj_sw = _dot(h, win_ref[:, C_SWQ:N_IN])

    sbq_ref[0] = (proj_sb[:, C_SBQ:C_SBQ + 256] * (HEAD_DIM ** -0.5 * LOG2E)).astype(BF16)
    sbk_ref[0] = proj_sb[:, C_SBK:C_SBK + 256].astype(BF16)
    sbvt_ref[0] = proj_sb[:, C_SBV:C_SBV + 256].T.astype(BF16)

    cos = cos_ref[0]
    sin = sin_ref[0]

    nq = MLA_HEADS * SLOT
    q_scale = MLA_QK ** -0.5 * LOG2E
    q_cos, q_sin = gq_ref[...] * cos, gqs_ref[...] * sin
    for hd in range(MLA_HEADS):
        slot = qraw[:, hd * SLOT:(hd + 1) * SLOT]
        swapped = qraw[:, nq + hd * SLOT:nq + (hd + 1) * SLOT]
        r = lax.rsqrt(jnp.sum(slot * slot, axis=-1, keepdims=True) * (1.0 / MLA_QK) + EPS) * q_scale
        mq_ref[0, :, hd * SLOT:(hd + 1) * SLOT] = ((slot * q_cos + swapped * q_sin) * r).astype(BF16)

    mvt_ref[0] = kv[:, nq:nq + MLA_HEADS * MLA_V].T.astype(BF16)
    k_cos = gk_ref[...] * cos
    k_rot = krope_sw * (gks_ref[...] * sin)
    for hd in range(MLA_HEADS):
        slot = kv[:, hd * SLOT:(hd + 1) * SLOT] + krope
        r = lax.rsqrt(jnp.sum(slot * slot, axis=-1, keepdims=True) * (1.0 / MLA_QK) + EPS)
        mk_ref[0, :, hd * SLOT:(hd + 1) * SLOT] = ((slot * k_cos + k_rot) * r).astype(BF16)

    for g in range(SW_HEADS // 2):
        xq = proj_sw[:, g * SLOT:(g + 1) * SLOT]
        swq_ref[0, :, g * SLOT:(g + 1) * SLOT] = _half_rms(
            xq, swqg_ref[...], HEAD_DIM ** -0.5 * LOG2E).astype(BF16)
    swk_ref[0] = _half_rms(proj_mla[:, C_SWK - C_CQ:C_SWK - C_CQ + SLOT], swkg_ref[...], 1.0).astype(BF16)
    swvt_ref[0] = proj_sw[:, C_SWV - C_SWQ:C_SWV - C_SWQ + SLOT].T.astype(BF16)


def _prep(x, mods, n1g, w_in, cqg, w_uq, ckvg, w_ukv, gq, gqs, gk, gks, swqg, swkg, cos_t, sin_t):
    b, s, d = x.shape
    tm = TM_PREP
    row = lambda w: pl.BlockSpec((1, tm, w), lambda i, j: (i, j, 0))
    colt = lambda w: pl.BlockSpec((1, w, tm), lambda i, j: (i, 0, j))
    full = lambda a: pl.BlockSpec(a.shape, lambda i, j: (0,) * a.ndim)
    act = lambda w: jax.ShapeDtypeStruct((b, s, w), BF16)
    actt = lambda w: jax.ShapeDtypeStruct((b, w, s), BF16)
    return pl.pallas_call(
        _prep_kernel,
        out_shape=(act(256), act(256), actt(256), act(768), act(768), actt(384),
                   act(384), act(128), actt(128)),
        grid=(b, s // tm),
        in_specs=[row(d), pl.BlockSpec((1, 6, d), lambda i, j: (i, 0, 0)), full(n1g), full(w_in),
                  full(cqg), full(w_uq), full(ckvg), full(w_ukv), full(gq), full(gqs), full(gk),
                  full(gks), full(swqg), full(swkg), row(SLOT), row(SLOT)],
        out_specs=(row(256), row(256), colt(256), row(768), row(768), colt(384),
                   row(384), row(128), colt(128)),
        compiler_params=_cparams(2),
        name="prep_qkv",
    )(x, mods, n1g, w_in, cqg, w_uq, ckvg, w_ukv, gq, gqs, gk, gks, swqg, swkg, cos_t, sin_t)


def _half_mask(q, half):
    lane = lax.broadcasted_iota(jnp.int32, q.shape, 1)
    keep = (lane < HALF) if half == 0 else (lane >= HALF)
    return jnp.where(keep, q, jnp.zeros_like(q))


def _store_pair(o_ref, g, out_lo, out_hi, row0=0):
    pair = jnp.concatenate([out_lo, out_hi], axis=0)
    o_ref[0, row0:row0 + pair.shape[1], g * SLOT:(g + 1) * SLOT] = pair.T.astype(o_ref.dtype)


SIGN_BIT = -2 ** 31
JOBS = (("a", 0), ("b", 0), ("b", 1), ("a", 1), ("b", 2), ("b", 3), ("a", 2), ("b", 4), ("b", 5), ("a", 3))


def _softplus2(z):
    neg_abs = lax.bitcast_convert_type(lax.bitcast_convert_type(z, jnp.int32) | SIGN_BIT, F32)
    return jnp.maximum(z, 0.0) + jnp.log(1.0 + jnp.exp2(neg_abs)) * LOG2E


def _hi_lo(x):
    hi = x.astype(BF16)
    lo = (x - hi.astype(F32)).astype(BF16)
    return jnp.concatenate([hi, lo], axis=0)


def _causal_kernel(qb, sink_ref, aq_ref, ak_ref, avt_ref, tri_ref, bq_ref, bk_ref, bvt_ref,
                   cq_ref, ck_ref, cvt_ref, bias_ref, pa_ref, pb_ref, pc_ref, oa_ref, ob_ref, oc_ref):
    del pa_ref, pb_ref, pc_ref
    n_diag = TQ // TK
    a_heads, b_heads = range(SB_HEADS), range(MLA_HEADS)

    def put(full, part, c0):
        return part if c0 == 0 else jnp.concatenate([full[:, :c0], part], axis=1)

    def run_jobs(carry, jobs):
        ca, cb = list(carry[0]), list(carry[1])
        st = [None] * len(jobs)
        out_c = {}

        def geometry(t):
            mixer, hd, start, diag = jobs[t]
            c0 = 0 if diag is None else diag * TK
            row = lax.broadcasted_iota(jnp.int32, (TK, TQ - c0), 0)
            col = lax.broadcasted_iota(jnp.int32, (TK, TQ - c0), 1)
            return mixer, hd, start, diag is not None, c0, row, col

        def window(t):
            _, slot, sub = jobs[t]
            head = slot // 2 + (slot % 2) * (SW_HEADS // SW_KV_HEADS)
            key0 = qb * TQ + sub * TQ_SW - WINDOW
            lo = max(key0, 0)
            return slot, sub, head, lo, key0 + WINDOW + TQ_SW, lo - key0

        def scores(t):
            if jobs[t][0] == "c":
                slot, sub, head, lo, hi, skip = window(t)
                q = _half_mask(cq_ref[0, sub * TQ_SW:(sub + 1) * TQ_SW, (slot // 2) * SLOT:(slot // 2 + 1) * SLOT],
                               slot % 2)
                st[t] = _nt_dot(ck_ref[0, lo:hi, :], q) + bias_ref[head, skip:, :]
                return
            mixer, hd, start, masked, c0, row, col = geometry(t)
            if mixer == "a":
                g, half = divmod(hd, 2)
                q = _half_mask(aq_ref[0, c0:, g * SLOT:(g + 1) * SLOT], half)
                st[t] = _nt_dot(ak_ref[0, pl.ds(start, TK), g * SLOT:(g + 1) * SLOT], q)
            else:
                sc = _nt_dot(bk_ref[0, pl.ds(start, TK), hd * SLOT:(hd + 1) * SLOT],
                             bq_ref[0, c0:, hd * SLOT:(hd + 1) * SLOT])
                st[t] = jnp.where(row <= col, sc, NEG) if masked else sc

        def second(t):
            if jobs[t][0] == "c":
                slot, sub, head, lo, hi, skip = window(t)
                sink = sink_ref[head] * LOG2E
                m = jnp.maximum(jnp.max(st[t], axis=0, keepdims=True), sink)
                p = jnp.exp2(st[t] - m)
                denom = jnp.sum(p, axis=0, keepdims=True) + jnp.exp2(sink - m)
                vt = cvt_ref[0, (slot % 2) * HEAD_DIM:(slot % 2 + 1) * HEAD_DIM, lo:hi]
                out_c[(sub, slot)] = _dot(vt, p.astype(BF16)) * (1.0 / denom)
                st[t] = None
                return
            mixer, hd, start, masked, c0, row, col = geometry(t)
            if mixer == "a":
                drop = _softplus2(st[t])
                if masked:
                    drop = jnp.where(row < col, drop, 0.0)
                st[t] = (st[t], _dot(tri_ref[...], _hi_lo(drop)))
            else:
                m_all, l_all, acc_all = cb[hd]
                m_old = m_all[:, c0:]
                m_new = jnp.maximum(m_old, jnp.max(st[t], axis=0, keepdims=True))
                alpha = jnp.exp2(m_old - m_new)
                p = jnp.exp2(st[t] - m_new)
                l_new = alpha * l_all[:, c0:] + jnp.sum(p, axis=0, keepdims=True)
                pv = _dot(bvt_ref[0, hd * MLA_V:(hd + 1) * MLA_V, pl.ds(start, TK)], p.astype(BF16))
                cb[hd] = (put(m_all, m_new, c0), put(l_all, l_new, c0),
                          put(acc_all, alpha * acc_all[:, c0:] + pv, c0))

        def third(t):
            if jobs[t][0] == "c":
                return
            mixer, hd, start, masked, c0, row, col = geometry(t)
            if mixer == "a":
                z, incl = st[t]
                acc, run = ca[hd]
                w = jnp.exp2(z + incl + run[:, c0:])
                if masked:
                    w = jnp.where(row < col, w, 0.0)
                pv = _dot(avt_ref[0, hd * HEAD_DIM:(hd + 1) * HEAD_DIM, pl.ds(start, TK)], w.astype(BF16))
                ca[hd] = (put(acc, acc[:, c0:] + pv, c0), put(run, run[:, c0:] + incl[0:1, :], c0))
            st[t] = None

        n = len(jobs)
        for t in range(n + 2):
            if t < n:
                scores(t)
            if 0 <= t - 1 < n:
                second(t - 1)
            if 0 <= t - 2 < n:
                third(t - 2)
        return (tuple(ca), tuple(cb)), out_c

    carry = (tuple((jnp.zeros((HEAD_DIM, TQ), F32), jnp.zeros((1, TQ), F32)) for _ in a_heads),
             tuple((jnp.full((1, TQ), NEG, F32), jnp.zeros((1, TQ), F32), jnp.zeros((MLA_V, TQ), F32))
                   for _ in b_heads))
    n_past = qb * n_diag
    past = [(j * TK, None) for j in reversed(range(n_past))]
    a_blocks = [((n_past + d) * TK, d) for d in reversed(range(n_diag))] + past
    b_blocks = [((n_past + d) * TK, d) for d in range(n_diag)] + past
    sweep = [(mixer, hd) + (a_blocks[i] if mixer == "a" else b_blocks[i])
             for i in range(len(a_blocks)) for (mixer, hd) in JOBS]
    local = [("c", slot, sub) for sub in range(TQ // TQ_SW) for slot in range(SW_HEADS)]
    jobs = []
    for job in sweep:
        jobs.append(job)
        if local:
            jobs.append(local.pop(0))
    (ca, cb), out_c = run_jobs(carry, jobs)
    for sub in range(TQ // TQ_SW):
        for g in range(SW_HEADS // 2):
            _store_pair(oc_ref, g, out_c[(sub, 2 * g)], out_c[(sub, 2 * g + 1)], sub * TQ_SW)
    for g in range(SB_HEADS // 2):
        _store_pair(oa_ref, g, ca[2 * g][0], ca[2 * g + 1][0])
    outs = [acc * (1.0 / l) for (_, l, acc) in cb]
    for g in range(MLA_HEADS // 2):
        _store_pair(ob_ref, g, outs[2 * g], outs[2 * g + 1])


def _attention(sinks, aq, ak, avt, bq, bk, bvt, cq, ck, cvt, bias):
    b, s, wa = aq.shape
    wb, wbv, wc = bq.shape[2], bvt.shape[1], cq.shape[2]
    tri = -(jnp.arange(TK)[None, :] >= jnp.arange(TK)[:, None]).astype(BF16)
    tri = jnp.concatenate([tri, tri], axis=1)
    const = lambda a: pl.BlockSpec(a.shape, lambda i: (0,) * a.ndim)
    outs = tuple(jnp.zeros((b, s, w), BF16) for w in (wa, wbv, wc))
    n_in = 12
    for qb in range(s // TQ):
        nk = (qb + 1) * TQ
        qblk = lambda w, qb=qb: pl.BlockSpec((1, TQ, w), lambda i: (i, qb, 0))
        keys = lambda w, nk=nk: pl.BlockSpec((1, nk, w), lambda i: (i, 0, 0))
        keyst = lambda w, nk=nk: pl.BlockSpec((1, w, nk), lambda i: (i, 0, 0))
        outs = pl.pallas_call(
            functools.partial(_causal_kernel, qb),
            out_shape=tuple(jax.ShapeDtypeStruct(o.shape, o.dtype) for o in outs),
            grid=(b,),
            in_specs=[pl.BlockSpec(memory_space=pltpu.SMEM),
                      qblk(wa), keys(wa), keyst(wa), const(tri),
                      qblk(wb), keys(wb), keyst(wbv),
                      qblk(wc), keys(ck.shape[2]), keyst(cvt.shape[1]), const(bias)]
                     + [pl.BlockSpec(memory_space=pl.ANY)] * 3,
            out_specs=(qblk(wa), qblk(wbv), qblk(wc)),
            input_output_aliases={n_in + k: k for k in range(3)},
            compiler_params=_cparams(1),
            name=f"attention_q{qb}",
        )(sinks, aq, ak, avt, tri, bq, bk, bvt, cq, ck, cvt, bias, *outs)
    return outs


def _mlp_kernel(x_ref, mod_ref, oa_ref, ob_ref, oc_ref, wo_ref, n2g_ref, wup_ref, cw_ref, cb_ref,
                wdn_ref, out_ref, carry_ref):
    si = pl.program_id(1)
    tm = x_ref.shape[1]
    mod = mod_ref[0]
    gate1, shift2, scale2, gate2 = mod[2:3], mod[3:4], mod[4:5], mod[5:6]
    na, nb = oa_ref.shape[2], ob_ref.shape[2]
    att = (_dot(oa_ref[0], wo_ref[0:na, :]) + _dot(ob_ref[0], wo_ref[na:na + nb, :])
           + _dot(oc_ref[0], wo_ref[na + nb:, :]))
    x1 = x_ref[0] + gate1 * att
    h2 = (_rms(x1, D_MODEL) * n2g_ref[...] * (1.0 + scale2) + shift2).astype(BF16)

    @pl.when(si == 0)
    def _():
        carry_ref[...] = jnp.zeros_like(carry_ref)

    def up(c):
        lo = c * FF_CHUNK
        return (_dot(h2, wup_ref[:, lo:lo + FF_CHUNK]),
                _dot(h2, wup_ref[:, D_FF + lo:D_FF + lo + FF_CHUNK]))

    def conv(u, col):
        cols = slice(col, col + FF_CHUNK)
        prev = carry_ref[:, cols]
        carry_ref[:, cols] = u[tm - CARRY_ROWS:, :]
        ext = jnp.concatenate([prev, u], axis=0)
        u1 = ext[CARRY_ROWS - 1:CARRY_ROWS - 1 + tm, :]
        u2 = ext[CARRY_ROWS - 2:CARRY_ROWS - 2 + tm, :]
        cw = cw_ref[:, cols]
        return u * cw[2:3] + u1 * cw[1:2] + u2 * cw[0:1] + cb_ref[:, cols]

    n_chunks = D_FF // FF_CHUNK
    u_next = up(0)
    for c in range(n_chunks):
        u_gate, u_val = u_next
        if c + 1 < n_chunks:
            u_next = up(c + 1)
        gate = conv(u_gate, c * FF_CHUNK)
        val = conv(u_val, D_FF + c * FF_CHUNK)
        a = (gate * jax.nn.sigmoid(gate) * val).astype(BF16)
        part = _dot(a, wdn_ref[c * FF_CHUNK:(c + 1) * FF_CHUNK, :])
        acc = part if c == 0 else acc + part
    out_ref[0] = x1 + gate2 * acc


def _mlp(x, mods, oa, ob, oc, w_out, n2g, w_up, conv_w, conv_b, w_down):
    b, s, d = x.shape
    tm = TM_MLP
    row = lambda w: pl.BlockSpec((1, tm, w), lambda i, j: (i, j, 0))
    const = lambda a: pl.BlockSpec(a.shape, lambda i, j: (0,) * a.ndim, pipeline_mode=pl.Buffered(1))
    return pl.pallas_call(
        _mlp_kernel,
        out_shape=jax.ShapeDtypeStruct((b, s, d), F32),
        grid=(b, s // tm),
        in_specs=[row(d), pl.BlockSpec((1, 6, d), lambda i, j: (i, 0, 0)),
                  row(oa.shape[2]), row(ob.shape[2]), row(oc.shape[2]),
                  const(w_out), const(n2g), const(w_up), const(conv_w), const(conv_b), const(w_down)],
        out_specs=row(d),
        scratch_shapes=[pltpu.VMEM((CARRY_ROWS, 2 * D_FF), F32)],
        compiler_params=_cparams(2),
        name="outproj_mlp",
    )(x, mods, oa, ob, oc, w_out, n2g, w_up, conv_w, conv_b, w_down)


def _slot_gain(g, swap):
    z = jnp.zeros((SLOT - MLA_QK,), F32)
    lo, hi = g[ROPE_LO:ROPE_LO + ROPE_HALF], g[ROPE_LO + ROPE_HALF:MLA_QK]
    if swap:
        return jnp.concatenate([jnp.zeros((MLA_NOPE,), F32), hi, lo, z]).reshape(1, SLOT)
    return jnp.concatenate([g, z]).reshape(1, SLOT)


def _layout_w_in(w):
    d = w.shape[0]
    z = lambda n: jnp.zeros((d, n), w.dtype)
    kr = w[:, 1152:1184]
    swq = w[:, 1184:1568].reshape(d, SW_HEADS, HEAD_DIM)
    order = [0, 3, 1, 4, 2, 5]
    swq = swq[:, order, :].reshape(d, SW_HEADS * HEAD_DIM)
    cols = [w[:, 0:1152],
            z(MLA_NOPE), kr, z(SLOT - MLA_QK),
            z(MLA_NOPE), kr[:, ROPE_HALF:], kr[:, :ROPE_HALF], z(SLOT - MLA_QK),
            w[:, 1568:1696], swq, w[:, 1696:1824]]
    return jnp.concatenate(cols, axis=1).astype(BF16)


def _layout_w_uq(w):
    r = w.shape[0]
    w = w.reshape(r, MLA_HEADS, MLA_QK)
    nope, x1, x2 = w[..., :MLA_NOPE], w[..., MLA_NOPE:MLA_NOPE + ROPE_HALF], w[..., MLA_NOPE + ROPE_HALF:]
    z = jnp.zeros((r, MLA_HEADS, SLOT - MLA_QK), w.dtype)
    plain = jnp.concatenate([nope, x1, x2, z], axis=-1).reshape(r, MLA_HEADS * SLOT)
    swapped = jnp.concatenate([jnp.zeros_like(nope), x2, x1, z], axis=-1).reshape(r, MLA_HEADS * SLOT)
    return jnp.concatenate([plain, swapped], axis=1).astype(BF16)


def _layout_w_ukv(w):
    r = w.shape[0]
    w = w.reshape(r, MLA_HEADS, MLA_NOPE + MLA_V)
    k_nope, v = w[..., :MLA_NOPE], w[..., MLA_NOPE:]
    k_slots = jnp.concatenate([k_nope, jnp.zeros((r, MLA_HEADS, SLOT - MLA_NOPE), w.dtype)], axis=-1)
    return jnp.concatenate([k_slots.reshape(r, MLA_HEADS * SLOT), v.reshape(r, MLA_HEADS * MLA_V)],
                           axis=1).astype(BF16)


def _layout_w_out(w):
    n_ab = SB_HEADS * HEAD_DIM + MLA_HEADS * MLA_V
    sw = w[n_ab:].reshape(SW_HEADS, HEAD_DIM, w.shape[1])[jnp.array([0, 3, 1, 4, 2, 5])]
    return jnp.concatenate([w[:n_ab], sw.reshape(SW_HEADS * HEAD_DIM, w.shape[1])], axis=0).astype(BF16)


def kernel(x, c, positions, rel_table, norm1_g, norm2_g, w_ada, b_ada, w_in, mla_cq_g, w_uq, mla_ckv_g,
           w_ukv, mla_qn_g, mla_kn_g, sw_qn_g, sw_kn_g, sw_sinks, w_out, w_up, conv_w, conv_b, w_down):
    depth = w_in.shape[0]
    b = x.shape[0]
    mods = _mods(c, w_ada, b_ada).reshape(depth, b, 6, D_MODEL)
    cos_t, sin_t = _rope_tables(positions)
    bias = _window_bias(rel_table)
    row = lambda v: v.reshape(1, -1).astype(F32)
    two = lambda v: jnp.concatenate([v, v]).reshape(1, SLOT).astype(F32)
    for l in range(depth):
        qkv = _prep(x, mods[l], row(norm1_g[l]), _layout_w_in(w_in[l]), row(mla_cq_g[l]),
                    _layout_w_uq(w_uq[l]), row(mla_ckv_g[l]), _layout_w_ukv(w_ukv[l]),
                    _slot_gain(mla_qn_g[l], False), _slot_gain(mla_qn_g[l], True),
                    _slot_gain(mla_kn_g[l], False), _slot_gain(mla_kn_g[l], True),
                    two(sw_qn_g[l]), two(sw_kn_g[l]), cos_t, sin_t)
        sbq, sbk, sbvt, mq, mk, mvt, swq, swk, swvt = qkv
        o_a, o_b, o_c = _attention(sw_sinks[l], sbq, sbk, sbvt, mq, mk, mvt, swq, swk, swvt, bias)
        x = _mlp(x, mods[l], o_a, o_b, o_c, _layout_w_out(w_out[l]), row(norm2_g[l]),
                 w_up[l].astype(BF16), conv_w[l], row(conv_b[l]), w_down[l].astype(BF16))
    return x
```

```python
import functools
import math

import numpy as np
import jax
import jax.numpy as jnp
from jax import lax
from jax.experimental import pallas as pl
from jax.experimental.pallas import tpu as pltpu

F32 = jnp.float32
BF16 = jnp.bfloat16

D_MODEL = 1024
HEAD_DIM = 64
SB_HEADS = 4
MLA_HEADS = 6
MLA_Q_RANK = 256
MLA_KV_RANK = 128
MLA_NOPE = 64
MLA_ROPE = 32
MLA_V = 64
MLA_QK = MLA_NOPE + MLA_ROPE
ROPE_THETA = 10000.0
SW_HEADS = 6
SW_KV_HEADS = 2
WINDOW = 128
REL_BUCKETS = 32
REL_MAX_DIST = 128
D_FF = 2816
CONV_W = 3
EPS = 1e-6
NEG = -1e30

LANES = 128
SLOT = LANES
HALF = SLOT // 2

C_SBQ, C_SBK, C_SBV = 0, 256, 512
C_CQ = 768
C_CKV = 1024
C_KROPE = 1152
C_KROPE_SW = 1280
C_SWK = 1408
C_SWQ = 1536
C_SWV = 1920
N_IN = 2048

ROPE_LO = MLA_NOPE
ROPE_HALF = MLA_ROPE // 2

TM_PREP = 512
TM_MLP = 512
TQ = 512
TK = 256
TQ_SW = 256
LOG2E = math.log2(math.e)
FF_CHUNK = 256
CARRY_ROWS = 8

VMEM_LIMIT = 56 * 1024 * 1024


def _cparams(n_axes):
    return pltpu.CompilerParams(dimension_semantics=("arbitrary",) * n_axes,
                                vmem_limit_bytes=VMEM_LIMIT)


def _rms(x, n):
    return x * lax.rsqrt(jnp.sum(x * x, axis=-1, keepdims=True) * (1.0 / n) + EPS)


def _nt_dot(a, b):
    return lax.dot_general(a, b, (((1,), (1,)), ((), ())), preferred_element_type=F32)


def _dot(a, b):
    return jnp.dot(a, b, preferred_element_type=F32)


def _mods_kernel(c_ref, w_ref, b_ref, o_ref):
    c = c_ref[...]
    a = (c * jax.nn.sigmoid(c)).astype(BF16)
    o_ref[0] = _dot(a, w_ref[0].astype(BF16)) + b_ref[0]


def _mods(c, w_ada, b_ada):
    depth, d, n = w_ada.shape
    b = c.shape[0]
    tn = 1536
    return pl.pallas_call(
        _mods_kernel,
        out_shape=jax.ShapeDtypeStruct((depth, b, n), F32),
        grid=(depth, n // tn),
        in_specs=[pl.BlockSpec((b, d), lambda l, j: (0, 0)),
                  pl.BlockSpec((1, d, tn), lambda l, j: (l, 0, j)),
                  pl.BlockSpec((1, 1, tn), lambda l, j: (l, 0, j))],
        out_specs=pl.BlockSpec((1, b, tn), lambda l, j: (l, 0, j)),
        compiler_params=_cparams(2),
        name="adaln_mods",
    )(c, w_ada, b_ada.reshape(depth, 1, n))


def _rope_kernel(pos_ref, invf_ref, cos_ref, sin_ref):
    pos = pos_ref[0].astype(F32)
    ang = invf_ref[...] * pos
    c = jnp.cos(ang)
    s = jnp.sin(ang)
    tm = pos.shape[1]
    ones = jnp.ones((ROPE_LO, tm), F32)
    zeros = jnp.zeros((ROPE_LO, tm), F32)
    pad = SLOT - ROPE_LO - MLA_ROPE
    cos_t = jnp.concatenate([ones, c, c, jnp.ones((pad, tm), F32)], axis=0)
    sin_t = jnp.concatenate([zeros, -s, s, jnp.zeros((pad, tm), F32)], axis=0)
    cos_ref[0] = cos_t.T
    sin_ref[0] = sin_t.T


def _rope_tables(positions):
    b, s = positions.shape
    tm = s
    half = ROPE_HALF
    inv_freq = jnp.power(ROPE_THETA, -jnp.arange(half, dtype=F32) / half).reshape(half, 1)
    out = jax.ShapeDtypeStruct((b, s, SLOT), F32)
    return pl.pallas_call(
        _rope_kernel,
        out_shape=(out, out),
        grid=(b, s // tm),
        in_specs=[pl.BlockSpec((1, 1, tm), lambda i, j: (i, 0, j)),
                  pl.BlockSpec((half, 1), lambda i, j: (0, 0))],
        out_specs=(pl.BlockSpec((1, tm, SLOT), lambda i, j: (i, j, 0)),
                   pl.BlockSpec((1, tm, SLOT), lambda i, j: (i, j, 0))),
        compiler_params=_cparams(2),
        name="rope_tables",
    )(positions.reshape(b, 1, s), inv_freq)


def _t5_bucket(dist):
    max_exact = REL_BUCKETS // 2
    n = jnp.maximum(dist, 0)
    nf = jnp.maximum(n, 1).astype(F32)
    large = max_exact + (jnp.log(nf / max_exact) / math.log(REL_MAX_DIST / max_exact)
                         * (REL_BUCKETS - max_exact)).astype(jnp.int32)
    large = jnp.minimum(large, REL_BUCKETS - 1)
    return jnp.where(n < max_exact, n, large)


def _bias_kernel(tab_ref, bucket_ref, o_ref):
    bucket = bucket_ref[...]
    for h in range(SW_HEADS):
        acc = jnp.zeros(bucket.shape, F32)
        for bkt in range(REL_BUCKETS):
            acc = jnp.where(bucket == bkt, tab_ref[bkt, h], acc)
        o_ref[h] = jnp.where(bucket >= 0, acc * LOG2E, NEG)


def _window_bias(rel_table):
    nk = WINDOW + TQ_SW
    key = jnp.arange(nk)[:, None]
    qry = jnp.arange(TQ_SW)[None, :]
    dist = qry + WINDOW - key
    valid = (dist >= 0) & (dist < WINDOW)
    bucket = jnp.where(valid, _t5_bucket(dist), -1).astype(jnp.int32)
    return pl.pallas_call(
        _bias_kernel,
        out_shape=jax.ShapeDtypeStruct((SW_HEADS, nk, TQ_SW), F32),
        in_specs=[pl.BlockSpec(memory_space=pltpu.SMEM),
                  pl.BlockSpec(memory_space=pltpu.VMEM)],
        out_specs=pl.BlockSpec(memory_space=pltpu.VMEM),
        name="window_bias",
    )(rel_table, bucket)


def _half_rms(x, gain, scale):
    lo = lax.broadcasted_iota(jnp.int32, x.shape, 1) < HALF
    sq = x * x
    s_lo = jnp.sum(jnp.where(lo, sq, 0.0), axis=-1, keepdims=True)
    s_hi = jnp.sum(jnp.where(lo, 0.0, sq), axis=-1, keepdims=True)
    r = jnp.where(lo, lax.rsqrt(s_lo * (1.0 / HEAD_DIM) + EPS), lax.rsqrt(s_hi * (1.0 / HEAD_DIM) + EPS))
    return x * r * (gain * scale)


def _prep_kernel(x_ref, mod_ref, n1g_ref, win_ref, cqg_ref, wuq_ref, ckvg_ref, wukv_ref,
                 gq_ref, gqs_ref, gk_ref, gks_ref, swqg_ref, swkg_ref, cos_ref, sin_ref,
                 sbq_ref, sbk_ref, sbvt_ref, mq_ref, mk_ref, mvt_ref, swq_ref, swk_ref, swvt_ref):
    x = x_ref[0]
    mod = mod_ref[0]
    shift1, scale1 = mod[0:1], mod[1:2]
    h = (_rms(x, D_MODEL) * n1g_ref[...] * (1.0 + scale1) + shift1).astype(BF16)

    proj_mla = _dot(h, win_ref[:, C_CQ:C_SWQ])
    proj_sb = _dot(h, win_ref[:, C_SBQ:C_CQ])
    cq = proj_mla[:, 0:MLA_Q_RANK]
    ckv = proj_mla[:, C_CKV - C_CQ:C_CKV - C_CQ + MLA_KV_RANK]
    krope = proj_mla[:, C_KROPE - C_CQ:C_KROPE - C_CQ + SLOT]
    krope_sw = proj_mla[:, C_KROPE_SW - C_CQ:C_KROPE_SW - C_CQ + SLOT]
    cqn = (_rms(cq, MLA_Q_RANK) * cqg_ref[...]).astype(BF16)
    ckvn = (_rms(ckv, MLA_KV_RANK) * ckvg_ref[...]).astype(BF16)
    qraw = _dot(cqn, wuq_ref[...])
    kv = _dot(ckvn, wukv_ref[...])
    proj_sw = _dot(h, win_ref[:, C_SWQ:N_IN])

    sbq_ref[0] = (proj_sb[:, C_SBQ:C_SBQ + 256] * (HEAD_DIM ** -0.5 * LOG2E)).astype(BF16)
    sbk_ref[0] = proj_sb[:, C_SBK:C_SBK + 256].astype(BF16)
    sbvt_ref[0] = proj_sb[:, C_SBV:C_SBV + 256].T.astype(BF16)

    cos = cos_ref[0]
    sin = sin_ref[0]

    nq = MLA_HEADS * SLOT
    q_scale = MLA_QK ** -0.5 * LOG2E
    q_cos, q_sin = gq_ref[...] * cos, gqs_ref[...] * sin
    for hd in range(MLA_HEADS):
        slot = qraw[:, hd * SLOT:(hd + 1) * SLOT]
        swapped = qraw[:, nq + hd * SLOT:nq + (hd + 1) * SLOT]
        r = lax.rsqrt(jnp.sum(slot * slot, axis=-1, keepdims=True) * (1.0 / MLA_QK) + EPS) * q_scale
        mq_ref[0, :, hd * SLOT:(hd + 1) * SLOT] = ((slot * q_cos + swapped * q_sin) * r).astype(BF16)

    mvt_ref[0] = kv[:, nq:nq + MLA_HEADS * MLA_V].T.astype(BF16)
    k_cos = gk_ref[...] * cos
    k_rot = krope_sw * (gks_ref[...] * sin)
    for hd in range(MLA_HEADS):
        slot = kv[:, hd * SLOT:(hd + 1) * SLOT] + krope
        r = lax.rsqrt(jnp.sum(slot * slot, axis=-1, keepdims=True) * (1.0 / MLA_QK) + EPS)
        mk_ref[0, :, hd * SLOT:(hd + 1) * SLOT] = ((slot * k_cos + k_rot) * r).astype(BF16)

    for g in range(SW_HEADS // 2):
        xq = proj_sw[:, g * SLOT:(g + 1) * SLOT]
        swq_ref[0, :, g * SLOT:(g + 1) * SLOT] = _half_rms(
            xq, swqg_ref[...], HEAD_DIM ** -0.5 * LOG2E).astype(BF16)
    swk_ref[0] = _half_rms(proj_mla[:, C_SWK - C_CQ:C_SWK - C_CQ + SLOT], swkg_ref[...], 1.0).astype(BF16)
    swvt_ref[0] = proj_sw[:, C_SWV - C_SWQ:C_SWV - C_SWQ + SLOT].T.astype(BF16)


def _prep(x, mods, n1g, w_in, cqg, w_uq, ckvg, w_ukv, gq, gqs, gk, gks, swqg, swkg, cos_t, sin_t):
    b, s, d = x.shape
    tm = TM_PREP
    row = lambda w: pl.BlockSpec((1, tm, w), lambda i, j: (i, j, 0))
    colt = lambda w: pl.BlockSpec((1, w, tm), lambda i, j: (i, 0, j))
    full = lambda a: pl.BlockSpec(a.shape, lambda i, j: (0,) * a.ndim)
    act = lambda w: jax.ShapeDtypeStruct((b, s, w), BF16)
    actt = lambda w: jax.ShapeDtypeStruct((b, w, s), BF16)
    return pl.pallas_call(
        _prep_kernel,
        out_shape=(act(256), act(256), actt(256), act(768), act(768), actt(384),
                   act(384), act(128), actt(128)),
        grid=(b, s // tm),
        in_specs=[row(d), pl.BlockSpec((1, 6, d), lambda i, j: (i, 0, 0)), full(n1g), full(w_in),
                  full(cqg), full(w_uq), full(ckvg), full(w_ukv), full(gq), full(gqs), full(gk),
                  full(gks), full(swqg), full(swkg), row(SLOT), row(SLOT)],
        out_specs=(row(256), row(256), colt(256), row(768), row(768), colt(384),
                   row(384), row(128), colt(128)),
        compiler_params=_cparams(2),
        name="prep_qkv",
    )(x, mods, n1g, w_in, cqg, w_uq, ckvg, w_ukv, gq, gqs, gk, gks, swqg, swkg, cos_t, sin_t)


def _half_mask(q, half):
    lane = lax.broadcasted_iota(jnp.int32, q.shape, 1)
    keep = (lane < HALF) if half == 0 else (lane >= HALF)
    return jnp.where(keep, q, jnp.zeros_like(q))


def _store_pair(o_ref, g, out_lo, out_hi, row0=0):
    pair = jnp.concatenate([out_lo, out_hi], axis=0)
    o_ref[0, row0:row0 + pair.shape[1], g * SLOT:(g + 1) * SLOT] = pair.T.astype(o_ref.dtype)


SIGN_BIT = -2 ** 31
JOBS = (("a", 0), ("b", 0), ("b", 1), ("a", 1), ("b", 2), ("b", 3), ("a", 2), ("b", 4), ("b", 5), ("a", 3))


def _softplus2(z):
    neg_abs = lax.bitcast_convert_type(lax.bitcast_convert_type(z, jnp.int32) | SIGN_BIT, F32)
    return jnp.maximum(z, 0.0) + jnp.log(1.0 + jnp.exp2(neg_abs)) * LOG2E


def _causal_kernel(qb, sink_ref, aq_ref, ak_ref, avt_ref, tri_ref, bq_ref, bk_ref, bvt_ref,
                   cq_ref, ck_ref, cvt_ref, bias_ref, pa_ref, pb_ref, pc_ref, oa_ref, ob_ref, oc_ref):
    del pa_ref, pb_ref, pc_ref
    n_diag = TQ // TK
    a_heads, b_heads = range(SB_HEADS), range(MLA_HEADS)

    def put(full, part, c0):
        return part if c0 == 0 else jnp.concatenate([full[:, :c0], part], axis=1)

    def run_jobs(carry, jobs):
        ca, cb = list(carry[0]), list(carry[1])
        st = [None] * len(jobs)
        out_c = {}

        def geometry(t):
            mixer, hd, start, diag = jobs[t]
            c0 = 0 if diag is None else diag * TK
            row = lax.broadcasted_iota(jnp.int32, (TK, TQ - c0), 0)
            col = lax.broadcasted_iota(jnp.int32, (TK, TQ - c0), 1)
            return mixer, hd, start, diag is not None, c0, row, col

        def window(t):
            _, slot, sub = jobs[t]
            head = slot // 2 + (slot % 2) * (SW_HEADS // SW_KV_HEADS)
            key0 = qb * TQ + sub * TQ_SW - WINDOW
            lo = max(key0, 0)
            return slot, sub, head, lo, key0 + WINDOW + TQ_SW, lo - key0

        def scores(t):
            if jobs[t][0] == "c":
                slot, sub, head, lo, hi, skip = window(t)
                q = _half_mask(cq_ref[0, sub * TQ_SW:(sub + 1) * TQ_SW, (slot // 2) * SLOT:(slot // 2 + 1) * SLOT],
                               slot % 2)
                st[t] = _nt_dot(ck_ref[0, lo:hi, :], q) + bias_ref[head, skip:, :]
                return
            mixer, hd, start, masked, c0, row, col = geometry(t)
            if mixer == "a":
                g, half = divmod(hd, 2)
                q = _half_mask(aq_ref[0, c0:, g * SLOT:(g + 1) * SLOT], half)
                st[t] = _nt_dot(ak_ref[0, pl.ds(start, TK), g * SLOT:(g + 1) * SLOT], q)
            else:
                sc = _nt_dot(bk_ref[0, pl.ds(start, TK), hd * SLOT:(hd + 1) * SLOT],
                             bq_ref[0, c0:, hd * SLOT:(hd + 1) * SLOT])
                st[t] = jnp.where(row <= col, sc, NEG) if masked else sc

        def second(t):
            if jobs[t][0] == "c":
                slot, sub, head, lo, hi, skip = window(t)
                sink = sink_ref[head] * LOG2E
                m = jnp.maximum(jnp.max(st[t], axis=0, keepdims=True), sink)
                p = jnp.exp2(st[t] - m)
                denom = jnp.sum(p, axis=0, keepdims=True) + jnp.exp2(sink - m)
                vt = cvt_ref[0, (slot % 2) * HEAD_DIM:(slot % 2 + 1) * HEAD_DIM, lo:hi]
                out_c[(sub, slot)] = _dot(vt, p.astype(BF16)) * (1.0 / denom)
                st[t] = None
                return
            mixer, hd, start, masked, c0, row, col = geometry(t)
            if mixer == "a":
                drop = _softplus2(st[t])
                if masked:
                    drop = jnp.where(row < col, drop, 0.0)
                st[t] = (st[t], _dot(tri_ref[...], drop.astype(BF16)))
            else:
                m_all, l_all, acc_all = cb[hd]
                m_old = m_all[:, c0:]
                m_new = jnp.maximum(m_old, jnp.max(st[t], axis=0, keepdims=True))
                alpha = jnp.exp2(m_old - m_new)
                p = jnp.exp2(st[t] - m_new)
                l_new = alpha * l_all[:, c0:] + jnp.sum(p, axis=0, keepdims=True)
                pv = _dot(bvt_ref[0, hd * MLA_V:(hd + 1) * MLA_V, pl.ds(start, TK)], p.astype(BF16))
                cb[hd] = (put(m_all, m_new, c0), put(l_all, l_new, c0),
                          put(acc_all, alpha * acc_all[:, c0:] + pv, c0))

        def third(t):
            if jobs[t][0] == "c":
                return
            mixer, hd, start, masked, c0, row, col = geometry(t)
            if mixer == "a":
                z, incl = st[t]
                acc, run = ca[hd]
                w = jnp.exp2(z + incl + run[:, c0:])
                if masked:
                    w = jnp.where(row < col, w, 0.0)
                pv = _dot(avt_ref[0, hd * HEAD_DIM:(hd + 1) * HEAD_DIM, pl.ds(start, TK)], w.astype(BF16))
                ca[hd] = (put(acc, acc[:, c0:] + pv, c0), put(run, run[:, c0:] + incl[0:1, :], c0))
            st[t] = None

        n = len(jobs)
        for t in range(n + 2):
            if t < n:
                scores(t)
            if 0 <= t - 1 < n:
                second(t - 1)
            if 0 <= t - 2 < n:
                third(t - 2)
        return (tuple(ca), tuple(cb)), out_c

    carry = (tuple((jnp.zeros((HEAD_DIM, TQ), F32), jnp.zeros((1, TQ), F32)) for _ in a_heads),
             tuple((jnp.full((1, TQ), NEG, F32), jnp.zeros((1, TQ), F32), jnp.zeros((MLA_V, TQ), F32))
                   for _ in b_heads))
    n_past = qb * n_diag
    past = [(j * TK, None) for j in reversed(range(n_past))]
    a_blocks = [((n_past + d) * TK, d) for d in reversed(range(n_diag))] + past
    b_blocks = [((n_past + d) * TK, d) for d in range(n_diag)] + past
    sweep = [(mixer, hd) + (a_blocks[i] if mixer == "a" else b_blocks[i])
             for i in range(len(a_blocks)) for (mixer, hd) in JOBS]
    local = [("c", slot, sub) for sub in range(TQ // TQ_SW) for slot in range(SW_HEADS)]
    jobs = []
    for job in sweep:
        jobs.append(job)
        if local:
            jobs.append(local.pop(0))
    (ca, cb), out_c = run_jobs(carry, jobs)
    for sub in range(TQ // TQ_SW):
        for g in range(SW_HEADS // 2):
            _store_pair(oc_ref, g, out_c[(sub, 2 * g)], out_c[(sub, 2 * g + 1)], sub * TQ_SW)
    for g in range(SB_HEADS // 2):
        _store_pair(oa_ref, g, ca[2 * g][0], ca[2 * g + 1][0])
    outs = [acc * (1.0 / l) for (_, l, acc) in cb]
    for g in range(MLA_HEADS // 2):
        _store_pair(ob_ref, g, outs[2 * g], outs[2 * g + 1])


def _attention(sinks, aq, ak, avt, bq, bk, bvt, cq, ck, cvt, bias):
    b, s, wa = aq.shape
    wb, wbv, wc = bq.shape[2], bvt.shape[1], cq.shape[2]
    tri = -(jnp.arange(TK)[None, :] >= jnp.arange(TK)[:, None]).astype(BF16)
    const = lambda a: pl.BlockSpec(a.shape, lambda i: (0,) * a.ndim)
    outs = tuple(jnp.zeros((b, s, w), BF16) for w in (wa, wbv, wc))
    n_in = 12
    for qb in range(s // TQ):
        nk = (qb + 1) * TQ
        qblk = lambda w, qb=qb: pl.BlockSpec((1, TQ, w), lambda i: (i, qb, 0))
        keys = lambda w, nk=nk: pl.BlockSpec((1, nk, w), lambda i: (i, 0, 0))
        keyst = lambda w, nk=nk: pl.BlockSpec((1, w, nk), lambda i: (i, 0, 0))
        outs = pl.pallas_call(
            functools.partial(_causal_kernel, qb),
            out_shape=tuple(jax.ShapeDtypeStruct(o.shape, o.dtype) for o in outs),
            grid=(b,),
            in_specs=[pl.BlockSpec(memory_space=pltpu.SMEM),
                      qblk(wa), keys(wa), keyst(wa), const(tri),
                      qblk(wb), keys(wb), keyst(wbv),
                      qblk(wc), keys(ck.shape[2]), keyst(cvt.shape[1]), const(bias)]
                     + [pl.BlockSpec(memory_space=pl.ANY)] * 3,
            out_specs=(qblk(wa), qblk(wbv), qblk(wc)),
            input_output_aliases={n_in + k: k for k in range(3)},
            compiler_params=_cparams(1),
            name=f"attention_q{qb}",
        )(sinks, aq, ak, avt, tri, bq, bk, bvt, cq, ck, cvt, bias, *outs)
    return outs


def _mlp_kernel(x_ref, mod_ref, oa_ref, ob_ref, oc_ref, wo_ref, n2g_ref, wup_ref, cw_ref, cb_ref,
                wdn_ref, out_ref, carry_ref):
    si = pl.program_id(1)
    tm = x_ref.shape[1]
    mod = mod_ref[0]
    gate1, shift2, scale2, gate2 = mod[2:3], mod[3:4], mod[4:5], mod[5:6]
    na, nb = oa_ref.shape[2], ob_ref.shape[2]
    att = (_dot(oa_ref[0], wo_ref[0:na, :]) + _dot(ob_ref[0], wo_ref[na:na + nb, :])
           + _dot(oc_ref[0], wo_ref[na + nb:, :]))
    x1 = x_ref[0] + gate1 * att
    h2 = (_rms(x1, D_MODEL) * n2g_ref[...] * (1.0 + scale2) + shift2).astype(BF16)

    @pl.when(si == 0)
    def _():
        carry_ref[...] = jnp.zeros_like(carry_ref)

    def up(c):
        lo = c * FF_CHUNK
        return (_dot(h2, wup_ref[:, lo:lo + FF_CHUNK]),
                _dot(h2, wup_ref[:, D_FF + lo:D_FF + lo + FF_CHUNK]))

    def conv(u, col):
        cols = slice(col, col + FF_CHUNK)
        prev = carry_ref[:, cols]
        carry_ref[:, cols] = u[tm - CARRY_ROWS:, :]
        ext = jnp.concatenate([prev, u], axis=0)
        u1 = ext[CARRY_ROWS - 1:CARRY_ROWS - 1 + tm, :]
        u2 = ext[CARRY_ROWS - 2:CARRY_ROWS - 2 + tm, :]
        cw = cw_ref[:, cols]
        return u * cw[2:3] + u1 * cw[1:2] + u2 * cw[0:1] + cb_ref[:, cols]

    n_chunks = D_FF // FF_CHUNK
    u_next = up(0)
    for c in range(n_chunks):
        u_gate, u_val = u_next
        if c + 1 < n_chunks:
            u_next = up(c + 1)
        gate = conv(u_gate, c * FF_CHUNK)
        val = conv(u_val, D_FF + c * FF_CHUNK)
        a = (gate * jax.nn.sigmoid(gate) * val).astype(BF16)
        part = _dot(a, wdn_ref[c * FF_CHUNK:(c + 1) * FF_CHUNK, :])
        acc = part if c == 0 else acc + part
    out_ref[0] = x1 + gate2 * acc


def _mlp(x, mods, oa, ob, oc, w_out, n2g, w_up, conv_w, conv_b, w_down):
    b, s, d = x.shape
    tm = TM_MLP
    row = lambda w: pl.BlockSpec((1, tm, w), lambda i, j: (i, j, 0))
    const = lambda a: pl.BlockSpec(a.shape, lambda i, j: (0,) * a.ndim, pipeline_mode=pl.Buffered(1))
    return pl.pallas_call(
        _mlp_kernel,
        out_shape=jax.ShapeDtypeStruct((b, s, d), F32),
        grid=(b, s // tm),
        in_specs=[row(d), pl.BlockSpec((1, 6, d), lambda i, j: (i, 0, 0)),
                  row(oa.shape[2]), row(ob.shape[2]), row(oc.shape[2]),
                  const(w_out), const(n2g), const(w_up), const(conv_w), const(conv_b), const(w_down)],
        out_specs=row(d),
        scratch_shapes=[pltpu.VMEM((CARRY_ROWS, 2 * D_FF), F32)],
        compiler_params=_cparams(2),
        name="outproj_mlp",
    )(x, mods, oa, ob, oc, w_out, n2g, w_up, conv_w, conv_b, w_down)


def _slot_gain(g, swap):
    z = jnp.zeros((SLOT - MLA_QK,), F32)
    lo, hi = g[ROPE_LO:ROPE_LO + ROPE_HALF], g[ROPE_LO + ROPE_HALF:MLA_QK]
    if swap:
        return jnp.concatenate([jnp.zeros((MLA_NOPE,), F32), hi, lo, z]).reshape(1, SLOT)
    return jnp.concatenate([g, z]).reshape(1, SLOT)


def _layout_w_in(w):
    d = w.shape[0]
    z = lambda n: jnp.zeros((d, n), w.dtype)
    kr = w[:, 1152:1184]
    swq = w[:, 1184:1568].reshape(d, SW_HEADS, HEAD_DIM)
    order = [0, 3, 1, 4, 2, 5]
    swq = swq[:, order, :].reshape(d, SW_HEADS * HEAD_DIM)
    cols = [w[:, 0:1152],
            z(MLA_NOPE), kr, z(SLOT - MLA_QK),
            z(MLA_NOPE), kr[:, ROPE_HALF:], kr[:, :ROPE_HALF], z(SLOT - MLA_QK),
            w[:, 1568:1696], swq, w[:, 1696:1824]]
    return jnp.concatenate(cols, axis=1).astype(BF16)


def _layout_w_uq(w):
    r = w.shape[0]
    w = w.reshape(r, MLA_HEADS, MLA_QK)
    nope, x1, x2 = w[..., :MLA_NOPE], w[..., MLA_NOPE:MLA_NOPE + ROPE_HALF], w[..., MLA_NOPE + ROPE_HALF:]
    z = jnp.zeros((r, MLA_HEADS, SLOT - MLA_QK), w.dtype)
    plain = jnp.concatenate([nope, x1, x2, z], axis=-1).reshape(r, MLA_HEADS * SLOT)
    swapped = jnp.concatenate([jnp.zeros_like(nope), x2, x1, z], axis=-1).reshape(r, MLA_HEADS * SLOT)
    return jnp.concatenate([plain, swapped], axis=1).astype(BF16)


def _layout_w_ukv(w):
    r = w.shape[0]
    w = w.reshape(r, MLA_HEADS, MLA_NOPE + MLA_V)
    k_nope, v = w[..., :MLA_NOPE], w[..., MLA_NOPE:]
    k_slots = jnp.concatenate([k_nope, jnp.zeros((r, MLA_HEADS, SLOT - MLA_NOPE), w.dtype)], axis=-1)
    return jnp.concatenate([k_slots.reshape(r, MLA_HEADS * SLOT), v.reshape(r, MLA_HEADS * MLA_V)],
                           axis=1).astype(BF16)


def _layout_w_out(w):
    n_ab = SB_HEADS * HEAD_DIM + MLA_HEADS * MLA_V
    sw = w[n_ab:].reshape(SW_HEADS, HEAD_DIM, w.shape[1])[jnp.array([0, 3, 1, 4, 2, 5])]
    return jnp.concatenate([w[:n_ab], sw.reshape(SW_HEADS * HEAD_DIM, w.shape[1])], axis=0).astype(BF16)


def kernel(x, c, positions, rel_table, norm1_g, norm2_g, w_ada, b_ada, w_in, mla_cq_g, w_uq, mla_ckv_g,
           w_ukv, mla_qn_g, mla_kn_g, sw_qn_g, sw_kn_g, sw_sinks, w_out, w_up, conv_w, conv_b, w_down):
    depth = w_in.shape[0]
    b = x.shape[0]
    mods = _mods(c, w_ada, b_ada).reshape(depth, b, 6, D_MODEL)
    cos_t, sin_t = _rope_tables(positions)
    bias = _window_bias(rel_table)
    row = lambda v: v.reshape(1, -1).astype(F32)
    two = lambda v: jnp.concatenate([v, v]).reshape(1, SLOT).astype(F32)
    for l in range(depth):
        qkv = _prep(x, mods[l], row(norm1_g[l]), _layout_w_in(w_in[l]), row(mla_cq_g[l]),
                    _layout_w_uq(w_uq[l]), row(mla_ckv_g[l]), _layout_w_ukv(w_ukv[l]),
                    _slot_gain(mla_qn_g[l], False), _slot_gain(mla_qn_g[l], True),
                    _slot_gain(mla_kn_g[l], False), _slot_gain(mla_kn_g[l], True),
                    two(sw_qn_g[l]), two(sw_kn_g[l]), cos_t, sin_t)
        sbq, sbk, sbvt, mq, mk, mvt, swq, swk, swvt = qkv
        o_a, o_b, o_c = _attention(sw_sinks[l], sbq, sbk, sbvt, mq, mk, mvt, swq, swk, swvt, bias)
        x = _mlp(x, mods[l], o_a, o_b, o_c, _layout_w_out(w_out[l]), row(norm2_g[l]),
                 w_up[l].astype(BF16), conv_w[l], row(conv_b[l]), w_down[l].astype(BF16))
    return x
```

```python
import functools
import math

import numpy as np
import jax
import jax.numpy as jnp
from jax import lax
from jax.experimental import pallas as pl
from jax.experimental.pallas import tpu as pltpu

F32 = jnp.float32
BF16 = jnp.bfloat16

D_MODEL = 1024
HEAD_DIM = 64
SB_HEADS = 4
MLA_HEADS = 6
MLA_Q_RANK = 256
MLA_KV_RANK = 128
MLA_NOPE = 64
MLA_ROPE = 32
MLA_V = 64
MLA_QK = MLA_NOPE + MLA_ROPE
ROPE_THETA = 10000.0
SW_HEADS = 6
SW_KV_HEADS = 2
WINDOW = 128
REL_BUCKETS = 32
REL_MAX_DIST = 128
D_FF = 2816
CONV_W = 3
EPS = 1e-6
NEG = -1e30

LANES = 128
SLOT = LANES
HALF = SLOT // 2

C_SBQ, C_SBK, C_SBV = 0, 256, 512
C_CQ = 768
C_CKV = 1024
C_KROPE = 1152
C_KROPE_SW = 1280
C_SWK = 1408
C_SWQ = 1536
C_SWV = 1920
N_IN = 2048

ROPE_LO = MLA_NOPE
ROPE_HALF = MLA_ROPE // 2

TM_PREP = 512
TM_MLP = 512
TQ = 512
TK = 256
TQ_SW = 256
LOG2E = math.log2(math.e)
FF_CHUNK = 256
CARRY_ROWS = 8

VMEM_LIMIT = 56 * 1024 * 1024


def _cparams(n_axes):
    return pltpu.CompilerParams(dimension_semantics=("arbitrary",) * n_axes,
                                vmem_limit_bytes=VMEM_LIMIT)


def _rms(x, n):
    return x * lax.rsqrt(jnp.sum(x * x, axis=-1, keepdims=True) * (1.0 / n) + EPS)


def _nt_dot(a, b):
    return lax.dot_general(a, b, (((1,), (1,)), ((), ())), preferred_element_type=F32)


def _dot(a, b):
    return jnp.dot(a, b, preferred_element_type=F32)


def _mods_kernel(c_ref, w_ref, b_ref, o_ref):
    c = c_ref[...]
    a = (c * jax.nn.sigmoid(c)).astype(BF16)
    o_ref[0] = _dot(a, w_ref[0].astype(BF16)) + b_ref[0]


def _mods(c, w_ada, b_ada):
    depth, d, n = w_ada.shape
    b = c.shape[0]
    tn = 1536
    return pl.pallas_call(
        _mods_kernel,
        out_shape=jax.ShapeDtypeStruct((depth, b, n), F32),
        grid=(depth, n // tn),
        in_specs=[pl.BlockSpec((b, d), lambda l, j: (0, 0)),
                  pl.BlockSpec((1, d, tn), lambda l, j: (l, 0, j)),
                  pl.BlockSpec((1, 1, tn), lambda l, j: (l, 0, j))],
        out_specs=pl.BlockSpec((1, b, tn), lambda l, j: (l, 0, j)),
        compiler_params=_cparams(2),
        name="adaln_mods",
    )(c, w_ada, b_ada.reshape(depth, 1, n))


def _rope_kernel(pos_ref, invf_ref, cos_ref, sin_ref):
    pos = pos_ref[0].astype(F32)
    ang = invf_ref[...] * pos
    c = jnp.cos(ang)
    s = jnp.sin(ang)
    tm = pos.shape[1]
    ones = jnp.ones((ROPE_LO, tm), F32)
    zeros = jnp.zeros((ROPE_LO, tm), F32)
    pad = SLOT - ROPE_LO - MLA_ROPE
    cos_t = jnp.concatenate([ones, c, c, jnp.ones((pad, tm), F32)], axis=0)
    sin_t = jnp.concatenate([zeros, -s, s, jnp.zeros((pad, tm), F32)], axis=0)
    cos_ref[0] = cos_t.T
    sin_ref[0] = sin_t.T


def _rope_tables(positions):
    b, s = positions.shape
    tm = s
    half = ROPE_HALF
    inv_freq = jnp.power(ROPE_THETA, -jnp.arange(half, dtype=F32) / half).reshape(half, 1)
    out = jax.ShapeDtypeStruct((b, s, SLOT), F32)
    return pl.pallas_call(
        _rope_kernel,
        out_shape=(out, out),
        grid=(b, s // tm),
        in_specs=[pl.BlockSpec((1, 1, tm), lambda i, j: (i, 0, j)),
                  pl.BlockSpec((half, 1), lambda i, j: (0, 0))],
        out_specs=(pl.BlockSpec((1, tm, SLOT), lambda i, j: (i, j, 0)),
                   pl.BlockSpec((1, tm, SLOT), lambda i, j: (i, j, 0))),
        compiler_params=_cparams(2),
        name="rope_tables",
    )(positions.reshape(b, 1, s), inv_freq)


def _t5_bucket(dist):
    max_exact = REL_BUCKETS // 2
    n = jnp.maximum(dist, 0)
    nf = jnp.maximum(n, 1).astype(F32)
    large = max_exact + (jnp.log(nf / max_exact) / math.log(REL_MAX_DIST / max_exact)
                         * (REL_BUCKETS - max_exact)).astype(jnp.int32)
    large = jnp.minimum(large, REL_BUCKETS - 1)
    return jnp.where(n < max_exact, n, large)


def _bias_kernel(tab_ref, bucket_ref, o_ref):
    bucket = bucket_ref[...]
    for h in range(SW_HEADS):
        acc = jnp.zeros(bucket.shape, F32)
        for bkt in range(REL_BUCKETS):
            acc = jnp.where(bucket == bkt, tab_ref[bkt, h], acc)
        o_ref[h] = jnp.where(bucket >= 0, acc * LOG2E, NEG)


def _window_bias(rel_table):
    nk = WINDOW + TQ_SW
    key = jnp.arange(nk)[:, None]
    qry = jnp.arange(TQ_SW)[None, :]
    dist = qry + WINDOW - key
    valid = (dist >= 0) & (dist < WINDOW)
    bucket = jnp.where(valid, _t5_bucket(dist), -1).astype(jnp.int32)
    return pl.pallas_call(
        _bias_kernel,
        out_shape=jax.ShapeDtypeStruct((SW_HEADS, nk, TQ_SW), F32),
        in_specs=[pl.BlockSpec(memory_space=pltpu.SMEM),
                  pl.BlockSpec(memory_space=pltpu.VMEM)],
        out_specs=pl.BlockSpec(memory_space=pltpu.VMEM),
        name="window_bias",
    )(rel_table, bucket)


def _half_rms(x, gain, scale):
    lo = lax.broadcasted_iota(jnp.int32, x.shape, 1) < HALF
    sq = x * x
    s_lo = jnp.sum(jnp.where(lo, sq, 0.0), axis=-1, keepdims=True)
    s_hi = jnp.sum(jnp.where(lo, 0.0, sq), axis=-1, keepdims=True)
    r = jnp.where(lo, lax.rsqrt(s_lo * (1.0 / HEAD_DIM) + EPS), lax.rsqrt(s_hi * (1.0 / HEAD_DIM) + EPS))
    return x * r * (gain * scale)


def _prep_kernel(x_ref, mod_ref, n1g_ref, win_ref, cqg_ref, wuq_ref, ckvg_ref, wukv_ref,
                 gq_ref, gqs_ref, gk_ref, gks_ref, swqg_ref, swkg_ref, cos_ref, sin_ref,
                 sbq_ref, sbk_ref, sbvt_ref, mq_ref, mk_ref, mvt_ref, swq_ref, swk_ref, swvt_ref):
    x = x_ref[0]
    mod = mod_ref[0]
    shift1, scale1 = mod[0:1], mod[1:2]
    h = (_rms(x, D_MODEL) * n1g_ref[...] * (1.0 + scale1) + shift1).astype(BF16)

    proj_mla = _dot(h, win_ref[:, C_CQ:C_SWQ])
    proj_sb = _dot(h, win_ref[:, C_SBQ:C_CQ])
    cq = proj_mla[:, 0:MLA_Q_RANK]
    ckv = proj_mla[:, C_CKV - C_CQ:C_CKV - C_CQ + MLA_KV_RANK]
    krope = proj_mla[:, C_KROPE - C_CQ:C_KROPE - C_CQ + SLOT]
    krope_sw = proj_mla[:, C_KROPE_SW - C_CQ:C_KROPE_SW - C_CQ + SLOT]
    cqn = (_rms(cq, MLA_Q_RANK) * cqg_ref[...]).astype(BF16)
    ckvn = (_rms(ckv, MLA_KV_RANK) * ckvg_ref[...]).astype(BF16)
    qraw = _dot(cqn, wuq_ref[...])
    kv = _dot(ckvn, wukv_ref[...])
    proj_sw = _dot(h, win_ref[:, C_SWQ:N_IN])

    sbq_ref[0] = (proj_sb[:, C_SBQ:C_SBQ + 256] * (HEAD_DIM ** -0.5 * LOG2E)).astype(BF16)
    sbk_ref[0] = proj_sb[:, C_SBK:C_SBK + 256].astype(BF16)
    sbvt_ref[0] = proj_sb[:, C_SBV:C_SBV + 256].T.astype(BF16)

    cos = cos_ref[0]
    sin = sin_ref[0]

    nq = MLA_HEADS * SLOT
    q_scale = MLA_QK ** -0.5 * LOG2E
    q_cos, q_sin = gq_ref[...] * cos, gqs_ref[...] * sin
    for hd in range(MLA_HEADS):
        slot = qraw[:, hd * SLOT:(hd + 1) * SLOT]
        swapped = qraw[:, nq + hd * SLOT:nq + (hd + 1) * SLOT]
        r = lax.rsqrt(jnp.sum(slot * slot, axis=-1, keepdims=True) * (1.0 / MLA_QK) + EPS) * q_scale
        mq_ref[0, :, hd * SLOT:(hd + 1) * SLOT] = ((slot * q_cos + swapped * q_sin) * r).astype(BF16)

    mvt_ref[0] = kv[:, nq:nq + MLA_HEADS * MLA_V].T.astype(BF16)
    k_cos = gk_ref[...] * cos
    k_rot = krope_sw * (gks_ref[...] * sin)
    for hd in range(MLA_HEADS):
        slot = kv[:, hd * SLOT:(hd + 1) * SLOT] + krope
        r = lax.rsqrt(jnp.sum(slot * slot, axis=-1, keepdims=True) * (1.0 / MLA_QK) + EPS)
        mk_ref[0, :, hd * SLOT:(hd + 1) * SLOT] = ((slot * k_cos + k_rot) * r).astype(BF16)

    for g in range(SW_HEADS // 2):
        xq = proj_sw[:, g * SLOT:(g + 1) * SLOT]
        swq_ref[0, :, g * SLOT:(g + 1) * SLOT] = _half_rms(
            xq, swqg_ref[...], HEAD_DIM ** -0.5 * LOG2E).astype(BF16)
    swk_ref[0] = _half_rms(proj_mla[:, C_SWK - C_CQ:C_SWK - C_CQ + SLOT], swkg_ref[...], 1.0).astype(BF16)
    swvt_ref[0] = proj_sw[:, C_SWV - C_SWQ:C_SWV - C_SWQ + SLOT].T.astype(BF16)


def _prep(x, mods, n1g, w_in, cqg, w_uq, ckvg, w_ukv, gq, gqs, gk, gks, swqg, swkg, cos_t, sin_t):
    b, s, d = x.shape
    tm = TM_PREP
    row = lambda w: pl.BlockSpec((1, tm, w), lambda i, j: (i, j, 0))
    colt = lambda w: pl.BlockSpec((1, w, tm), lambda i, j: (i, 0, j))
    full = lambda a: pl.BlockSpec(a.shape, lambda i, j: (0,) * a.ndim)
    act = lambda w: jax.ShapeDtypeStruct((b, s, w), BF16)
    actt = lambda w: jax.ShapeDtypeStruct((b, w, s), BF16)
    return pl.pallas_call(
        _prep_kernel,
        out_shape=(act(256), act(256), actt(256), act(768), act(768), actt(384),
                   act(384), act(128), actt(128)),
        grid=(b, s // tm),
        in_specs=[row(d), pl.BlockSpec((1, 6, d), lambda i, j: (i, 0, 0)), full(n1g), full(w_in),
                  full(cqg), full(w_uq), full(ckvg), full(w_ukv), full(gq), full(gqs), full(gk),
                  full(gks), full(swqg), full(swkg), row(SLOT), row(SLOT)],
        out_specs=(row(256), row(256), colt(256), row(768), row(768), colt(384),
                   row(384), row(128), colt(128)),
        compiler_params=_cparams(2),
        name="prep_qkv",
    )(x, mods, n1g, w_in, cqg, w_uq, ckvg, w_ukv, gq, gqs, gk, gks, swqg, swkg, cos_t, sin_t)


def _half_mask(q, half):
    lane = lax.broadcasted_iota(jnp.int32, q.shape, 1)
    keep = (lane < HALF) if half == 0 else (lane >= HALF)
    return jnp.where(keep, q, jnp.zeros_like(q))


def _store_pair(o_ref, g, out_lo, out_hi, row0=0):
    pair = jnp.concatenate([out_lo, out_hi], axis=0)
    o_ref[0, row0:row0 + pair.shape[1], g * SLOT:(g + 1) * SLOT] = pair.T.astype(o_ref.dtype)


SIGN_BIT = -2 ** 31
JOBS = (("a", 0), ("b", 0), ("b", 1), ("a", 1), ("b", 2), ("b", 3), ("a", 2), ("b", 4), ("b", 5), ("a", 3))


def _softplus2(z):
    neg_abs = lax.bitcast_convert_type(lax.bitcast_convert_type(z, jnp.int32) | SIGN_BIT, F32)
    return jnp.maximum(z, 0.0) + jnp.log(1.0 + jnp.exp2(neg_abs)) * LOG2E


def _causal_kernel(qb, sink_ref, aq_ref, ak_ref, avt_ref, tri_ref, bq_ref, bk_ref, bvt_ref,
                   cq_ref, ck_ref, cvt_ref, bias_ref, *out_refs):
    oa_ref, ob_ref, oc_ref = out_refs[-3:]
    if qb == 0:
        for o_ref in (oa_ref, ob_ref, oc_ref):
            o_ref[0, TQ:, :] = jnp.zeros((o_ref.shape[1] - TQ, o_ref.shape[2]), o_ref.dtype)
    n_diag = TQ // TK
    a_heads, b_heads = range(SB_HEADS), range(MLA_HEADS)

    def put(full, part, c0):
        return part if c0 == 0 else jnp.concatenate([full[:, :c0], part], axis=1)

    def run_jobs(carry, jobs):
        ca, cb = list(carry[0]), list(carry[1])
        st = [None] * len(jobs)
        out_c = {}

        def geometry(t):
            mixer, hd, start, diag = jobs[t]
            c0 = 0 if diag is None else diag * TK
            row = lax.broadcasted_iota(jnp.int32, (TK, TQ - c0), 0)
            col = lax.broadcasted_iota(jnp.int32, (TK, TQ - c0), 1)
            return mixer, hd, start, diag is not None, c0, row, col

        def window(t):
            _, slot, sub = jobs[t]
            head = slot // 2 + (slot % 2) * (SW_HEADS // SW_KV_HEADS)
            key0 = qb * TQ + sub * TQ_SW - WINDOW
            lo = max(key0, 0)
            return slot, sub, head, lo, key0 + WINDOW + TQ_SW, lo - key0

        def scores(t):
            if jobs[t][0] == "c":
                slot, sub, head, lo, hi, skip = window(t)
                q = _half_mask(cq_ref[0, sub * TQ_SW:(sub + 1) * TQ_SW, (slot // 2) * SLOT:(slot // 2 + 1) * SLOT],
                               slot % 2)
                st[t] = _nt_dot(ck_ref[0, lo:hi, :], q) + bias_ref[head, skip:, :]
                return
            mixer, hd, start, masked, c0, row, col = geometry(t)
            if mixer == "a":
                g, half = divmod(hd, 2)
                q = _half_mask(aq_ref[0, c0:, g * SLOT:(g + 1) * SLOT], half)
                st[t] = _nt_dot(ak_ref[0, pl.ds(start, TK), g * SLOT:(g + 1) * SLOT], q)
            else:
                sc = _nt_dot(bk_ref[0, pl.ds(start, TK), hd * SLOT:(hd + 1) * SLOT],
                             bq_ref[0, c0:, hd * SLOT:(hd + 1) * SLOT])
                st[t] = jnp.where(row <= col, sc, NEG) if masked else sc

        def second(t):
            if jobs[t][0] == "c":
                slot, sub, head, lo, hi, skip = window(t)
                sink = sink_ref[head] * LOG2E
                m = jnp.maximum(jnp.max(st[t], axis=0, keepdims=True), sink)
                p = jnp.exp2(st[t] - m)
                denom = jnp.sum(p, axis=0, keepdims=True) + jnp.exp2(sink - m)
                vt = cvt_ref[0, (slot % 2) * HEAD_DIM:(slot % 2 + 1) * HEAD_DIM, lo:hi]
                out_c[(sub, slot)] = _dot(vt, p.astype(BF16)) * (1.0 / denom)
                st[t] = None
                return
            mixer, hd, start, masked, c0, row, col = geometry(t)
            if mixer == "a":
                drop = _softplus2(st[t])
                if masked:
                    drop = jnp.where(row < col, drop, 0.0)
                st[t] = (st[t], _dot(tri_ref[...], drop.astype(BF16)))
            else:
                m_all, l_all, acc_all = cb[hd]
                m_old = m_all[:, c0:]
                m_new = jnp.maximum(m_old, jnp.max(st[t], axis=0, keepdims=True))
                alpha = jnp.exp2(m_old - m_new)
                p = jnp.exp2(st[t] - m_new)
                l_new = alpha * l_all[:, c0:] + jnp.sum(p, axis=0, keepdims=True)
                pv = _dot(bvt_ref[0, hd * MLA_V:(hd + 1) * MLA_V, pl.ds(start, TK)], p.astype(BF16))
                cb[hd] = (put(m_all, m_new, c0), put(l_all, l_new, c0),
                          put(acc_all, alpha * acc_all[:, c0:] + pv, c0))

        def third(t):
            if jobs[t][0] == "c":
                return
            mixer, hd, start, masked, c0, row, col = geometry(t)
            if mixer == "a":
                z, incl = st[t]
                acc, run = ca[hd]
                w = jnp.exp2(z + incl + run[:, c0:])
                if masked:
                    w = jnp.where(row < col, w, 0.0)
                pv = _dot(avt_ref[0, hd * HEAD_DIM:(hd + 1) * HEAD_DIM, pl.ds(start, TK)], w.astype(BF16))
                ca[hd] = (put(acc, acc[:, c0:] + pv, c0), put(run, run[:, c0:] + incl[0:1, :], c0))
            st[t] = None

        n = len(jobs)
        for t in range(n + 2):
            if t < n:
                scores(t)
            if 0 <= t - 1 < n:
                second(t - 1)
            if 0 <= t - 2 < n:
                third(t - 2)
        return (tuple(ca), tuple(cb)), out_c

    carry = (tuple((jnp.zeros((HEAD_DIM, TQ), F32), jnp.zeros((1, TQ), F32)) for _ in a_heads),
             tuple((jnp.full((1, TQ), NEG, F32), jnp.zeros((1, TQ), F32), jnp.zeros((MLA_V, TQ), F32))
                   for _ in b_heads))
    n_past = qb * n_diag
    past = [(j * TK, None) for j in reversed(range(n_past))]
    a_blocks = [((n_past + d) * TK, d) for d in reversed(range(n_diag))] + past
    b_blocks = [((n_past + d) * TK, d) for d in range(n_diag)] + past
    sweep = [(mixer, hd) + (a_blocks[i] if mixer == "a" else b_blocks[i])
             for i in range(len(a_blocks)) for (mixer, hd) in JOBS]
    local = [("c", slot, sub) for sub in range(TQ // TQ_SW) for slot in range(SW_HEADS)]
    jobs = []
    for job in sweep:
        jobs.append(job)
        if local:
            jobs.append(local.pop(0))
    (ca, cb), out_c = run_jobs(carry, jobs)
    for sub in range(TQ // TQ_SW):
        for g in range(SW_HEADS // 2):
            _store_pair(oc_ref, g, out_c[(sub, 2 * g)], out_c[(sub, 2 * g + 1)], sub * TQ_SW)
    for g in range(SB_HEADS // 2):
        _store_pair(oa_ref, g, ca[2 * g][0], ca[2 * g + 1][0])
    outs = [acc * (1.0 / l) for (_, l, acc) in cb]
    for g in range(MLA_HEADS // 2):
        _store_pair(ob_ref, g, outs[2 * g], outs[2 * g + 1])


def _attention(sinks, aq, ak, avt, bq, bk, bvt, cq, ck, cvt, bias):
    b, s, wa = aq.shape
    wb, wbv, wc = bq.shape[2], bvt.shape[1], cq.shape[2]
    tri = -(jnp.arange(TK)[None, :] >= jnp.arange(TK)[:, None]).astype(BF16)
    const = lambda a: pl.BlockSpec(a.shape, lambda i: (0,) * a.ndim)
    outs = ()
    n_in = 12
    for qb in range(s // TQ):
        nk = (qb + 1) * TQ
        qblk = lambda w, qb=qb: pl.BlockSpec((1, TQ, w), lambda i: (i, qb, 0))
        keys = lambda w, nk=nk: pl.BlockSpec((1, nk, w), lambda i: (i, 0, 0))
        keyst = lambda w, nk=nk: pl.BlockSpec((1, w, nk), lambda i: (i, 0, 0))
        oblk = (lambda w: pl.BlockSpec((1, s, w), lambda i: (i, 0, 0))) if qb == 0 else qblk
        outs = pl.pallas_call(
            functools.partial(_causal_kernel, qb),
            out_shape=tuple(jax.ShapeDtypeStruct((b, s, w), BF16) for w in (wa, wbv, wc)),
            grid=(b,),
            in_specs=[pl.BlockSpec(memory_space=pltpu.SMEM),
                      qblk(wa), keys(wa), keyst(wa), const(tri),
                      qblk(wb), keys(wb), keyst(wbv),
                      qblk(wc), keys(ck.shape[2]), keyst(cvt.shape[1]), const(bias)]
                     + [pl.BlockSpec(memory_space=pl.ANY)] * len(outs),
            out_specs=(oblk(wa), oblk(wbv), oblk(wc)),
            input_output_aliases={n_in + k: k for k in range(len(outs))},
            compiler_params=_cparams(1),
            name=f"attention_q{qb}",
        )(sinks, aq, ak, avt, tri, bq, bk, bvt, cq, ck, cvt, bias, *outs)
    return outs


def _mlp_kernel(x_ref, mod_ref, oa_ref, ob_ref, oc_ref, wo_ref, n2g_ref, wup_ref, cw_ref, cb_ref,
                wdn_ref, out_ref, carry_ref):
    si = pl.program_id(1)
    tm = x_ref.shape[1]
    mod = mod_ref[0]
    gate1, shift2, scale2, gate2 = mod[2:3], mod[3:4], mod[4:5], mod[5:6]
    na, nb = oa_ref.shape[2], ob_ref.shape[2]
    att = (_dot(oa_ref[0], wo_ref[0:na, :]) + _dot(ob_ref[0], wo_ref[na:na + nb, :])
           + _dot(oc_ref[0], wo_ref[na + nb:, :]))
    x1 = x_ref[0] + gate1 * att
    h2 = (_rms(x1, D_MODEL) * n2g_ref[...] * (1.0 + scale2) + shift2).astype(BF16)

    @pl.when(si == 0)
    def _():
        carry_ref[...] = jnp.zeros_like(carry_ref)

    def up(c):
        lo = c * FF_CHUNK
        return (_dot(h2, wup_ref[:, lo:lo + FF_CHUNK]),
                _dot(h2, wup_ref[:, D_FF + lo:D_FF + lo + FF_CHUNK]))

    def conv(u, col):
        cols = slice(col, col + FF_CHUNK)
        prev = carry_ref[:, cols]
        carry_ref[:, cols] = u[tm - CARRY_ROWS:, :]
        ext = jnp.concatenate([prev, u], axis=0)
        u1 = ext[CARRY_ROWS - 1:CARRY_ROWS - 1 + tm, :]
        u2 = ext[CARRY_ROWS - 2:CARRY_ROWS - 2 + tm, :]
        cw = cw_ref[:, cols]
        return u * cw[2:3] + u1 * cw[1:2] + u2 * cw[0:1] + cb_ref[:, cols]

    n_chunks = D_FF // FF_CHUNK
    u_next = up(0)
    for c in range(n_chunks):
        u_gate, u_val = u_next
        if c + 1 < n_chunks:
            u_next = up(c + 1)
        gate = conv(u_gate, c * FF_CHUNK)
        val = conv(u_val, D_FF + c * FF_CHUNK)
        a = (gate * jax.nn.sigmoid(gate) * val).astype(BF16)
        part = _dot(a, wdn_ref[c * FF_CHUNK:(c + 1) * FF_CHUNK, :])
        acc = part if c == 0 else acc + part
    out_ref[0] = x1 + gate2 * acc


def _mlp(x, mods, oa, ob, oc, w_out, n2g, w_up, conv_w, conv_b, w_down):
    b, s, d = x.shape
    tm = TM_MLP
    row = lambda w: pl.BlockSpec((1, tm, w), lambda i, j: (i, j, 0))
    const = lambda a: pl.BlockSpec(a.shape, lambda i, j: (0,) * a.ndim, pipeline_mode=pl.Buffered(1))
    return pl.pallas_call(
        _mlp_kernel,
        out_shape=jax.ShapeDtypeStruct((b, s, d), F32),
        grid=(b, s // tm),
        in_specs=[row(d), pl.BlockSpec((1, 6, d), lambda i, j: (i, 0, 0)),
                  row(oa.shape[2]), row(ob.shape[2]), row(oc.shape[2]),
                  const(w_out), const(n2g), const(w_up), const(conv_w), const(conv_b), const(w_down)],
        out_specs=row(d),
        scratch_shapes=[pltpu.VMEM((CARRY_ROWS, 2 * D_FF), F32)],
        compiler_params=_cparams(2),
        name="outproj_mlp",
    )(x, mods, oa, ob, oc, w_out, n2g, w_up, conv_w, conv_b, w_down)


def _slot_gain(g, swap):
    z = jnp.zeros((SLOT - MLA_QK,), F32)
    lo, hi = g[ROPE_LO:ROPE_LO + ROPE_HALF], g[ROPE_LO + ROPE_HALF:MLA_QK]
    if swap:
        return jnp.concatenate([jnp.zeros((MLA_NOPE,), F32), hi, lo, z]).reshape(1, SLOT)
    return jnp.concatenate([g, z]).reshape(1, SLOT)


def _layout_w_in(w):
    d = w.shape[0]
    z = lambda n: jnp.zeros((d, n), w.dtype)
    kr = w[:, 1152:1184]
    swq = w[:, 1184:1568].reshape(d, SW_HEADS, HEAD_DIM)
    order = [0, 3, 1, 4, 2, 5]
    swq = swq[:, order, :].reshape(d, SW_HEADS * HEAD_DIM)
    cols = [w[:, 0:1152],
            z(MLA_NOPE), kr, z(SLOT - MLA_QK),
            z(MLA_NOPE), kr[:, ROPE_HALF:], kr[:, :ROPE_HALF], z(SLOT - MLA_QK),
            w[:, 1568:1696], swq, w[:, 1696:1824]]
    return jnp.concatenate(cols, axis=1).astype(BF16)


def _layout_w_uq(w):
    r = w.shape[0]
    w = w.reshape(r, MLA_HEADS, MLA_QK)
    nope, x1, x2 = w[..., :MLA_NOPE], w[..., MLA_NOPE:MLA_NOPE + ROPE_HALF], w[..., MLA_NOPE + ROPE_HALF:]
    z = jnp.zeros((r, MLA_HEADS, SLOT - MLA_QK), w.dtype)
    plain = jnp.concatenate([nope, x1, x2, z], axis=-1).reshape(r, MLA_HEADS * SLOT)
    swapped = jnp.concatenate([jnp.zeros_like(nope), x2, x1, z], axis=-1).reshape(r, MLA_HEADS * SLOT)
    return jnp.concatenate([plain, swapped], axis=1).astype(BF16)


def _layout_w_ukv(w):
    r = w.shape[0]
    w = w.reshape(r, MLA_HEADS, MLA_NOPE + MLA_V)
    k_nope, v = w[..., :MLA_NOPE], w[..., MLA_NOPE:]
    k_slots = jnp.concatenate([k_nope, jnp.zeros((r, MLA_HEADS, SLOT - MLA_NOPE), w.dtype)], axis=-1)
    return jnp.concatenate([k_slots.reshape(r, MLA_HEADS * SLOT), v.reshape(r, MLA_HEADS * MLA_V)],
                           axis=1).astype(BF16)


def _layout_w_out(w):
    n_ab = SB_HEADS * HEAD_DIM + MLA_HEADS * MLA_V
    sw = w[n_ab:].reshape(SW_HEADS, HEAD_DIM, w.shape[1])[jnp.array([0, 3, 1, 4, 2, 5])]
    return jnp.concatenate([w[:n_ab], sw.reshape(SW_HEADS * HEAD_DIM, w.shape[1])], axis=0).astype(BF16)


def kernel(x, c, positions, rel_table, norm1_g, norm2_g, w_ada, b_ada, w_in, mla_cq_g, w_uq, mla_ckv_g,
           w_ukv, mla_qn_g, mla_kn_g, sw_qn_g, sw_kn_g, sw_sinks, w_out, w_up, conv_w, conv_b, w_down):
    depth = w_in.shape[0]
    b = x.shape[0]
    mods = _mods(c, w_ada, b_ada).reshape(depth, b, 6, D_MODEL)
    cos_t, sin_t = _rope_tables(positions)
    bias = _window_bias(rel_table)
    row = lambda v: v.reshape(1, -1).astype(F32)
    two = lambda v: jnp.concatenate([v, v]).reshape(1, SLOT).astype(F32)
    for l in range(depth):
        qkv = _prep(x, mods[l], row(norm1_g[l]), _layout_w_in(w_in[l]), row(mla_cq_g[l]),
                    _layout_w_uq(w_uq[l]), row(mla_ckv_g[l]), _layout_w_ukv(w_ukv[l]),
                    _slot_gain(mla_qn_g[l], False), _slot_gain(mla_qn_g[l], True),
                    _slot_gain(mla_kn_g[l], False), _slot_gain(mla_kn_g[l], True),
                    two(sw_qn_g[l]), two(sw_kn_g[l]), cos_t, sin_t)
        sbq, sbk, sbvt, mq, mk, mvt, swq, swk, swvt = qkv
        o_a, o_b, o_c = _attention(sw_sinks[l], sbq, sbk, sbvt, mq, mk, mvt, swq, swk, swvt, bias)
        x = _mlp(x, mods[l], o_a, o_b, o_c, _layout_w_out(w_out[l]), row(norm2_g[l]),
                 w_up[l].astype(BF16), conv_w[l], row(conv_b[l]), w_down[l].astype(BF16))
    return x
```

```python
import functools
import math

import numpy as np
import jax
import jax.numpy as jnp
from jax import lax
from jax.experimental import pallas as pl
from jax.experimental.pallas import tpu as pltpu

F32 = jnp.float32
BF16 = jnp.bfloat16

D_MODEL = 1024
HEAD_DIM = 64
SB_HEADS = 4
MLA_HEADS = 6
MLA_Q_RANK = 256
MLA_KV_RANK = 128
MLA_NOPE = 64
MLA_ROPE = 32
MLA_V = 64
MLA_QK = MLA_NOPE + MLA_ROPE
ROPE_THETA = 10000.0
SW_HEADS = 6
SW_KV_HEADS = 2
WINDOW = 128
REL_BUCKETS = 32
REL_MAX_DIST = 128
D_FF = 2816
CONV_W = 3
EPS = 1e-6
NEG = -1e30

LANES = 128
SLOT = LANES
HALF = SLOT // 2

C_SBQ, C_SBK, C_SBV = 0, 256, 512
C_CQ = 768
C_CKV = 1024
C_KROPE = 1152
C_KROPE_SW = 1280
C_SWK = 1408
C_SWQ = 1536
C_SWV = 1920
N_IN = 2048

ROPE_LO = MLA_NOPE
ROPE_HALF = MLA_ROPE // 2

TM_PREP = 512
TM_MLP = 512
TQ = 512
TK = 256
TQ_SW = 256
LOG2E = math.log2(math.e)
FF_CHUNK = 256
CARRY_ROWS = 8
UP_AHEAD = 2

VMEM_LIMIT = 56 * 1024 * 1024


def _cparams(n_axes):
    return pltpu.CompilerParams(dimension_semantics=("arbitrary",) * n_axes,
                                vmem_limit_bytes=VMEM_LIMIT)


def _rms(x, n):
    return x * lax.rsqrt(jnp.sum(x * x, axis=-1, keepdims=True) * (1.0 / n) + EPS)


def _nt_dot(a, b):
    return lax.dot_general(a, b, (((1,), (1,)), ((), ())), preferred_element_type=F32)


def _dot(a, b):
    return jnp.dot(a, b, preferred_element_type=F32)


def _mods_kernel(c_ref, w_ref, b_ref, o_ref):
    c = c_ref[...]
    a = (c * jax.nn.sigmoid(c)).astype(BF16)
    o_ref[0] = _dot(a, w_ref[0].astype(BF16)) + b_ref[0]


def _mods(c, w_ada, b_ada):
    depth, d, n = w_ada.shape
    b = c.shape[0]
    tn = 1536
    return pl.pallas_call(
        _mods_kernel,
        out_shape=jax.ShapeDtypeStruct((depth, b, n), F32),
        grid=(depth, n // tn),
        in_specs=[pl.BlockSpec((b, d), lambda l, j: (0, 0)),
                  pl.BlockSpec((1, d, tn), lambda l, j: (l, 0, j)),
                  pl.BlockSpec((1, 1, tn), lambda l, j: (l, 0, j))],
        out_specs=pl.BlockSpec((1, b, tn), lambda l, j: (l, 0, j)),
        compiler_params=_cparams(2),
        name="adaln_mods",
    )(c, w_ada, b_ada.reshape(depth, 1, n))


def _rope_kernel(pos_ref, invf_ref, cos_ref, sin_ref):
    pos = pos_ref[0].astype(F32)
    ang = invf_ref[...] * pos
    c = jnp.cos(ang)
    s = jnp.sin(ang)
    tm = pos.shape[1]
    ones = jnp.ones((ROPE_LO, tm), F32)
    zeros = jnp.zeros((ROPE_LO, tm), F32)
    pad = SLOT - ROPE_LO - MLA_ROPE
    cos_t = jnp.concatenate([ones, c, c, jnp.ones((pad, tm), F32)], axis=0)
    sin_t = jnp.concatenate([zeros, -s, s, jnp.zeros((pad, tm), F32)], axis=0)
    cos_ref[0] = cos_t.T
    sin_ref[0] = sin_t.T


def _rope_tables(positions):
    b, s = positions.shape
    tm = s
    half = ROPE_HALF
    inv_freq = jnp.power(ROPE_THETA, -jnp.arange(half, dtype=F32) / half).reshape(half, 1)
    out = jax.ShapeDtypeStruct((b, s, SLOT), F32)
    return pl.pallas_call(
        _rope_kernel,
        out_shape=(out, out),
        grid=(b, s // tm),
        in_specs=[pl.BlockSpec((1, 1, tm), lambda i, j: (i, 0, j)),
                  pl.BlockSpec((half, 1), lambda i, j: (0, 0))],
        out_specs=(pl.BlockSpec((1, tm, SLOT), lambda i, j: (i, j, 0)),
                   pl.BlockSpec((1, tm, SLOT), lambda i, j: (i, j, 0))),
        compiler_params=_cparams(2),
        name="rope_tables",
    )(positions.reshape(b, 1, s), inv_freq)


def _t5_bucket(dist):
    max_exact = REL_BUCKETS // 2
    n = jnp.maximum(dist, 0)
    nf = jnp.maximum(n, 1).astype(F32)
    large = max_exact + (jnp.log(nf / max_exact) / math.log(REL_MAX_DIST / max_exact)
                         * (REL_BUCKETS - max_exact)).astype(jnp.int32)
    large = jnp.minimum(large, REL_BUCKETS - 1)
    return jnp.where(n < max_exact, n, large)


def _bias_kernel(tab_ref, bucket_ref, o_ref):
    bucket = bucket_ref[...]
    for h in range(SW_HEADS):
        acc = jnp.zeros(bucket.shape, F32)
        for bkt in range(REL_BUCKETS):
            acc = jnp.where(bucket == bkt, tab_ref[bkt, h], acc)
        o_ref[h] = jnp.where(bucket >= 0, acc * LOG2E, NEG)


def _window_bias(rel_table):
    nk = WINDOW + TQ_SW
    key = jnp.arange(nk)[:, None]
    qry = jnp.arange(TQ_SW)[None, :]
    dist = qry + WINDOW - key
    valid = (dist >= 0) & (dist < WINDOW)
    bucket = jnp.where(valid, _t5_bucket(dist), -1).astype(jnp.int32)
    return pl.pallas_call(
        _bias_kernel,
        out_shape=jax.ShapeDtypeStruct((SW_HEADS, nk, TQ_SW), F32),
        in_specs=[pl.BlockSpec(memory_space=pltpu.SMEM),
                  pl.BlockSpec(memory_space=pltpu.VMEM)],
        out_specs=pl.BlockSpec(memory_space=pltpu.VMEM),
        name="window_bias",
    )(rel_table, bucket)


def _half_rms(x, gain, scale):
    lo = lax.broadcasted_iota(jnp.int32, x.shape, 1) < HALF
    sq = x * x
    s_lo = jnp.sum(jnp.where(lo, sq, 0.0), axis=-1, keepdims=True)
    s_hi = jnp.sum(jnp.where(lo, 0.0, sq), axis=-1, keepdims=True)
    r = jnp.where(lo, lax.rsqrt(s_lo * (1.0 / HEAD_DIM) + EPS), lax.rsqrt(s_hi * (1.0 / HEAD_DIM) + EPS))
    return x * r * (gain * scale)


def _prep_kernel(x_ref, mod_ref, n1g_ref, win_ref, cqg_ref, wuq_ref, ckvg_ref, wukv_ref,
                 gq_ref, gqs_ref, gk_ref, gks_ref, swqg_ref, swkg_ref, cos_ref, sin_ref,
                 sbq_ref, sbk_ref, sbvt_ref, mq_ref, mk_ref, mvt_ref, swq_ref, swk_ref, swvt_ref):
    x = x_ref[0]
    mod = mod_ref[0]
    shift1, scale1 = mod[0:1], mod[1:2]
    h = (_rms(x, D_MODEL) * n1g_ref[...] * (1.0 + scale1) + shift1).astype(BF16)

    proj_mla = _dot(h, win_ref[:, C_CQ:C_SWQ])
    proj_sb = _dot(h, win_ref[:, C_SBQ:C_CQ])
    cq = proj_mla[:, 0:MLA_Q_RANK]
    ckv = proj_mla[:, C_CKV - C_CQ:C_CKV - C_CQ + MLA_KV_RANK]
    krope = proj_mla[:, C_KROPE - C_CQ:C_KROPE - C_CQ + SLOT]
    krope_sw = proj_mla[:, C_KROPE_SW - C_CQ:C_KROPE_SW - C_CQ + SLOT]
    cqn = (_rms(cq, MLA_Q_RANK) * cqg_ref[...]).astype(BF16)
    ckvn = (_rms(ckv, MLA_KV_RANK) * ckvg_ref[...]).astype(BF16)
    qraw = _dot(cqn, wuq_ref[...])
    kv = _dot(ckvn, wukv_ref[...])
    proj_sw = _dot(h, win_ref[:, C_SWQ:N_IN])

    sbq_ref[0] = (proj_sb[:, C_SBQ:C_SBQ + 256] * (HEAD_DIM ** -0.5 * LOG2E)).astype(BF16)
    sbk_ref[0] = proj_sb[:, C_SBK:C_SBK + 256].astype(BF16)
    sbvt_ref[0] = proj_sb[:, C_SBV:C_SBV + 256].T.astype(BF16)

    cos = cos_ref[0]
    sin = sin_ref[0]

    nq = MLA_HEADS * SLOT
    q_scale = MLA_QK ** -0.5 * LOG2E
    q_cos, q_sin = gq_ref[...] * cos, gqs_ref[...] * sin
    for hd in range(MLA_HEADS):
        slot = qraw[:, hd * SLOT:(hd + 1) * SLOT]
        swapped = qraw[:, nq + hd * SLOT:nq + (hd + 1) * SLOT]
        r = lax.rsqrt(jnp.sum(slot * slot, axis=-1, keepdims=True) * (1.0 / MLA_QK) + EPS) * q_scale
        mq_ref[0, :, hd * SLOT:(hd + 1) * SLOT] = ((slot * q_cos + swapped * q_sin) * r).astype(BF16)

    mvt_ref[0] = kv[:, nq:nq + MLA_HEADS * MLA_V].T.astype(BF16)
    k_cos = gk_ref[...] * cos
    k_rot = krope_sw * (gks_ref[...] * sin)
    for hd in range(MLA_HEADS):
        slot = kv[:, hd * SLOT:(hd + 1) * SLOT] + krope
        r = lax.rsqrt(jnp.sum(slot * slot, axis=-1, keepdims=True) * (1.0 / MLA_QK) + EPS)
        mk_ref[0, :, hd * SLOT:(hd + 1) * SLOT] = ((slot * k_cos + k_rot) * r).astype(BF16)

    for g in range(SW_HEADS // 2):
        xq = proj_sw[:, g * SLOT:(g + 1) * SLOT]
        swq_ref[0, :, g * SLOT:(g + 1) * SLOT] = _half_rms(
            xq, swqg_ref[...], HEAD_DIM ** -0.5 * LOG2E).astype(BF16)
    swk_ref[0] = _half_rms(proj_mla[:, C_SWK - C_CQ:C_SWK - C_CQ + SLOT], swkg_ref[...], 1.0).astype(BF16)
    swvt_ref[0] = proj_sw[:, C_SWV - C_SWQ:C_SWV - C_SWQ + SLOT].T.astype(BF16)


def _prep(x, mods, n1g, w_in, cqg, w_uq, ckvg, w_ukv, gq, gqs, gk, gks, swqg, swkg, cos_t, sin_t):
    b, s, d = x.shape
    tm = TM_PREP
    row = lambda w: pl.BlockSpec((1, tm, w), lambda i, j: (i, j, 0))
    colt = lambda w: pl.BlockSpec((1, w, tm), lambda i, j: (i, 0, j))
    full = lambda a: pl.BlockSpec(a.shape, lambda i, j: (0,) * a.ndim)
    act = lambda w: jax.ShapeDtypeStruct((b, s, w), BF16)
    actt = lambda w: jax.ShapeDtypeStruct((b, w, s), BF16)
    return pl.pallas_call(
        _prep_kernel,
        out_shape=(act(256), act(256), actt(256), act(768), act(768), actt(384),
                   act(384), act(128), actt(128)),
        grid=(b, s // tm),
        in_specs=[row(d), pl.BlockSpec((1, 6, d), lambda i, j: (i, 0, 0)), full(n1g), full(w_in),
                  full(cqg), full(w_uq), full(ckvg), full(w_ukv), full(gq), full(gqs), full(gk),
                  full(gks), full(swqg), full(swkg), row(SLOT), row(SLOT)],
        out_specs=(row(256), row(256), colt(256), row(768), row(768), colt(384),
                   row(384), row(128), colt(128)),
        compiler_params=_cparams(2),
        name="prep_qkv",
    )(x, mods, n1g, w_in, cqg, w_uq, ckvg, w_ukv, gq, gqs, gk, gks, swqg, swkg, cos_t, sin_t)


def _half_mask(q, half):
    lane = lax.broadcasted_iota(jnp.int32, q.shape, 1)
    keep = (lane < HALF) if half == 0 else (lane >= HALF)
    return jnp.where(keep, q, jnp.zeros_like(q))


def _store_pair(o_ref, g, out_lo, out_hi, row0=0):
    pair = jnp.concatenate([out_lo, out_hi], axis=0)
    o_ref[0, row0:row0 + pair.shape[1], g * SLOT:(g + 1) * SLOT] = pair.T.astype(o_ref.dtype)


SIGN_BIT = -2 ** 31
JOBS = (("a", 0), ("b", 0), ("b", 1), ("a", 1), ("b", 2), ("b", 3), ("a", 2), ("b", 4), ("b", 5), ("a", 3))


def _softplus2(z):
    neg_abs = lax.bitcast_convert_type(lax.bitcast_convert_type(z, jnp.int32) | SIGN_BIT, F32)
    return jnp.maximum(z, 0.0) + jnp.log(1.0 + jnp.exp2(neg_abs)) * LOG2E


def _causal_kernel(qb, sink_ref, aq_ref, ak_ref, avt_ref, tri_ref, bq_ref, bk_ref, bvt_ref,
                   cq_ref, ck_ref, cvt_ref, bias_ref, pa_ref, pb_ref, pc_ref, oa_ref, ob_ref, oc_ref):
    del pa_ref, pb_ref, pc_ref
    n_diag = TQ // TK
    a_heads, b_heads = range(SB_HEADS), range(MLA_HEADS)

    def put(full, part, c0):
        return part if c0 == 0 else jnp.concatenate([full[:, :c0], part], axis=1)

    def run_jobs(carry, jobs):
        ca, cb = list(carry[0]), list(carry[1])
        st = [None] * len(jobs)
        out_c = {}

        def geometry(t):
            mixer, hd, start, diag = jobs[t]
            c0 = 0 if diag is None else diag * TK
            row = lax.broadcasted_iota(jnp.int32, (TK, TQ - c0), 0)
            col = lax.broadcasted_iota(jnp.int32, (TK, TQ - c0), 1)
            return mixer, hd, start, diag is not None, c0, row, col

        def window(t):
            _, slot, sub = jobs[t]
            head = slot // 2 + (slot % 2) * (SW_HEADS // SW_KV_HEADS)
            key0 = qb * TQ + sub * TQ_SW - WINDOW
            lo = max(key0, 0)
            return slot, sub, head, lo, key0 + WINDOW + TQ_SW, lo - key0

        def scores(t):
            if jobs[t][0] == "c":
                slot, sub, head, lo, hi, skip = window(t)
                q = _half_mask(cq_ref[0, sub * TQ_SW:(sub + 1) * TQ_SW, (slot // 2) * SLOT:(slot // 2 + 1) * SLOT],
                               slot % 2)
                st[t] = _nt_dot(ck_ref[0, lo:hi, :], q) + bias_ref[head, skip:, :]
                return
            mixer, hd, start, masked, c0, row, col = geometry(t)
            if mixer == "a":
                g, half = divmod(hd, 2)
                q = _half_mask(aq_ref[0, c0:, g * SLOT:(g + 1) * SLOT], half)
                st[t] = _nt_dot(ak_ref[0, pl.ds(start, TK), g * SLOT:(g + 1) * SLOT], q)
            else:
                sc = _nt_dot(bk_ref[0, pl.ds(start, TK), hd * SLOT:(hd + 1) * SLOT],
                             bq_ref[0, c0:, hd * SLOT:(hd + 1) * SLOT])
                st[t] = jnp.where(row <= col, sc, NEG) if masked else sc

        def second(t):
            if jobs[t][0] == "c":
                slot, sub, head, lo, hi, skip = window(t)
                sink = sink_ref[head] * LOG2E
                m = jnp.maximum(jnp.max(st[t], axis=0, keepdims=True), sink)
                p = jnp.exp2(st[t] - m)
                denom = jnp.sum(p, axis=0, keepdims=True) + jnp.exp2(sink - m)
                vt = cvt_ref[0, (slot % 2) * HEAD_DIM:(slot % 2 + 1) * HEAD_DIM, lo:hi]
                out_c[(sub, slot)] = _dot(vt, p.astype(BF16)) * (1.0 / denom)
                st[t] = None
                return
            mixer, hd, start, masked, c0, row, col = geometry(t)
            if mixer == "a":
                drop = _softplus2(st[t])
                if masked:
                    drop = jnp.where(row < col, drop, 0.0)
                st[t] = (st[t], _dot(tri_ref[...], drop.astype(BF16)))
            else:
                m_all, l_all, acc_all = cb[hd]
                m_old = m_all[:, c0:]
                m_new = jnp.maximum(m_old, jnp.max(st[t], axis=0, keepdims=True))
                alpha = jnp.exp2(m_old - m_new)
                p = jnp.exp2(st[t] - m_new)
                l_new = alpha * l_all[:, c0:] + jnp.sum(p, axis=0, keepdims=True)
                pv = _dot(bvt_ref[0, hd * MLA_V:(hd + 1) * MLA_V, pl.ds(start, TK)], p.astype(BF16))
                cb[hd] = (put(m_all, m_new, c0), put(l_all, l_new, c0),
                          put(acc_all, alpha * acc_all[:, c0:] + pv, c0))

        def third(t):
            if jobs[t][0] == "c":
                return
            mixer, hd, start, masked, c0, row, col = geometry(t)
            if mixer == "a":
                z, incl = st[t]
                acc, run = ca[hd]
                w = jnp.exp2(z + incl + run[:, c0:])
                if masked:
                    w = jnp.where(row < col, w, 0.0)
                pv = _dot(avt_ref[0, hd * HEAD_DIM:(hd + 1) * HEAD_DIM, pl.ds(start, TK)], w.astype(BF16))
                ca[hd] = (put(acc, acc[:, c0:] + pv, c0), put(run, run[:, c0:] + incl[0:1, :], c0))
            st[t] = None

        n = len(jobs)
        for t in range(n + 2):
            if t < n:
                scores(t)
            if 0 <= t - 1 < n:
                second(t - 1)
            if 0 <= t - 2 < n:
                third(t - 2)
        return (tuple(ca), tuple(cb)), out_c

    carry = (tuple((jnp.zeros((HEAD_DIM, TQ), F32), jnp.zeros((1, TQ), F32)) for _ in a_heads),
             tuple((jnp.full((1, TQ), NEG, F32), jnp.zeros((1, TQ), F32), jnp.zeros((MLA_V, TQ), F32))
                   for _ in b_heads))
    n_past = qb * n_diag
    past = [(j * TK, None) for j in reversed(range(n_past))]
    a_blocks = [((n_past + d) * TK, d) for d in reversed(range(n_diag))] + past
    b_blocks = [((n_past + d) * TK, d) for d in range(n_diag)] + past
    sweep = [(mixer, hd) + (a_blocks[i] if mixer == "a" else b_blocks[i])
             for i in range(len(a_blocks)) for (mixer, hd) in JOBS]
    local = [("c", slot, sub) for sub in range(TQ // TQ_SW) for slot in range(SW_HEADS)]
    jobs = []
    for job in sweep:
        jobs.append(job)
        if local:
            jobs.append(local.pop(0))
    (ca, cb), out_c = run_jobs(carry, jobs)
    for sub in range(TQ // TQ_SW):
        for g in range(SW_HEADS // 2):
            _store_pair(oc_ref, g, out_c[(sub, 2 * g)], out_c[(sub, 2 * g + 1)], sub * TQ_SW)
    for g in range(SB_HEADS // 2):
        _store_pair(oa_ref, g, ca[2 * g][0], ca[2 * g + 1][0])
    outs = [acc * (1.0 / l) for (_, l, acc) in cb]
    for g in range(MLA_HEADS // 2):
        _store_pair(ob_ref, g, outs[2 * g], outs[2 * g + 1])


def _attention(sinks, aq, ak, avt, bq, bk, bvt, cq, ck, cvt, bias):
    b, s, wa = aq.shape
    wb, wbv, wc = bq.shape[2], bvt.shape[1], cq.shape[2]
    tri = -(jnp.arange(TK)[None, :] >= jnp.arange(TK)[:, None]).astype(BF16)
    const = lambda a: pl.BlockSpec(a.shape, lambda i: (0,) * a.ndim)
    outs = tuple(jnp.zeros((b, s, w), BF16) for w in (wa, wbv, wc))
    n_in = 12
    for qb in range(s // TQ):
        nk = (qb + 1) * TQ
        qblk = lambda w, qb=qb: pl.BlockSpec((1, TQ, w), lambda i: (i, qb, 0))
        keys = lambda w, nk=nk: pl.BlockSpec((1, nk, w), lambda i: (i, 0, 0))
        keyst = lambda w, nk=nk: pl.BlockSpec((1, w, nk), lambda i: (i, 0, 0))
        outs = pl.pallas_call(
            functools.partial(_causal_kernel, qb),
            out_shape=tuple(jax.ShapeDtypeStruct(o.shape, o.dtype) for o in outs),
            grid=(b,),
            in_specs=[pl.BlockSpec(memory_space=pltpu.SMEM),
                      qblk(wa), keys(wa), keyst(wa), const(tri),
                      qblk(wb), keys(wb), keyst(wbv),
                      qblk(wc), keys(ck.shape[2]), keyst(cvt.shape[1]), const(bias)]
                     + [pl.BlockSpec(memory_space=pl.ANY)] * 3,
            out_specs=(qblk(wa), qblk(wbv), qblk(wc)),
            input_output_aliases={n_in + k: k for k in range(3)},
            compiler_params=_cparams(1),
            name=f"attention_q{qb}",
        )(sinks, aq, ak, avt, tri, bq, bk, bvt, cq, ck, cvt, bias, *outs)
    return outs


def _mlp_kernel(x_ref, mod_ref, oa_ref, ob_ref, oc_ref, wo_ref, n2g_ref, wup_ref, cw_ref, cb_ref,
                wdn_ref, out_ref, carry_ref):
    si = pl.program_id(1)
    tm = x_ref.shape[1]
    mod = mod_ref[0]
    gate1, shift2, scale2, gate2 = mod[2:3], mod[3:4], mod[4:5], mod[5:6]
    na, nb = oa_ref.shape[2], ob_ref.shape[2]
    att = (_dot(oa_ref[0], wo_ref[0:na, :]) + _dot(ob_ref[0], wo_ref[na:na + nb, :])
           + _dot(oc_ref[0], wo_ref[na + nb:, :]))
    x1 = x_ref[0] + gate1 * att
    h2 = (_rms(x1, D_MODEL) * n2g_ref[...] * (1.0 + scale2) + shift2).astype(BF16)

    @pl.when(si == 0)
    def _():
        carry_ref[...] = jnp.zeros_like(carry_ref)

    def up(c):
        lo = c * FF_CHUNK
        return (_dot(h2, wup_ref[:, lo:lo + FF_CHUNK]),
                _dot(h2, wup_ref[:, D_FF + lo:D_FF + lo + FF_CHUNK]))

    def conv(u, col):
        cols = slice(col, col + FF_CHUNK)
        prev = carry_ref[:, cols]
        carry_ref[:, cols] = u[tm - CARRY_ROWS:, :]
        ext = jnp.concatenate([prev, u], axis=0)
        u1 = ext[CARRY_ROWS - 1:CARRY_ROWS - 1 + tm, :]
        u2 = ext[CARRY_ROWS - 2:CARRY_ROWS - 2 + tm, :]
        cw = cw_ref[:, cols]
        return u * cw[2:3] + u1 * cw[1:2] + u2 * cw[0:1] + cb_ref[:, cols]

    n_chunks = D_FF // FF_CHUNK
    ahead = [up(c) for c in range(min(UP_AHEAD, n_chunks))]
    for c in range(n_chunks):
        u_gate, u_val = ahead.pop(0)
        if c + UP_AHEAD < n_chunks:
            ahead.append(up(c + UP_AHEAD))
        gate = conv(u_gate, c * FF_CHUNK)
        val = conv(u_val, D_FF + c * FF_CHUNK)
        a = (gate * jax.nn.sigmoid(gate) * val).astype(BF16)
        part = _dot(a, wdn_ref[c * FF_CHUNK:(c + 1) * FF_CHUNK, :])
        acc = part if c == 0 else acc + part
    out_ref[0] = x1 + gate2 * acc


def _mlp(x, mods, oa, ob, oc, w_out, n2g, w_up, conv_w, conv_b, w_down):
    b, s, d = x.shape
    tm = TM_MLP
    row = lambda w: pl.BlockSpec((1, tm, w), lambda i, j: (i, j, 0))
    const = lambda a: pl.BlockSpec(a.shape, lambda i, j: (0,) * a.ndim, pipeline_mode=pl.Buffered(1))
    return pl.pallas_call(
        _mlp_kernel,
        out_shape=jax.ShapeDtypeStruct((b, s, d), F32),
        grid=(b, s // tm),
        in_specs=[row(d), pl.BlockSpec((1, 6, d), lambda i, j: (i, 0, 0)),
                  row(oa.shape[2]), row(ob.shape[2]), row(oc.shape[2]),
                  const(w_out), const(n2g), const(w_up), const(conv_w), const(conv_b), const(w_down)],
        out_specs=row(d),
        scratch_shapes=[pltpu.VMEM((CARRY_ROWS, 2 * D_FF), F32)],
        compiler_params=_cparams(2),
        name="outproj_mlp",
    )(x, mods, oa, ob, oc, w_out, n2g, w_up, conv_w, conv_b, w_down)


def _slot_gain(g, swap):
    z = jnp.zeros((SLOT - MLA_QK,), F32)
    lo, hi = g[ROPE_LO:ROPE_LO + ROPE_HALF], g[ROPE_LO + ROPE_HALF:MLA_QK]
    if swap:
        return jnp.concatenate([jnp.zeros((MLA_NOPE,), F32), hi, lo, z]).reshape(1, SLOT)
    return jnp.concatenate([g, z]).reshape(1, SLOT)


def _layout_w_in(w):
    d = w.shape[0]
    z = lambda n: jnp.zeros((d, n), w.dtype)
    kr = w[:, 1152:1184]
    swq = w[:, 1184:1568].reshape(d, SW_HEADS, HEAD_DIM)
    order = [0, 3, 1, 4, 2, 5]
    swq = swq[:, order, :].reshape(d, SW_HEADS * HEAD_DIM)
    cols = [w[:, 0:1152],
            z(MLA_NOPE), kr, z(SLOT - MLA_QK),
            z(MLA_NOPE), kr[:, ROPE_HALF:], kr[:, :ROPE_HALF], z(SLOT - MLA_QK),
            w[:, 1568:1696], swq, w[:, 1696:1824]]
    return jnp.concatenate(cols, axis=1).astype(BF16)


def _layout_w_uq(w):
    r = w.shape[0]
    w = w.reshape(r, MLA_HEADS, MLA_QK)
    nope, x1, x2 = w[..., :MLA_NOPE], w[..., MLA_NOPE:MLA_NOPE + ROPE_HALF], w[..., MLA_NOPE + ROPE_HALF:]
    z = jnp.zeros((r, MLA_HEADS, SLOT - MLA_QK), w.dtype)
    plain = jnp.concatenate([nope, x1, x2, z], axis=-1).reshape(r, MLA_HEADS * SLOT)
    swapped = jnp.concatenate([jnp.zeros_like(nope), x2, x1, z], axis=-1).reshape(r, MLA_HEADS * SLOT)
    return jnp.concatenate([plain, swapped], axis=1).astype(BF16)


def _layout_w_ukv(w):
    r = w.shape[0]
    w = w.reshape(r, MLA_HEADS, MLA_NOPE + MLA_V)
    k_nope, v = w[..., :MLA_NOPE], w[..., MLA_NOPE:]
    k_slots = jnp.concatenate([k_nope, jnp.zeros((r, MLA_HEADS, SLOT - MLA_NOPE), w.dtype)], axis=-1)
    return jnp.concatenate([k_slots.reshape(r, MLA_HEADS * SLOT), v.reshape(r, MLA_HEADS * MLA_V)],
                           axis=1).astype(BF16)


def _layout_w_out(w):
    n_ab = SB_HEADS * HEAD_DIM + MLA_HEADS * MLA_V
    sw = w[n_ab:].reshape(SW_HEADS, HEAD_DIM, w.shape[1])[jnp.array([0, 3, 1, 4, 2, 5])]
    return jnp.concatenate([w[:n_ab], sw.reshape(SW_HEADS * HEAD_DIM, w.shape[1])], axis=0).astype(BF16)


def kernel(x, c, positions, rel_table, norm1_g, norm2_g, w_ada, b_ada, w_in, mla_cq_g, w_uq, mla_ckv_g,
           w_ukv, mla_qn_g, mla_kn_g, sw_qn_g, sw_kn_g, sw_sinks, w_out, w_up, conv_w, conv_b, w_down):
    depth = w_in.shape[0]
    b = x.shape[0]
    mods = _mods(c, w_ada, b_ada).reshape(depth, b, 6, D_MODEL)
    cos_t, sin_t = _rope_tables(positions)
    bias = _window_bias(rel_table)
    row = lambda v: v.reshape(1, -1).astype(F32)
    two = lambda v: jnp.concatenate([v, v]).reshape(1, SLOT).astype(F32)
    for l in range(depth):
        qkv = _prep(x, mods[l], row(norm1_g[l]), _layout_w_in(w_in[l]), row(mla_cq_g[l]),
                    _layout_w_uq(w_uq[l]), row(mla_ckv_g[l]), _layout_w_ukv(w_ukv[l]),
                    _slot_gain(mla_qn_g[l], False), _slot_gain(mla_qn_g[l], True),
                    _slot_gain(mla_kn_g[l], False), _slot_gain(mla_kn_g[l], True),
                    two(sw_qn_g[l]), two(sw_kn_g[l]), cos_t, sin_t)
        sbq, sbk, sbvt, mq, mk, mvt, swq, swk, swvt = qkv
        o_a, o_b, o_c = _attention(sw_sinks[l], sbq, sbk, sbvt, mq, mk, mvt, swq, swk, swvt, bias)
        x = _mlp(x, mods[l], o_a, o_b, o_c, _layout_w_out(w_out[l]), row(norm2_g[l]),
                 w_up[l].astype(BF16), conv_w[l], row(conv_b[l]), w_down[l].astype(BF16))
    return x
```

```python
import functools
import math

import numpy as np
import jax
import jax.numpy as jnp
from jax import lax
from jax.experimental import pallas as pl
from jax.experimental.pallas import tpu as pltpu

F32 = jnp.float32
BF16 = jnp.bfloat16

D_MODEL = 1024
HEAD_DIM = 64
SB_HEADS = 4
MLA_HEADS = 6
MLA_Q_RANK = 256
MLA_KV_RANK = 128
MLA_NOPE = 64
MLA_ROPE = 32
MLA_V = 64
MLA_QK = MLA_NOPE + MLA_ROPE
ROPE_THETA = 10000.0
SW_HEADS = 6
SW_KV_HEADS = 2
WINDOW = 128
REL_BUCKETS = 32
REL_MAX_DIST = 128
D_FF = 2816
CONV_W = 3
EPS = 1e-6
NEG = -1e30

LANES = 128
SLOT = LANES
HALF = SLOT // 2

C_SBQ, C_SBK, C_SBV = 0, 256, 512
C_CQ = 768
C_CKV = 1024
C_KROPE = 1152
C_KROPE_SW = 1280
C_SWK = 1408
C_SWQ = 1536
C_SWV = 1920
N_IN = 2048

ROPE_LO = MLA_NOPE
ROPE_HALF = MLA_ROPE // 2

TM_PREP = 512
TM_MLP = 512
TQ = 512
TK = 256
TQ_SW = 256
LOG2E = math.log2(math.e)
FF_CHUNK = 256
CARRY_ROWS = 8
UP_AHEAD = 2

VMEM_LIMIT = 56 * 1024 * 1024


def _cparams(n_axes):
    return pltpu.CompilerParams(dimension_semantics=("arbitrary",) * n_axes,
                                vmem_limit_bytes=VMEM_LIMIT)


def _rms(x, n):
    return x * lax.rsqrt(jnp.sum(x * x, axis=-1, keepdims=True) * (1.0 / n) + EPS)


def _nt_dot(a, b):
    return lax.dot_general(a, b, (((1,), (1,)), ((), ())), preferred_element_type=F32)


def _dot(a, b):
    return jnp.dot(a, b, preferred_element_type=F32)


def _mods_kernel(c_ref, w_ref, b_ref, o_ref):
    c = c_ref[...]
    a = (c * jax.nn.sigmoid(c)).astype(BF16)
    o_ref[0] = _dot(a, w_ref[0].astype(BF16)) + b_ref[0]


def _mods(c, w_ada, b_ada):
    depth, d, n = w_ada.shape
    b = c.shape[0]
    tn = 1536
    return pl.pallas_call(
        _mods_kernel,
        out_shape=jax.ShapeDtypeStruct((depth, b, n), F32),
        grid=(depth, n // tn),
        in_specs=[pl.BlockSpec((b, d), lambda l, j: (0, 0)),
                  pl.BlockSpec((1, d, tn), lambda l, j: (l, 0, j)),
                  pl.BlockSpec((1, 1, tn), lambda l, j: (l, 0, j))],
        out_specs=pl.BlockSpec((1, b, tn), lambda l, j: (l, 0, j)),
        compiler_params=_cparams(2),
        name="adaln_mods",
    )(c, w_ada, b_ada.reshape(depth, 1, n))


def _rope_kernel(pos_ref, invf_ref, cos_ref, sin_ref):
    pos = pos_ref[0].astype(F32)
    ang = invf_ref[...] * pos
    c = jnp.cos(ang)
    s = jnp.sin(ang)
    tm = pos.shape[1]
    ones = jnp.ones((ROPE_LO, tm), F32)
    zeros = jnp.zeros((ROPE_LO, tm), F32)
    pad = SLOT - ROPE_LO - MLA_ROPE
    cos_t = jnp.concatenate([ones, c, c, jnp.ones((pad, tm), F32)], axis=0)
    sin_t = jnp.concatenate([zeros, -s, s, jnp.zeros((pad, tm), F32)], axis=0)
    cos_ref[0] = cos_t.T
    sin_ref[0] = sin_t.T


def _rope_tables(positions):
    b, s = positions.shape
    tm = s
    half = ROPE_HALF
    inv_freq = jnp.power(ROPE_THETA, -jnp.arange(half, dtype=F32) / half).reshape(half, 1)
    out = jax.ShapeDtypeStruct((b, s, SLOT), F32)
    return pl.pallas_call(
        _rope_kernel,
        out_shape=(out, out),
        grid=(b, s // tm),
        in_specs=[pl.BlockSpec((1, 1, tm), lambda i, j: (i, 0, j)),
                  pl.BlockSpec((half, 1), lambda i, j: (0, 0))],
        out_specs=(pl.BlockSpec((1, tm, SLOT), lambda i, j: (i, j, 0)),
                   pl.BlockSpec((1, tm, SLOT), lambda i, j: (i, j, 0))),
        compiler_params=_cparams(2),
        name="rope_tables",
    )(positions.reshape(b, 1, s), inv_freq)


def _t5_bucket(dist):
    max_exact = REL_BUCKETS // 2
    n = jnp.maximum(dist, 0)
    nf = jnp.maximum(n, 1).astype(F32)
    large = max_exact + (jnp.log(nf / max_exact) / math.log(REL_MAX_DIST / max_exact)
                         * (REL_BUCKETS - max_exact)).astype(jnp.int32)
    large = jnp.minimum(large, REL_BUCKETS - 1)
    return jnp.where(n < max_exact, n, large)


def _bias_kernel(tab_ref, bucket_ref, o_ref):
    bucket = bucket_ref[...]
    for h in range(SW_HEADS):
        acc = jnp.zeros(bucket.shape, F32)
        for bkt in range(REL_BUCKETS):
            acc = jnp.where(bucket == bkt, tab_ref[bkt, h], acc)
        o_ref[h] = jnp.where(bucket >= 0, acc * LOG2E, NEG)


def _window_bias(rel_table):
    nk = WINDOW + TQ_SW
    key = jnp.arange(nk)[:, None]
    qry = jnp.arange(TQ_SW)[None, :]
    dist = qry + WINDOW - key
    valid = (dist >= 0) & (dist < WINDOW)
    bucket = jnp.where(valid, _t5_bucket(dist), -1).astype(jnp.int32)
    return pl.pallas_call(
        _bias_kernel,
        out_shape=jax.ShapeDtypeStruct((SW_HEADS, nk, TQ_SW), F32),
        in_specs=[pl.BlockSpec(memory_space=pltpu.SMEM),
                  pl.BlockSpec(memory_space=pltpu.VMEM)],
        out_specs=pl.BlockSpec(memory_space=pltpu.VMEM),
        name="window_bias",
    )(rel_table, bucket)


def _half_rms(x, gain, scale):
    lo = lax.broadcasted_iota(jnp.int32, x.shape, 1) < HALF
    sq = x * x
    s_lo = jnp.sum(jnp.where(lo, sq, 0.0), axis=-1, keepdims=True)
    s_hi = jnp.sum(jnp.where(lo, 0.0, sq), axis=-1, keepdims=True)
    r = jnp.where(lo, lax.rsqrt(s_lo * (1.0 / HEAD_DIM) + EPS), lax.rsqrt(s_hi * (1.0 / HEAD_DIM) + EPS))
    return x * r * (gain * scale)


def _prep_kernel(x_ref, mod_ref, n1g_ref, win_ref, cqg_ref, wuq_ref, ckvg_ref, wukv_ref,
                 gq_ref, gqs_ref, gk_ref, gks_ref, swqg_ref, swkg_ref, cos_ref, sin_ref,
                 sbq_ref, sbk_ref, sbvt_ref, mq_ref, mk_ref, mvt_ref, swq_ref, swk_ref, swvt_ref):
    x = x_ref[0]
    mod = mod_ref[0]
    shift1, scale1 = mod[0:1], mod[1:2]
    h = (_rms(x, D_MODEL) * n1g_ref[...] * (1.0 + scale1) + shift1).astype(BF16)

    proj_mla = _dot(h, win_ref[:, C_CQ:C_SWQ])
    proj_sb = _dot(h, win_ref[:, C_SBQ:C_CQ])
    cq = proj_mla[:, 0:MLA_Q_RANK]
    ckv = proj_mla[:, C_CKV - C_CQ:C_CKV - C_CQ + MLA_KV_RANK]
    krope = proj_mla[:, C_KROPE - C_CQ:C_KROPE - C_CQ + SLOT]
    krope_sw = proj_mla[:, C_KROPE_SW - C_CQ:C_KROPE_SW - C_CQ + SLOT]
    cqn = (_rms(cq, MLA_Q_RANK) * cqg_ref[...]).astype(BF16)
    ckvn = (_rms(ckv, MLA_KV_RANK) * ckvg_ref[...]).astype(BF16)
    qraw = _dot(cqn, wuq_ref[...])
    kv = _dot(ckvn, wukv_ref[...])
    proj_sw = _dot(h, win_ref[:, C_SWQ:N_IN])

    sbq_ref[0] = (proj_sb[:, C_SBQ:C_SBQ + 256] * (HEAD_DIM ** -0.5 * LOG2E)).astype(BF16)
    sbk_ref[0] = proj_sb[:, C_SBK:C_SBK + 256].astype(BF16)
    sbvt_ref[0] = proj_sb[:, C_SBV:C_SBV + 256].T.astype(BF16)

    cos = cos_ref[0]
    sin = sin_ref[0]

    nq = MLA_HEADS * SLOT
    q_scale = MLA_QK ** -0.5 * LOG2E
    q_cos, q_sin = gq_ref[...] * cos, gqs_ref[...] * sin
    for hd in range(MLA_HEADS):
        slot = qraw[:, hd * SLOT:(hd + 1) * SLOT]
        swapped = qraw[:, nq + hd * SLOT:nq + (hd + 1) * SLOT]
        r = lax.rsqrt(jnp.sum(slot * slot, axis=-1, keepdims=True) * (1.0 / MLA_QK) + EPS) * q_scale
        mq_ref[0, :, hd * SLOT:(hd + 1) * SLOT] = ((slot * q_cos + swapped * q_sin) * r).astype(BF16)

    mvt_ref[0] = kv[:, nq:nq + MLA_HEADS * MLA_V].T.astype(BF16)
    k_cos = gk_ref[...] * cos
    k_rot = krope_sw * (gks_ref[...] * sin)
    for hd in range(MLA_HEADS):
        slot = kv[:, hd * SLOT:(hd + 1) * SLOT] + krope
        r = lax.rsqrt(jnp.sum(slot * slot, axis=-1, keepdims=True) * (1.0 / MLA_QK) + EPS)
        mk_ref[0, :, hd * SLOT:(hd + 1) * SLOT] = ((slot * k_cos + k_rot) * r).astype(BF16)

    for g in range(SW_HEADS // 2):
        xq = proj_sw[:, g * SLOT:(g + 1) * SLOT]
        swq_ref[0, :, g * SLOT:(g + 1) * SLOT] = _half_rms(
            xq, swqg_ref[...], HEAD_DIM ** -0.5 * LOG2E).astype(BF16)
    swk_ref[0] = _half_rms(proj_mla[:, C_SWK - C_CQ:C_SWK - C_CQ + SLOT], swkg_ref[...], 1.0).astype(BF16)
    swvt_ref[0] = proj_sw[:, C_SWV - C_SWQ:C_SWV - C_SWQ + SLOT].T.astype(BF16)


def _prep(x, mods, n1g, w_in, cqg, w_uq, ckvg, w_ukv, gq, gqs, gk, gks, swqg, swkg, cos_t, sin_t):
    b, s, d = x.shape
    tm = TM_PREP
    row = lambda w: pl.BlockSpec((1, tm, w), lambda i, j: (i, j, 0))
    colt = lambda w: pl.BlockSpec((1, w, tm), lambda i, j: (i, 0, j))
    full = lambda a: pl.BlockSpec(a.shape, lambda i, j: (0,) * a.ndim)
    act = lambda w: jax.ShapeDtypeStruct((b, s, w), BF16)
    actt = lambda w: jax.ShapeDtypeStruct((b, w, s), BF16)
    return pl.pallas_call(
        _prep_kernel,
        out_shape=(act(256), act(256), actt(256), act(768), act(768), actt(384),
                   act(384), act(128), actt(128)),
        grid=(b, s // tm),
        in_specs=[row(d), pl.BlockSpec((1, 6, d), lambda i, j: (i, 0, 0)), full(n1g), full(w_in),
                  full(cqg), full(w_uq), full(ckvg), full(w_ukv), full(gq), full(gqs), full(gk),
                  full(gks), full(swqg), full(swkg), row(SLOT), row(SLOT)],
        out_specs=(row(256), row(256), colt(256), row(768), row(768), colt(384),
                   row(384), row(128), colt(128)),
        compiler_params=_cparams(2),
        name="prep_qkv",
    )(x, mods, n1g, w_in, cqg, w_uq, ckvg, w_ukv, gq, gqs, gk, gks, swqg, swkg, cos_t, sin_t)


def _half_mask(q, half):
    lane = lax.broadcasted_iota(jnp.int32, q.shape, 1)
    keep = (lane < HALF) if half == 0 else (lane >= HALF)
    return jnp.where(keep, q, jnp.zeros_like(q))


def _store_pair(o_ref, g, out_lo, out_hi, row0=0):
    pair = jnp.concatenate([out_lo, out_hi], axis=0)
    o_ref[0, row0:row0 + pair.shape[1], g * SLOT:(g + 1) * SLOT] = pair.T.astype(o_ref.dtype)


SIGN_BIT = -2 ** 31
JOBS = (("a", 0), ("b", 0), ("b", 1), ("a", 1), ("b", 2), ("b", 3), ("a", 2), ("b", 4), ("b", 5), ("a", 3))


def _softplus2(z):
    neg_abs = lax.bitcast_convert_type(lax.bitcast_convert_type(z, jnp.int32) | SIGN_BIT, F32)
    return jnp.maximum(z, 0.0) + jnp.log(1.0 + jnp.exp2(neg_abs)) * LOG2E


def _causal_kernel(qb, sink_ref, aq_ref, ak_ref, avt_ref, tri_ref, bq_ref, bk_ref, bvt_ref,
                   cq_ref, ck_ref, cvt_ref, bias_ref, pa_ref, pb_ref, pc_ref, oa_ref, ob_ref, oc_ref):
    del pa_ref, pb_ref, pc_ref
    n_diag = TQ // TK
    a_heads, b_heads = range(SB_HEADS), range(MLA_HEADS)

    def put(full, part, c0):
        return part if c0 == 0 else jnp.concatenate([full[:, :c0], part], axis=1)

    def run_jobs(carry, jobs):
        ca, cb = list(carry[0]), list(carry[1])
        st = [None] * len(jobs)
        out_c = {}

        def geometry(t):
            mixer, hd, start, diag = jobs[t]
            c0 = 0 if diag is None else diag * TK
            row = lax.broadcasted_iota(jnp.int32, (TK, TQ - c0), 0)
            col = lax.broadcasted_iota(jnp.int32, (TK, TQ - c0), 1)
            return mixer, hd, start, diag is not None, c0, row, col

        def window(t):
            _, slot, sub = jobs[t]
            head = slot // 2 + (slot % 2) * (SW_HEADS // SW_KV_HEADS)
            key0 = qb * TQ + sub * TQ_SW - WINDOW
            lo = max(key0, 0)
            return slot, sub, head, lo, key0 + WINDOW + TQ_SW, lo - key0

        def scores(t):
            if jobs[t][0] == "c":
                slot, sub, head, lo, hi, skip = window(t)
                q = _half_mask(cq_ref[0, sub * TQ_SW:(sub + 1) * TQ_SW, (slot // 2) * SLOT:(slot // 2 + 1) * SLOT],
                               slot % 2)
                st[t] = _nt_dot(ck_ref[0, lo:hi, :], q) + bias_ref[head, skip:, :]
                return
            mixer, hd, start, masked, c0, row, col = geometry(t)
            if mixer == "a":
                g, half = divmod(hd, 2)
                q = _half_mask(aq_ref[0, c0:, g * SLOT:(g + 1) * SLOT], half)
                st[t] = _nt_dot(ak_ref[0, pl.ds(start, TK), g * SLOT:(g + 1) * SLOT], q)
            else:
                sc = _nt_dot(bk_ref[0, pl.ds(start, TK), hd * SLOT:(hd + 1) * SLOT],
                             bq_ref[0, c0:, hd * SLOT:(hd + 1) * SLOT])
                st[t] = jnp.where(row <= col, sc, NEG) if masked else sc

        def second(t):
            if jobs[t][0] == "c":
                slot, sub, head, lo, hi, skip = window(t)
                sink = sink_ref[head] * LOG2E
                m = jnp.maximum(jnp.max(st[t], axis=0, keepdims=True), sink)
                p = jnp.exp2(st[t] - m)
                denom = jnp.sum(p, axis=0, keepdims=True) + jnp.exp2(sink - m)
                vt = cvt_ref[0, (slot % 2) * HEAD_DIM:(slot % 2 + 1) * HEAD_DIM, lo:hi]
                out_c[(sub, slot)] = _dot(vt, p.astype(BF16)) * (1.0 / denom)
                st[t] = None
                return
            mixer, hd, start, masked, c0, row, col = geometry(t)
            if mixer == "a":
                drop = _softplus2(st[t])
                if masked:
                    drop = jnp.where(row < col, drop, 0.0)
                st[t] = (st[t], _dot(tri_ref[...], drop.astype(BF16)))
            else:
                m_all, l_all, acc_all = cb[hd]
                m_old = m_all[:, c0:]
                m_new = jnp.maximum(m_old, jnp.max(st[t], axis=0, keepdims=True))
                alpha = jnp.exp2(m_old - m_new)
                p = jnp.exp2(st[t] - m_new)
                l_new = alpha * l_all[:, c0:] + jnp.sum(p, axis=0, keepdims=True)
                pv = _dot(bvt_ref[0, hd * MLA_V:(hd + 1) * MLA_V, pl.ds(start, TK)], p.astype(BF16))
                cb[hd] = (put(m_all, m_new, c0), put(l_all, l_new, c0),
                          put(acc_all, alpha * acc_all[:, c0:] + pv, c0))

        def third(t):
            if jobs[t][0] == "c":
                return
            mixer, hd, start, masked, c0, row, col = geometry(t)
            if mixer == "a":
                z, incl = st[t]
                acc, run = ca[hd]
                w = jnp.exp2(z + incl + run[:, c0:])
                if masked:
                    w = jnp.where(row < col, w, 0.0)
                pv = _dot(avt_ref[0, hd * HEAD_DIM:(hd + 1) * HEAD_DIM, pl.ds(start, TK)], w.astype(BF16))
                ca[hd] = (put(acc, acc[:, c0:] + pv, c0), put(run, run[:, c0:] + incl[0:1, :], c0))
            st[t] = None

        n = len(jobs)
        for t in range(n + 2):
            if t < n:
                scores(t)
            if 0 <= t - 1 < n:
                second(t - 1)
            if 0 <= t - 2 < n:
                third(t - 2)
        return (tuple(ca), tuple(cb)), out_c

    carry = (tuple((jnp.zeros((HEAD_DIM, TQ), F32), jnp.zeros((1, TQ), F32)) for _ in a_heads),
             tuple((jnp.full((1, TQ), NEG, F32), jnp.zeros((1, TQ), F32), jnp.zeros((MLA_V, TQ), F32))
                   for _ in b_heads))
    n_past = qb * n_diag
    past = [(j * TK, None) for j in reversed(range(n_past))]
    a_blocks = [((n_past + d) * TK, d) for d in reversed(range(n_diag))] + past
    b_blocks = [((n_past + d) * TK, d) for d in range(n_diag)] + past
    sweep = [(mixer, hd) + (a_blocks[i] if mixer == "a" else b_blocks[i])
             for i in range(len(a_blocks)) for (mixer, hd) in JOBS]
    local = [("c", slot, sub) for sub in range(TQ // TQ_SW) for slot in range(SW_HEADS)]
    jobs = []
    for job in sweep:
        jobs.append(job)
        if local:
            jobs.append(local.pop(0))
    (ca, cb), out_c = run_jobs(carry, jobs)
    for sub in range(TQ // TQ_SW):
        for g in range(SW_HEADS // 2):
            _store_pair(oc_ref, g, out_c[(sub, 2 * g)], out_c[(sub, 2 * g + 1)], sub * TQ_SW)
    for g in range(SB_HEADS // 2):
        _store_pair(oa_ref, g, ca[2 * g][0], ca[2 * g + 1][0])
    outs = [acc * (1.0 / l) for (_, l, acc) in cb]
    for g in range(MLA_HEADS // 2):
        _store_pair(ob_ref, g, outs[2 * g], outs[2 * g + 1])


def _attention(sinks, aq, ak, avt, bq, bk, bvt, cq, ck, cvt, bias):
    b, s, wa = aq.shape
    wb, wbv, wc = bq.shape[2], bvt.shape[1], cq.shape[2]
    tri = -(jnp.arange(TK)[None, :] >= jnp.arange(TK)[:, None]).astype(BF16)
    const = lambda a: pl.BlockSpec(a.shape, lambda i: (0,) * a.ndim)
    outs = tuple(jnp.zeros((b, s, w), BF16) for w in (wa, wbv, wc))
    n_in = 12
    for qb in range(s // TQ):
        nk = (qb + 1) * TQ
        qblk = lambda w, qb=qb: pl.BlockSpec((1, TQ, w), lambda i: (i, qb, 0))
        keys = lambda w, nk=nk: pl.BlockSpec((1, nk, w), lambda i: (i, 0, 0))
        keyst = lambda w, nk=nk: pl.BlockSpec((1, w, nk), lambda i: (i, 0, 0))
        outs = pl.pallas_call(
            functools.partial(_causal_kernel, qb),
            out_shape=tuple(jax.ShapeDtypeStruct(o.shape, o.dtype) for o in outs),
            grid=(b,),
            in_specs=[pl.BlockSpec(memory_space=pltpu.SMEM),
                      qblk(wa), keys(wa), keyst(wa), const(tri),
                      qblk(wb), keys(wb), keyst(wbv),
                      qblk(wc), keys(ck.shape[2]), keyst(cvt.shape[1]), const(bias)]
                     + [pl.BlockSpec(memory_space=pl.ANY)] * 3,
            out_specs=(qblk(wa), qblk(wbv), qblk(wc)),
            input_output_aliases={n_in + k: k for k in range(3)},
            compiler_params=_cparams(1),
            name=f"attention_q{qb}",
        )(sinks, aq, ak, avt, tri, bq, bk, bvt, cq, ck, cvt, bias, *outs)
    return outs


def _mlp_kernel(x_ref, mod_ref, oa_ref, ob_ref, oc_ref, wo_ref, n2g_ref, wup_ref, cw_ref, cb_ref,
                wdn_ref, out_ref, carry_ref):
    si = pl.program_id(1)
    tm = x_ref.shape[1]
    mod = mod_ref[0]
    gate1, shift2, scale2, gate2 = mod[2:3], mod[3:4], mod[4:5], mod[5:6]
    mix = jnp.concatenate([oa_ref[0], ob_ref[0], oc_ref[0]], axis=1)
    x1 = x_ref[0] + gate1 * _dot(mix, wo_ref[...])
    h2 = (_rms(x1, D_MODEL) * n2g_ref[...] * (1.0 + scale2) + shift2).astype(BF16)

    @pl.when(si == 0)
    def _():
        carry_ref[...] = jnp.zeros_like(carry_ref)

    def up(c):
        lo = c * FF_CHUNK
        return (_dot(h2, wup_ref[:, lo:lo + FF_CHUNK]),
                _dot(h2, wup_ref[:, D_FF + lo:D_FF + lo + FF_CHUNK]))

    def conv(u, col):
        cols = slice(col, col + FF_CHUNK)
        prev = carry_ref[:, cols]
        carry_ref[:, cols] = u[tm - CARRY_ROWS:, :]
        ext = jnp.concatenate([prev, u], axis=0)
        u1 = ext[CARRY_ROWS - 1:CARRY_ROWS - 1 + tm, :]
        u2 = ext[CARRY_ROWS - 2:CARRY_ROWS - 2 + tm, :]
        cw = cw_ref[:, cols]
        return u * cw[2:3] + u1 * cw[1:2] + u2 * cw[0:1] + cb_ref[:, cols]

    n_chunks = D_FF // FF_CHUNK
    ahead = [up(c) for c in range(min(UP_AHEAD, n_chunks))]
    for c in range(n_chunks):
        u_gate, u_val = ahead.pop(0)
        if c + UP_AHEAD < n_chunks:
            ahead.append(up(c + UP_AHEAD))
        gate = conv(u_gate, c * FF_CHUNK)
        val = conv(u_val, D_FF + c * FF_CHUNK)
        a = (gate * jax.nn.sigmoid(gate) * val).astype(BF16)
        part = _dot(a, wdn_ref[c * FF_CHUNK:(c + 1) * FF_CHUNK, :])
        acc = part if c == 0 else acc + part
    out_ref[0] = x1 + gate2 * acc


def _mlp(x, mods, oa, ob, oc, w_out, n2g, w_up, conv_w, conv_b, w_down):
    b, s, d = x.shape
    tm = TM_MLP
    row = lambda w: pl.BlockSpec((1, tm, w), lambda i, j: (i, j, 0))
    const = lambda a: pl.BlockSpec(a.shape, lambda i, j: (0,) * a.ndim, pipeline_mode=pl.Buffered(1))
    return pl.pallas_call(
        _mlp_kernel,
        out_shape=jax.ShapeDtypeStruct((b, s, d), F32),
        grid=(b, s // tm),
        in_specs=[row(d), pl.BlockSpec((1, 6, d), lambda i, j: (i, 0, 0)),
                  row(oa.shape[2]), row(ob.shape[2]), row(oc.shape[2]),
                  const(w_out), const(n2g), const(w_up), const(conv_w), const(conv_b), const(w_down)],
        out_specs=row(d),
        scratch_shapes=[pltpu.VMEM((CARRY_ROWS, 2 * D_FF), F32)],
        compiler_params=_cparams(2),
        name="outproj_mlp",
    )(x, mods, oa, ob, oc, w_out, n2g, w_up, conv_w, conv_b, w_down)


def _slot_gain(g, swap):
    z = jnp.zeros((SLOT - MLA_QK,), F32)
    lo, hi = g[ROPE_LO:ROPE_LO + ROPE_HALF], g[ROPE_LO + ROPE_HALF:MLA_QK]
    if swap:
        return jnp.concatenate([jnp.zeros((MLA_NOPE,), F32), hi, lo, z]).reshape(1, SLOT)
    return jnp.concatenate([g, z]).reshape(1, SLOT)


def _layout_w_in(w):
    d = w.shape[0]
    z = lambda n: jnp.zeros((d, n), w.dtype)
    kr = w[:, 1152:1184]
    swq = w[:, 1184:1568].reshape(d, SW_HEADS, HEAD_DIM)
    order = [0, 3, 1, 4, 2, 5]
    swq = swq[:, order, :].reshape(d, SW_HEADS * HEAD_DIM)
    cols = [w[:, 0:1152],
            z(MLA_NOPE), kr, z(SLOT - MLA_QK),
            z(MLA_NOPE), kr[:, ROPE_HALF:], kr[:, :ROPE_HALF], z(SLOT - MLA_QK),
            w[:, 1568:1696], swq, w[:, 1696:1824]]
    return jnp.concatenate(cols, axis=1).astype(BF16)


def _layout_w_uq(w):
    r = w.shape[0]
    w = w.reshape(r, MLA_HEADS, MLA_QK)
    nope, x1, x2 = w[..., :MLA_NOPE], w[..., MLA_NOPE:MLA_NOPE + ROPE_HALF], w[..., MLA_NOPE + ROPE_HALF:]
    z = jnp.zeros((r, MLA_HEADS, SLOT - MLA_QK), w.dtype)
    plain = jnp.concatenate([nope, x1, x2, z], axis=-1).reshape(r, MLA_HEADS * SLOT)
    swapped = jnp.concatenate([jnp.zeros_like(nope), x2, x1, z], axis=-1).reshape(r, MLA_HEADS * SLOT)
    return jnp.concatenate([plain, swapped], axis=1).astype(BF16)


def _layout_w_ukv(w):
    r = w.shape[0]
    w = w.reshape(r, MLA_HEADS, MLA_NOPE + MLA_V)
    k_nope, v = w[..., :MLA_NOPE], w[..., MLA_NOPE:]
    k_slots = jnp.concatenate([k_nope, jnp.zeros((r, MLA_HEADS, SLOT - MLA_NOPE), w.dtype)], axis=-1)
    return jnp.concatenate([k_slots.reshape(r, MLA_HEADS * SLOT), v.reshape(r, MLA_HEADS * MLA_V)],
                           axis=1).astype(BF16)


def _layout_w_out(w):
    n_ab = SB_HEADS * HEAD_DIM + MLA_HEADS * MLA_V
    sw = w[n_ab:].reshape(SW_HEADS, HEAD_DIM, w.shape[1])[jnp.array([0, 3, 1, 4, 2, 5])]
    return jnp.concatenate([w[:n_ab], sw.reshape(SW_HEADS * HEAD_DIM, w.shape[1])], axis=0).astype(BF16)


def kernel(x, c, positions, rel_table, norm1_g, norm2_g, w_ada, b_ada, w_in, mla_cq_g, w_uq, mla_ckv_g,
           w_ukv, mla_qn_g, mla_kn_g, sw_qn_g, sw_kn_g, sw_sinks, w_out, w_up, conv_w, conv_b, w_down):
    depth = w_in.shape[0]
    b = x.shape[0]
    mods = _mods(c, w_ada, b_ada).reshape(depth, b, 6, D_MODEL)
    cos_t, sin_t = _rope_tables(positions)
    bias = _window_bias(rel_table)
    row = lambda v: v.reshape(1, -1).astype(F32)
    two = lambda v: jnp.concatenate([v, v]).reshape(1, SLOT).astype(F32)
    for l in range(depth):
        qkv = _prep(x, mods[l], row(norm1_g[l]), _layout_w_in(w_in[l]), row(mla_cq_g[l]),
                    _layout_w_uq(w_uq[l]), row(mla_ckv_g[l]), _layout_w_ukv(w_ukv[l]),
                    _slot_gain(mla_qn_g[l], False), _slot_gain(mla_qn_g[l], True),
                    _slot_gain(mla_kn_g[l], False), _slot_gain(mla_kn_g[l], True),
                    two(sw_qn_g[l]), two(sw_kn_g[l]), cos_t, sin_t)
        sbq, sbk, sbvt, mq, mk, mvt, swq, swk, swvt = qkv
        o_a, o_b, o_c = _attention(sw_sinks[l], sbq, sbk, sbvt, mq, mk, mvt, swq, swk, swvt, bias)
        x = _mlp(x, mods[l], o_a, o_b, o_c, _layout_w_out(w_out[l]), row(norm2_g[l]),
                 w_up[l].astype(BF16), conv_w[l], row(conv_b[l]), w_down[l].astype(BF16))
    return x
```

```python
import functools
import math

import numpy as np
import jax
import jax.numpy as jnp
from jax import lax
from jax.experimental import pallas as pl
from jax.experimental.pallas import tpu as pltpu

F32 = jnp.float32
BF16 = jnp.bfloat16

D_MODEL = 1024
HEAD_DIM = 64
SB_HEADS = 4
MLA_HEADS = 6
MLA_Q_RANK = 256
MLA_KV_RANK = 128
MLA_NOPE = 64
MLA_ROPE = 32
MLA_V = 64
MLA_QK = MLA_NOPE + MLA_ROPE
ROPE_THETA = 10000.0
SW_HEADS = 6
SW_KV_HEADS = 2
WINDOW = 128
REL_BUCKETS = 32
REL_MAX_DIST = 128
D_FF = 2816
CONV_W = 3
EPS = 1e-6
NEG = -1e30

LANES = 128
SLOT = LANES
HALF = SLOT // 2

C_SBQ, C_SBK, C_SBV = 0, 256, 512
C_CQ = 768
C_CKV = 1024
C_KROPE = 1152
C_KROPE_SW = 1280
C_SWK = 1408
C_SWQ = 1536
C_SWV = 1920
N_IN = 2048

ROPE_LO = MLA_NOPE
ROPE_HALF = MLA_ROPE // 2

TM_PREP = 512
TM_MLP = 512
TQ = 512
TK = 256
TQ_SW = 256
LOG2E = math.log2(math.e)
FF_CHUNK = 256
CARRY_ROWS = 8
UP_AHEAD = 3

VMEM_LIMIT = 56 * 1024 * 1024


def _cparams(n_axes):
    return pltpu.CompilerParams(dimension_semantics=("arbitrary",) * n_axes,
                                vmem_limit_bytes=VMEM_LIMIT)


def _rms(x, n):
    return x * lax.rsqrt(jnp.sum(x * x, axis=-1, keepdims=True) * (1.0 / n) + EPS)


def _nt_dot(a, b):
    return lax.dot_general(a, b, (((1,), (1,)), ((), ())), preferred_element_type=F32)


def _dot(a, b):
    return jnp.dot(a, b, preferred_element_type=F32)


def _mods_kernel(c_ref, w_ref, b_ref, o_ref):
    c = c_ref[...]
    a = (c * jax.nn.sigmoid(c)).astype(BF16)
    o_ref[0] = _dot(a, w_ref[0].astype(BF16)) + b_ref[0]


def _mods(c, w_ada, b_ada):
    depth, d, n = w_ada.shape
    b = c.shape[0]
    tn = 1536
    return pl.pallas_call(
        _mods_kernel,
        out_shape=jax.ShapeDtypeStruct((depth, b, n), F32),
        grid=(depth, n // tn),
        in_specs=[pl.BlockSpec((b, d), lambda l, j: (0, 0)),
                  pl.BlockSpec((1, d, tn), lambda l, j: (l, 0, j)),
                  pl.BlockSpec((1, 1, tn), lambda l, j: (l, 0, j))],
        out_specs=pl.BlockSpec((1, b, tn), lambda l, j: (l, 0, j)),
        compiler_params=_cparams(2),
        name="adaln_mods",
    )(c, w_ada, b_ada.reshape(depth, 1, n))


def _rope_kernel(pos_ref, invf_ref, cos_ref, sin_ref):
    pos = pos_ref[0].astype(F32)
    ang = invf_ref[...] * pos
    c = jnp.cos(ang)
    s = jnp.sin(ang)
    tm = pos.shape[1]
    ones = jnp.ones((ROPE_LO, tm), F32)
    zeros = jnp.zeros((ROPE_LO, tm), F32)
    pad = SLOT - ROPE_LO - MLA_ROPE
    cos_t = jnp.concatenate([ones, c, c, jnp.ones((pad, tm), F32)], axis=0)
    sin_t = jnp.concatenate([zeros, -s, s, jnp.zeros((pad, tm), F32)], axis=0)
    cos_ref[0] = cos_t.T
    sin_ref[0] = sin_t.T


def _rope_tables(positions):
    b, s = positions.shape
    tm = s
    half = ROPE_HALF
    inv_freq = jnp.power(ROPE_THETA, -jnp.arange(half, dtype=F32) / half).reshape(half, 1)
    out = jax.ShapeDtypeStruct((b, s, SLOT), F32)
    return pl.pallas_call(
        _rope_kernel,
        out_shape=(out, out),
        grid=(b, s // tm),
        in_specs=[pl.BlockSpec((1, 1, tm), lambda i, j: (i, 0, j)),
                  pl.BlockSpec((half, 1), lambda i, j: (0, 0))],
        out_specs=(pl.BlockSpec((1, tm, SLOT), lambda i, j: (i, j, 0)),
                   pl.BlockSpec((1, tm, SLOT), lambda i, j: (i, j, 0))),
        compiler_params=_cparams(2),
        name="rope_tables",
    )(positions.reshape(b, 1, s), inv_freq)


def _t5_bucket(dist):
    max_exact = REL_BUCKETS // 2
    n = jnp.maximum(dist, 0)
    nf = jnp.maximum(n, 1).astype(F32)
    large = max_exact + (jnp.log(nf / max_exact) / math.log(REL_MAX_DIST / max_exact)
                         * (REL_BUCKETS - max_exact)).astype(jnp.int32)
    large = jnp.minimum(large, REL_BUCKETS - 1)
    return jnp.where(n < max_exact, n, large)


def _bias_kernel(tab_ref, bucket_ref, o_ref):
    bucket = bucket_ref[...]
    for h in range(SW_HEADS):
        acc = jnp.zeros(bucket.shape, F32)
        for bkt in range(REL_BUCKETS):
            acc = jnp.where(bucket == bkt, tab_ref[bkt, h], acc)
        o_ref[h] = jnp.where(bucket >= 0, acc * LOG2E, NEG)


def _window_bias(rel_table):
    nk = WINDOW + TQ_SW
    key = jnp.arange(nk)[:, None]
    qry = jnp.arange(TQ_SW)[None, :]
    dist = qry + WINDOW - key
    valid = (dist >= 0) & (dist < WINDOW)
    bucket = jnp.where(valid, _t5_bucket(dist), -1).astype(jnp.int32)
    return pl.pallas_call(
        _bias_kernel,
        out_shape=jax.ShapeDtypeStruct((SW_HEADS, nk, TQ_SW), F32),
        in_specs=[pl.BlockSpec(memory_space=pltpu.SMEM),
                  pl.BlockSpec(memory_space=pltpu.VMEM)],
        out_specs=pl.BlockSpec(memory_space=pltpu.VMEM),
        name="window_bias",
    )(rel_table, bucket)


def _half_rms(x, gain, scale):
    lo = lax.broadcasted_iota(jnp.int32, x.shape, 1) < HALF
    sq = x * x
    s_lo = jnp.sum(jnp.where(lo, sq, 0.0), axis=-1, keepdims=True)
    s_hi = jnp.sum(jnp.where(lo, 0.0, sq), axis=-1, keepdims=True)
    r = jnp.where(lo, lax.rsqrt(s_lo * (1.0 / HEAD_DIM) + EPS), lax.rsqrt(s_hi * (1.0 / HEAD_DIM) + EPS))
    return x * r * (gain * scale)


def _prep_kernel(x_ref, mod_ref, n1g_ref, win_ref, cqg_ref, wuq_ref, ckvg_ref, wukv_ref,
                 gq_ref, gqs_ref, gk_ref, gks_ref, swqg_ref, swkg_ref, cos_ref, sin_ref,
                 sbq_ref, sbk_ref, sbvt_ref, mq_ref, mk_ref, mvt_ref, swq_ref, swk_ref, swvt_ref):
    x = x_ref[0]
    mod = mod_ref[0]
    shift1, scale1 = mod[0:1], mod[1:2]
    h = (_rms(x, D_MODEL) * n1g_ref[...] * (1.0 + scale1) + shift1).astype(BF16)

    proj_mla = _dot(h, win_ref[:, C_CQ:C_SWQ])
    proj_sb = _dot(h, win_ref[:, C_SBQ:C_CQ])
    cq = proj_mla[:, 0:MLA_Q_RANK]
    ckv = proj_mla[:, C_CKV - C_CQ:C_CKV - C_CQ + MLA_KV_RANK]
    krope = proj_mla[:, C_KROPE - C_CQ:C_KROPE - C_CQ + SLOT]
    krope_sw = proj_mla[:, C_KROPE_SW - C_CQ:C_KROPE_SW - C_CQ + SLOT]
    cqn = (_rms(cq, MLA_Q_RANK) * cqg_ref[...]).astype(BF16)
    ckvn = (_rms(ckv, MLA_KV_RANK) * ckvg_ref[...]).astype(BF16)
    qraw = _dot(cqn, wuq_ref[...])
    kv = _dot(ckvn, wukv_ref[...])
    proj_sw = _dot(h, win_ref[:, C_SWQ:N_IN])

    sbq_ref[0] = (proj_sb[:, C_SBQ:C_SBQ + 256] * (HEAD_DIM ** -0.5 * LOG2E)).astype(BF16)
    sbk_ref[0] = proj_sb[:, C_SBK:C_SBK + 256].astype(BF16)
    sbvt_ref[0] = proj_sb[:, C_SBV:C_SBV + 256].T.astype(BF16)

    cos = cos_ref[0]
    sin = sin_ref[0]

    nq = MLA_HEADS * SLOT
    q_scale = MLA_QK ** -0.5 * LOG2E
    q_cos, q_sin = gq_ref[...] * cos, gqs_ref[...] * sin
    for hd in range(MLA_HEADS):
        slot = qraw[:, hd * SLOT:(hd + 1) * SLOT]
        swapped = qraw[:, nq + hd * SLOT:nq + (hd + 1) * SLOT]
        r = lax.rsqrt(jnp.sum(slot * slot, axis=-1, keepdims=True) * (1.0 / MLA_QK) + EPS) * q_scale
        mq_ref[0, :, hd * SLOT:(hd + 1) * SLOT] = ((slot * q_cos + swapped * q_sin) * r).astype(BF16)

    mvt_ref[0] = kv[:, nq:nq + MLA_HEADS * MLA_V].T.astype(BF16)
    k_cos = gk_ref[...] * cos
    k_rot = krope_sw * (gks_ref[...] * sin)
    for hd in range(MLA_HEADS):
        slot = kv[:, hd * SLOT:(hd + 1) * SLOT] + krope
        r = lax.rsqrt(jnp.sum(slot * slot, axis=-1, keepdims=True) * (1.0 / MLA_QK) + EPS)
        mk_ref[0, :, hd * SLOT:(hd + 1) * SLOT] = ((slot * k_cos + k_rot) * r).astype(BF16)

    for g in range(SW_HEADS // 2):
        xq = proj_sw[:, g * SLOT:(g + 1) * SLOT]
        swq_ref[0, :, g * SLOT:(g + 1) * SLOT] = _half_rms(
            xq, swqg_ref[...], HEAD_DIM ** -0.5 * LOG2E).astype(BF16)
    swk_ref[0] = _half_rms(proj_mla[:, C_SWK - C_CQ:C_SWK - C_CQ + SLOT], swkg_ref[...], 1.0).astype(BF16)
    swvt_ref[0] = proj_sw[:, C_SWV - C_SWQ:C_SWV - C_SWQ + SLOT].T.astype(BF16)


def _prep(x, mods, n1g, w_in, cqg, w_uq, ckvg, w_ukv, gq, gqs, gk, gks, swqg, swkg, cos_t, sin_t):
    b, s, d = x.shape
    tm = TM_PREP
    row = lambda w: pl.BlockSpec((1, tm, w), lambda i, j: (i, j, 0))
    colt = lambda w: pl.BlockSpec((1, w, tm), lambda i, j: (i, 0, j))
    full = lambda a: pl.BlockSpec(a.shape, lambda i, j: (0,) * a.ndim)
    act = lambda w: jax.ShapeDtypeStruct((b, s, w), BF16)
    actt = lambda w: jax.ShapeDtypeStruct((b, w, s), BF16)
    return pl.pallas_call(
        _prep_kernel,
        out_shape=(act(256), act(256), actt(256), act(768), act(768), actt(384),
                   act(384), act(128), actt(128)),
        grid=(b, s // tm),
        in_specs=[row(d), pl.BlockSpec((1, 6, d), lambda i, j: (i, 0, 0)), full(n1g), full(w_in),
                  full(cqg), full(w_uq), full(ckvg), full(w_ukv), full(gq), full(gqs), full(gk),
                  full(gks), full(swqg), full(swkg), row(SLOT), row(SLOT)],
        out_specs=(row(256), row(256), colt(256), row(768), row(768), colt(384),
                   row(384), row(128), colt(128)),
        compiler_params=_cparams(2),
        name="prep_qkv",
    )(x, mods, n1g, w_in, cqg, w_uq, ckvg, w_ukv, gq, gqs, gk, gks, swqg, swkg, cos_t, sin_t)


def _half_mask(q, half):
    lane = lax.broadcasted_iota(jnp.int32, q.shape, 1)
    keep = (lane < HALF) if half == 0 else (lane >= HALF)
    return jnp.where(keep, q, jnp.zeros_like(q))


def _store_pair(o_ref, g, out_lo, out_hi, row0=0):
    pair = jnp.concatenate([out_lo, out_hi], axis=0)
    o_ref[0, row0:row0 + pair.shape[1], g * SLOT:(g + 1) * SLOT] = pair.T.astype(o_ref.dtype)


SIGN_BIT = -2 ** 31
JOBS = (("a", 0), ("b", 0), ("b", 1), ("a", 1), ("b", 2), ("b", 3), ("a", 2), ("b", 4), ("b", 5), ("a", 3))


def _softplus2(z):
    neg_abs = lax.bitcast_convert_type(lax.bitcast_convert_type(z, jnp.int32) | SIGN_BIT, F32)
    return jnp.maximum(z, 0.0) + jnp.log(1.0 + jnp.exp2(neg_abs)) * LOG2E


def _causal_kernel(qb, sink_ref, aq_ref, ak_ref, avt_ref, tri_ref, bq_ref, bk_ref, bvt_ref,
                   cq_ref, ck_ref, cvt_ref, bias_ref, pa_ref, pb_ref, pc_ref, oa_ref, ob_ref, oc_ref):
    del pa_ref, pb_ref, pc_ref
    n_diag = TQ // TK
    a_heads, b_heads = range(SB_HEADS), range(MLA_HEADS)

    def put(full, part, c0):
        return part if c0 == 0 else jnp.concatenate([full[:, :c0], part], axis=1)

    def run_jobs(carry, jobs):
        ca, cb = list(carry[0]), list(carry[1])
        st = [None] * len(jobs)
        out_c = {}

        def geometry(t):
            mixer, hd, start, diag = jobs[t]
            c0 = 0 if diag is None else diag * TK
            row = lax.broadcasted_iota(jnp.int32, (TK, TQ - c0), 0)
            col = lax.broadcasted_iota(jnp.int32, (TK, TQ - c0), 1)
            return mixer, hd, start, diag is not None, c0, row, col

        def window(t):
            _, slot, sub = jobs[t]
            head = slot // 2 + (slot % 2) * (SW_HEADS // SW_KV_HEADS)
            key0 = qb * TQ + sub * TQ_SW - WINDOW
            lo = max(key0, 0)
            return slot, sub, head, lo, key0 + WINDOW + TQ_SW, lo - key0

        def scores(t):
            if jobs[t][0] == "c":
                slot, sub, head, lo, hi, skip = window(t)
                q = _half_mask(cq_ref[0, sub * TQ_SW:(sub + 1) * TQ_SW, (slot // 2) * SLOT:(slot // 2 + 1) * SLOT],
                               slot % 2)
                st[t] = _nt_dot(ck_ref[0, lo:hi, :], q) + bias_ref[head, skip:, :]
                return
            mixer, hd, start, masked, c0, row, col = geometry(t)
            if mixer == "a":
                g, half = divmod(hd, 2)
                q = _half_mask(aq_ref[0, c0:, g * SLOT:(g + 1) * SLOT], half)
                st[t] = _nt_dot(ak_ref[0, pl.ds(start, TK), g * SLOT:(g + 1) * SLOT], q)
            else:
                sc = _nt_dot(bk_ref[0, pl.ds(start, TK), hd * SLOT:(hd + 1) * SLOT],
                             bq_ref[0, c0:, hd * SLOT:(hd + 1) * SLOT])
                st[t] = jnp.where(row <= col, sc, NEG) if masked else sc

        def second(t):
            if jobs[t][0] == "c":
                slot, sub, head, lo, hi, skip = window(t)
                sink = sink_ref[head] * LOG2E
                m = jnp.maximum(jnp.max(st[t], axis=0, keepdims=True), sink)
                p = jnp.exp2(st[t] - m)
                denom = jnp.sum(p, axis=0, keepdims=True) + jnp.exp2(sink - m)
                vt = cvt_ref[0, (slot % 2) * HEAD_DIM:(slot % 2 + 1) * HEAD_DIM, lo:hi]
                out_c[(sub, slot)] = _dot(vt, p.astype(BF16)) * (1.0 / denom)
                st[t] = None
                return
            mixer, hd, start, masked, c0, row, col = geometry(t)
            if mixer == "a":
                drop = _softplus2(st[t])
                if masked:
                    drop = jnp.where(row < col, drop, 0.0)
                st[t] = (st[t], _dot(tri_ref[...], drop.astype(BF16)))
            else:
                m_all, l_all, acc_all = cb[hd]
                m_old = m_all[:, c0:]
                m_new = jnp.maximum(m_old, jnp.max(st[t], axis=0, keepdims=True))
                alpha = jnp.exp2(m_old - m_new)
                p = jnp.exp2(st[t] - m_new)
                l_new = alpha * l_all[:, c0:] + jnp.sum(p, axis=0, keepdims=True)
                pv = _dot(bvt_ref[0, hd * MLA_V:(hd + 1) * MLA_V, pl.ds(start, TK)], p.astype(BF16))
                cb[hd] = (put(m_all, m_new, c0), put(l_all, l_new, c0),
                          put(acc_all, alpha * acc_all[:, c0:] + pv, c0))

        def third(t):
            if jobs[t][0] == "c":
                return
            mixer, hd, start, masked, c0, row, col = geometry(t)
            if mixer == "a":
                z, incl = st[t]
                acc, run = ca[hd]
                w = jnp.exp2(z + incl + run[:, c0:])
                if masked:
                    w = jnp.where(row < col, w, 0.0)
                pv = _dot(avt_ref[0, hd * HEAD_DIM:(hd + 1) * HEAD_DIM, pl.ds(start, TK)], w.astype(BF16))
                ca[hd] = (put(acc, acc[:, c0:] + pv, c0), put(run, run[:, c0:] + incl[0:1, :], c0))
            st[t] = None

        n = len(jobs)
        for t in range(n + 2):
            if t < n:
                scores(t)
            if 0 <= t - 1 < n:
                second(t - 1)
            if 0 <= t - 2 < n:
                third(t - 2)
        return (tuple(ca), tuple(cb)), out_c

    carry = (tuple((jnp.zeros((HEAD_DIM, TQ), F32), jnp.zeros((1, TQ), F32)) for _ in a_heads),
             tuple((jnp.full((1, TQ), NEG, F32), jnp.zeros((1, TQ), F32), jnp.zeros((MLA_V, TQ), F32))
                   for _ in b_heads))
    n_past = qb * n_diag
    past = [(j * TK, None) for j in reversed(range(n_past))]
    a_blocks = [((n_past + d) * TK, d) for d in reversed(range(n_diag))] + past
    b_blocks = [((n_past + d) * TK, d) for d in range(n_diag)] + past
    sweep = [(mixer, hd) + (a_blocks[i] if mixer == "a" else b_blocks[i])
             for i in range(len(a_blocks)) for (mixer, hd) in JOBS]
    local = [("c", slot, sub) for sub in range(TQ // TQ_SW) for slot in range(SW_HEADS)]
    jobs = []
    for job in sweep:
        jobs.append(job)
        if local:
            jobs.append(local.pop(0))
    (ca, cb), out_c = run_jobs(carry, jobs)
    for sub in range(TQ // TQ_SW):
        for g in range(SW_HEADS // 2):
            _store_pair(oc_ref, g, out_c[(sub, 2 * g)], out_c[(sub, 2 * g + 1)], sub * TQ_SW)
    for g in range(SB_HEADS // 2):
        _store_pair(oa_ref, g, ca[2 * g][0], ca[2 * g + 1][0])
    outs = [acc * (1.0 / l) for (_, l, acc) in cb]
    for g in range(MLA_HEADS // 2):
        _store_pair(ob_ref, g, outs[2 * g], outs[2 * g + 1])


def _attention(sinks, aq, ak, avt, bq, bk, bvt, cq, ck, cvt, bias):
    b, s, wa = aq.shape
    wb, wbv, wc = bq.shape[2], bvt.shape[1], cq.shape[2]
    tri = -(jnp.arange(TK)[None, :] >= jnp.arange(TK)[:, None]).astype(BF16)
    const = lambda a: pl.BlockSpec(a.shape, lambda i: (0,) * a.ndim)
    outs = tuple(jnp.zeros((b, s, w), BF16) for w in (wa, wbv, wc))
    n_in = 12
    for qb in range(s // TQ):
        nk = (qb + 1) * TQ
        qblk = lambda w, qb=qb: pl.BlockSpec((1, TQ, w), lambda i: (i, qb, 0))
        keys = lambda w, nk=nk: pl.BlockSpec((1, nk, w), lambda i: (i, 0, 0))
        keyst = lambda w, nk=nk: pl.BlockSpec((1, w, nk), lambda i: (i, 0, 0))
        outs = pl.pallas_call(
            functools.partial(_causal_kernel, qb),
            out_shape=tuple(jax.ShapeDtypeStruct(o.shape, o.dtype) for o in outs),
            grid=(b,),
            in_specs=[pl.BlockSpec(memory_space=pltpu.SMEM),
                      qblk(wa), keys(wa), keyst(wa), const(tri),
                      qblk(wb), keys(wb), keyst(wbv),
                      qblk(wc), keys(ck.shape[2]), keyst(cvt.shape[1]), const(bias)]
                     + [pl.BlockSpec(memory_space=pl.ANY)] * 3,
            out_specs=(qblk(wa), qblk(wbv), qblk(wc)),
            input_output_aliases={n_in + k: k for k in range(3)},
            compiler_params=_cparams(1),
            name=f"attention_q{qb}",
        )(sinks, aq, ak, avt, tri, bq, bk, bvt, cq, ck, cvt, bias, *outs)
    return outs


def _mlp_kernel(x_ref, mod_ref, oa_ref, ob_ref, oc_ref, wo_ref, n2g_ref, wup_ref, cw_ref, cb_ref,
                wdn_ref, out_ref, carry_ref):
    si = pl.program_id(1)
    tm = x_ref.shape[1]
    mod = mod_ref[0]
    gate1, shift2, scale2, gate2 = mod[2:3], mod[3:4], mod[4:5], mod[5:6]
    mix = jnp.concatenate([oa_ref[0], ob_ref[0], oc_ref[0]], axis=1)
    x1 = x_ref[0] + gate1 * _dot(mix, wo_ref[...])
    h2 = (_rms(x1, D_MODEL) * n2g_ref[...] * (1.0 + scale2) + shift2).astype(BF16)

    @pl.when(si == 0)
    def _():
        carry_ref[...] = jnp.zeros_like(carry_ref)

    def up(c):
        lo = c * FF_CHUNK
        return (_dot(h2, wup_ref[:, lo:lo + FF_CHUNK]),
                _dot(h2, wup_ref[:, D_FF + lo:D_FF + lo + FF_CHUNK]))

    def conv(u, col):
        cols = slice(col, col + FF_CHUNK)
        prev = carry_ref[:, cols]
        carry_ref[:, cols] = u[tm - CARRY_ROWS:, :]
        ext = jnp.concatenate([prev, u], axis=0)
        u1 = ext[CARRY_ROWS - 1:CARRY_ROWS - 1 + tm, :]
        u2 = ext[CARRY_ROWS - 2:CARRY_ROWS - 2 + tm, :]
        cw = cw_ref[:, cols]
        return u * cw[2:3] + u1 * cw[1:2] + u2 * cw[0:1] + cb_ref[:, cols]

    n_chunks = D_FF // FF_CHUNK
    ahead = [up(c) for c in range(min(UP_AHEAD, n_chunks))]
    for c in range(n_chunks):
        u_gate, u_val = ahead.pop(0)
        if c + UP_AHEAD < n_chunks:
            ahead.append(up(c + UP_AHEAD))
        gate = conv(u_gate, c * FF_CHUNK)
        val = conv(u_val, D_FF + c * FF_CHUNK)
        a = (gate * jax.nn.sigmoid(gate) * val).astype(BF16)
        part = _dot(a, wdn_ref[c * FF_CHUNK:(c + 1) * FF_CHUNK, :])
        acc = part if c == 0 else acc + part
    out_ref[0] = x1 + gate2 * acc


def _mlp(x, mods, oa, ob, oc, w_out, n2g, w_up, conv_w, conv_b, w_down):
    b, s, d = x.shape
    tm = TM_MLP
    row = lambda w: pl.BlockSpec((1, tm, w), lambda i, j: (i, j, 0))
    const = lambda a: pl.BlockSpec(a.shape, lambda i, j: (0,) * a.ndim, pipeline_mode=pl.Buffered(1))
    return pl.pallas_call(
        _mlp_kernel,
        out_shape=jax.ShapeDtypeStruct((b, s, d), F32),
        grid=(b, s // tm),
        in_specs=[row(d), pl.BlockSpec((1, 6, d), lambda i, j: (i, 0, 0)),
                  row(oa.shape[2]), row(ob.shape[2]), row(oc.shape[2]),
                  const(w_out), const(n2g), const(w_up), const(conv_w), const(conv_b), const(w_down)],
        out_specs=row(d),
        scratch_shapes=[pltpu.VMEM((CARRY_ROWS, 2 * D_FF), F32)],
        compiler_params=_cparams(2),
        name="outproj_mlp",
    )(x, mods, oa, ob, oc, w_out, n2g, w_up, conv_w, conv_b, w_down)


def _slot_gain(g, swap):
    z = jnp.zeros((SLOT - MLA_QK,), F32)
    lo, hi = g[ROPE_LO:ROPE_LO + ROPE_HALF], g[ROPE_LO + ROPE_HALF:MLA_QK]
    if swap:
        return jnp.concatenate([jnp.zeros((MLA_NOPE,), F32), hi, lo, z]).reshape(1, SLOT)
    return jnp.concatenate([g, z]).reshape(1, SLOT)


def _layout_w_in(w):
    d = w.shape[0]
    z = lambda n: jnp.zeros((d, n), w.dtype)
    kr = w[:, 1152:1184]
    swq = w[:, 1184:1568].reshape(d, SW_HEADS, HEAD_DIM)
    order = [0, 3, 1, 4, 2, 5]
    swq = swq[:, order, :].reshape(d, SW_HEADS * HEAD_DIM)
    cols = [w[:, 0:1152],
            z(MLA_NOPE), kr, z(SLOT - MLA_QK),
            z(MLA_NOPE), kr[:, ROPE_HALF:], kr[:, :ROPE_HALF], z(SLOT - MLA_QK),
            w[:, 1568:1696], swq, w[:, 1696:1824]]
    return jnp.concatenate(cols, axis=1).astype(BF16)


def _layout_w_uq(w):
    r = w.shape[0]
    w = w.reshape(r, MLA_HEADS, MLA_QK)
    nope, x1, x2 = w[..., :MLA_NOPE], w[..., MLA_NOPE:MLA_NOPE + ROPE_HALF], w[..., MLA_NOPE + ROPE_HALF:]
    z = jnp.zeros((r, MLA_HEADS, SLOT - MLA_QK), w.dtype)
    plain = jnp.concatenate([nope, x1, x2, z], axis=-1).reshape(r, MLA_HEADS * SLOT)
    swapped = jnp.concatenate([jnp.zeros_like(nope), x2, x1, z], axis=-1).reshape(r, MLA_HEADS * SLOT)
    return jnp.concatenate([plain, swapped], axis=1).astype(BF16)


def _layout_w_ukv(w):
    r = w.shape[0]
    w = w.reshape(r, MLA_HEADS, MLA_NOPE + MLA_V)
    k_nope, v = w[..., :MLA_NOPE], w[..., MLA_NOPE:]
    k_slots = jnp.concatenate([k_nope, jnp.zeros((r, MLA_HEADS, SLOT - MLA_NOPE), w.dtype)], axis=-1)
    return jnp.concatenate([k_slots.reshape(r, MLA_HEADS * SLOT), v.reshape(r, MLA_HEADS * MLA_V)],
                           axis=1).astype(BF16)


def _layout_w_out(w):
    n_ab = SB_HEADS * HEAD_DIM + MLA_HEADS * MLA_V
    sw = w[n_ab:].reshape(SW_HEADS, HEAD_DIM, w.shape[1])[jnp.array([0, 3, 1, 4, 2, 5])]
    return jnp.concatenate([w[:n_ab], sw.reshape(SW_HEADS * HEAD_DIM, w.shape[1])], axis=0).astype(BF16)


def kernel(x, c, positions, rel_table, norm1_g, norm2_g, w_ada, b_ada, w_in, mla_cq_g, w_uq, mla_ckv_g,
           w_ukv, mla_qn_g, mla_kn_g, sw_qn_g, sw_kn_g, sw_sinks, w_out, w_up, conv_w, conv_b, w_down):
    depth = w_in.shape[0]
    b = x.shape[0]
    mods = _mods(c, w_ada, b_ada).reshape(depth, b, 6, D_MODEL)
    cos_t, sin_t = _rope_tables(positions)
    bias = _window_bias(rel_table)
    row = lambda v: v.reshape(1, -1).astype(F32)
    two = lambda v: jnp.concatenate([v, v]).reshape(1, SLOT).astype(F32)
    for l in range(depth):
        qkv = _prep(x, mods[l], row(norm1_g[l]), _layout_w_in(w_in[l]), row(mla_cq_g[l]),
                    _layout_w_uq(w_uq[l]), row(mla_ckv_g[l]), _layout_w_ukv(w_ukv[l]),
                    _slot_gain(mla_qn_g[l], False), _slot_gain(mla_qn_g[l], True),
                    _slot_gain(mla_kn_g[l], False), _slot_gain(mla_kn_g[l], True),
                    two(sw_qn_g[l]), two(sw_kn_g[l]), cos_t, sin_t)
        sbq, sbk, sbvt, mq, mk, mvt, swq, swk, swvt = qkv
        o_a, o_b, o_c = _attention(sw_sinks[l], sbq, sbk, sbvt, mq, mk, mvt, swq, swk, swvt, bias)
        x = _mlp(x, mods[l], o_a, o_b, o_c, _layout_w_out(w_out[l]), row(norm2_g[l]),
                 w_up[l].astype(BF16), conv_w[l], row(conv_b[l]), w_down[l].astype(BF16))
    return x
```

```python
import functools
import math

import numpy as np
import jax
import jax.numpy as jnp
from jax import lax
from jax.experimental import pallas as pl
from jax.experimental.pallas import tpu as pltpu

F32 = jnp.float32
BF16 = jnp.bfloat16

D_MODEL = 1024
HEAD_DIM = 64
SB_HEADS = 4
MLA_HEADS = 6
MLA_Q_RANK = 256
MLA_KV_RANK = 128
MLA_NOPE = 64
MLA_ROPE = 32
MLA_V = 64
MLA_QK = MLA_NOPE + MLA_ROPE
ROPE_THETA = 10000.0
SW_HEADS = 6
SW_KV_HEADS = 2
WINDOW = 128
REL_BUCKETS = 32
REL_MAX_DIST = 128
D_FF = 2816
CONV_W = 3
EPS = 1e-6
NEG = -1e30

LANES = 128
SLOT = LANES
HALF = SLOT // 2

C_SBQ, C_SBK, C_SBV = 0, 256, 512
C_CQ = 768
C_CKV = 1024
C_KROPE = 1152
C_KROPE_SW = 1280
C_SWK = 1408
C_SWQ = 1536
C_SWV = 1920
N_IN = 2048

ROPE_LO = MLA_NOPE
ROPE_HALF = MLA_ROPE // 2

TM_PREP = 512
TM_MLP = 512
TQ = 512
TK = 256
TQ_SW = 256
LOG2E = math.log2(math.e)
FF_CHUNK = 256
CARRY_ROWS = 8
UP_AHEAD = 5

VMEM_LIMIT = 56 * 1024 * 1024


def _cparams(n_axes):
    return pltpu.CompilerParams(dimension_semantics=("arbitrary",) * n_axes,
                                vmem_limit_bytes=VMEM_LIMIT)


def _rms(x, n):
    return x * lax.rsqrt(jnp.sum(x * x, axis=-1, keepdims=True) * (1.0 / n) + EPS)


def _nt_dot(a, b):
    return lax.dot_general(a, b, (((1,), (1,)), ((), ())), preferred_element_type=F32)


def _dot(a, b):
    return jnp.dot(a, b, preferred_element_type=F32)


def _mods_kernel(c_ref, w_ref, b_ref, o_ref):
    c = c_ref[...]
    a = (c * jax.nn.sigmoid(c)).astype(BF16)
    o_ref[0] = _dot(a, w_ref[0].astype(BF16)) + b_ref[0]


def _mods(c, w_ada, b_ada):
    depth, d, n = w_ada.shape
    b = c.shape[0]
    tn = 1536
    return pl.pallas_call(
        _mods_kernel,
        out_shape=jax.ShapeDtypeStruct((depth, b, n), F32),
        grid=(depth, n // tn),
        in_specs=[pl.BlockSpec((b, d), lambda l, j: (0, 0)),
                  pl.BlockSpec((1, d, tn), lambda l, j: (l, 0, j)),
                  pl.BlockSpec((1, 1, tn), lambda l, j: (l, 0, j))],
        out_specs=pl.BlockSpec((1, b, tn), lambda l, j: (l, 0, j)),
        compiler_params=_cparams(2),
        name="adaln_mods",
    )(c, w_ada, b_ada.reshape(depth, 1, n))


def _rope_kernel(pos_ref, invf_ref, cos_ref, sin_ref):
    pos = pos_ref[0].astype(F32)
    ang = invf_ref[...] * pos
    c = jnp.cos(ang)
    s = jnp.sin(ang)
    tm = pos.shape[1]
    ones = jnp.ones((ROPE_LO, tm), F32)
    zeros = jnp.zeros((ROPE_LO, tm), F32)
    pad = SLOT - ROPE_LO - MLA_ROPE
    cos_t = jnp.concatenate([ones, c, c, jnp.ones((pad, tm), F32)], axis=0)
    sin_t = jnp.concatenate([zeros, -s, s, jnp.zeros((pad, tm), F32)], axis=0)
    cos_ref[0] = cos_t.T
    sin_ref[0] = sin_t.T


def _rope_tables(positions):
    b, s = positions.shape
    tm = s
    half = ROPE_HALF
    inv_freq = jnp.power(ROPE_THETA, -jnp.arange(half, dtype=F32) / half).reshape(half, 1)
    out = jax.ShapeDtypeStruct((b, s, SLOT), F32)
    return pl.pallas_call(
        _rope_kernel,
        out_shape=(out, out),
        grid=(b, s // tm),
        in_specs=[pl.BlockSpec((1, 1, tm), lambda i, j: (i, 0, j)),
                  pl.BlockSpec((half, 1), lambda i, j: (0, 0))],
        out_specs=(pl.BlockSpec((1, tm, SLOT), lambda i, j: (i, j, 0)),
                   pl.BlockSpec((1, tm, SLOT), lambda i, j: (i, j, 0))),
        compiler_params=_cparams(2),
        name="rope_tables",
    )(positions.reshape(b, 1, s), inv_freq)


def _t5_bucket(dist):
    max_exact = REL_BUCKETS // 2
    n = jnp.maximum(dist, 0)
    nf = jnp.maximum(n, 1).astype(F32)
    large = max_exact + (jnp.log(nf / max_exact) / math.log(REL_MAX_DIST / max_exact)
                         * (REL_BUCKETS - max_exact)).astype(jnp.int32)
    large = jnp.minimum(large, REL_BUCKETS - 1)
    return jnp.where(n < max_exact, n, large)


def _bias_kernel(tab_ref, bucket_ref, o_ref):
    bucket = bucket_ref[...]
    for h in range(SW_HEADS):
        acc = jnp.zeros(bucket.shape, F32)
        for bkt in range(REL_BUCKETS):
            acc = jnp.where(bucket == bkt, tab_ref[bkt, h], acc)
        o_ref[h] = jnp.where(bucket >= 0, acc * LOG2E, NEG)


def _window_bias(rel_table):
    nk = WINDOW + TQ_SW
    key = jnp.arange(nk)[:, None]
    qry = jnp.arange(TQ_SW)[None, :]
    dist = qry + WINDOW - key
    valid = (dist >= 0) & (dist < WINDOW)
    bucket = jnp.where(valid, _t5_bucket(dist), -1).astype(jnp.int32)
    return pl.pallas_call(
        _bias_kernel,
        out_shape=jax.ShapeDtypeStruct((SW_HEADS, nk, TQ_SW), F32),
        in_specs=[pl.BlockSpec(memory_space=pltpu.SMEM),
                  pl.BlockSpec(memory_space=pltpu.VMEM)],
        out_specs=pl.BlockSpec(memory_space=pltpu.VMEM),
        name="window_bias",
    )(rel_table, bucket)


def _half_rms(x, gain, scale):
    lo = lax.broadcasted_iota(jnp.int32, x.shape, 1) < HALF
    sq = x * x
    s_lo = jnp.sum(jnp.where(lo, sq, 0.0), axis=-1, keepdims=True)
    s_hi = jnp.sum(jnp.where(lo, 0.0, sq), axis=-1, keepdims=True)
    r = jnp.where(lo, lax.rsqrt(s_lo * (1.0 / HEAD_DIM) + EPS), lax.rsqrt(s_hi * (1.0 / HEAD_DIM) + EPS))
    return x * r * (gain * scale)


def _prep_kernel(x_ref, mod_ref, n1g_ref, win_ref, cqg_ref, wuq_ref, ckvg_ref, wukv_ref,
                 gq_ref, gqs_ref, gk_ref, gks_ref, swqg_ref, swkg_ref, cos_ref, sin_ref,
                 sbq_ref, sbk_ref, sbvt_ref, mq_ref, mk_ref, mvt_ref, swq_ref, swk_ref, swvt_ref):
    x = x_ref[0]
    mod = mod_ref[0]
    shift1, scale1 = mod[0:1], mod[1:2]
    h = (_rms(x, D_MODEL) * n1g_ref[...] * (1.0 + scale1) + shift1).astype(BF16)

    proj_mla = _dot(h, win_ref[:, C_CQ:C_SWQ])
    proj_sb = _dot(h, win_ref[:, C_SBQ:C_CQ])
    cq = proj_mla[:, 0:MLA_Q_RANK]
    ckv = proj_mla[:, C_CKV - C_CQ:C_CKV - C_CQ + MLA_KV_RANK]
    krope = proj_mla[:, C_KROPE - C_CQ:C_KROPE - C_CQ + SLOT]
    krope_sw = proj_mla[:, C_KROPE_SW - C_CQ:C_KROPE_SW - C_CQ + SLOT]
    cqn = (_rms(cq, MLA_Q_RANK) * cqg_ref[...]).astype(BF16)
    ckvn = (_rms(ckv, MLA_KV_RANK) * ckvg_ref[...]).astype(BF16)
    qraw = _dot(cqn, wuq_ref[...])
    kv = _dot(ckvn, wukv_ref[...])
    proj_sw = _dot(h, win_ref[:, C_SWQ:N_IN])

    sbq_ref[0] = (proj_sb[:, C_SBQ:C_SBQ + 256] * (HEAD_DIM ** -0.5 * LOG2E)).astype(BF16)
    sbk_ref[0] = proj_sb[:, C_SBK:C_SBK + 256].astype(BF16)
    sbvt_ref[0] = proj_sb[:, C_SBV:C_SBV + 256].T.astype(BF16)

    cos = cos_ref[0]
    sin = sin_ref[0]

    nq = MLA_HEADS * SLOT
    q_scale = MLA_QK ** -0.5 * LOG2E
    q_cos, q_sin = gq_ref[...] * cos, gqs_ref[...] * sin
    for hd in range(MLA_HEADS):
        slot = qraw[:, hd * SLOT:(hd + 1) * SLOT]
        swapped = qraw[:, nq + hd * SLOT:nq + (hd + 1) * SLOT]
        r = lax.rsqrt(jnp.sum(slot * slot, axis=-1, keepdims=True) * (1.0 / MLA_QK) + EPS) * q_scale
        mq_ref[0, :, hd * SLOT:(hd + 1) * SLOT] = ((slot * q_cos + swapped * q_sin) * r).astype(BF16)

    mvt_ref[0] = kv[:, nq:nq + MLA_HEADS * MLA_V].T.astype(BF16)
    k_cos = gk_ref[...] * cos
    k_rot = krope_sw * (gks_ref[...] * sin)
    for hd in range(MLA_HEADS):
        slot = kv[:, hd * SLOT:(hd + 1) * SLOT] + krope
        r = lax.rsqrt(jnp.sum(slot * slot, axis=-1, keepdims=True) * (1.0 / MLA_QK) + EPS)
        mk_ref[0, :, hd * SLOT:(hd + 1) * SLOT] = ((slot * k_cos + k_rot) * r).astype(BF16)

    for g in range(SW_HEADS // 2):
        xq = proj_sw[:, g * SLOT:(g + 1) * SLOT]
        swq_ref[0, :, g * SLOT:(g + 1) * SLOT] = _half_rms(
            xq, swqg_ref[...], HEAD_DIM ** -0.5 * LOG2E).astype(BF16)
    swk_ref[0] = _half_rms(proj_mla[:, C_SWK - C_CQ:C_SWK - C_CQ + SLOT], swkg_ref[...], 1.0).astype(BF16)
    swvt_ref[0] = proj_sw[:, C_SWV - C_SWQ:C_SWV - C_SWQ + SLOT].T.astype(BF16)


def _prep(x, mods, n1g, w_in, cqg, w_uq, ckvg, w_ukv, gq, gqs, gk, gks, swqg, swkg, cos_t, sin_t):
    b, s, d = x.shape
    tm = TM_PREP
    row = lambda w: pl.BlockSpec((1, tm, w), lambda i, j: (i, j, 0))
    colt = lambda w: pl.BlockSpec((1, w, tm), lambda i, j: (i, 0, j))
    full = lambda a: pl.BlockSpec(a.shape, lambda i, j: (0,) * a.ndim)
    act = lambda w: jax.ShapeDtypeStruct((b, s, w), BF16)
    actt = lambda w: jax.ShapeDtypeStruct((b, w, s), BF16)
    return pl.pallas_call(
        _prep_kernel,
        out_shape=(act(256), act(256), actt(256), act(768), act(768), actt(384),
                   act(384), act(128), actt(128)),
        grid=(b, s // tm),
        in_specs=[row(d), pl.BlockSpec((1, 6, d), lambda i, j: (i, 0, 0)), full(n1g), full(w_in),
                  full(cqg), full(w_uq), full(ckvg), full(w_ukv), full(gq), full(gqs), full(gk),
                  full(gks), full(swqg), full(swkg), row(SLOT), row(SLOT)],
        out_specs=(row(256), row(256), colt(256), row(768), row(768), colt(384),
                   row(384), row(128), colt(128)),
        compiler_params=_cparams(2),
        name="prep_qkv",
    )(x, mods, n1g, w_in, cqg, w_uq, ckvg, w_ukv, gq, gqs, gk, gks, swqg, swkg, cos_t, sin_t)


def _half_mask(q, half):
    lane = lax.broadcasted_iota(jnp.int32, q.shape, 1)
    keep = (lane < HALF) if half == 0 else (lane >= HALF)
    return jnp.where(keep, q, jnp.zeros_like(q))


def _store_pair(o_ref, g, out_lo, out_hi, row0=0):
    pair = jnp.concatenate([out_lo, out_hi], axis=0)
    o_ref[0, row0:row0 + pair.shape[1], g * SLOT:(g + 1) * SLOT] = pair.T.astype(o_ref.dtype)


SIGN_BIT = -2 ** 31
JOBS = (("a", 0), ("b", 0), ("b", 1), ("a", 1), ("b", 2), ("b", 3), ("a", 2), ("b", 4), ("b", 5), ("a", 3))


def _softplus2(z):
    neg_abs = lax.bitcast_convert_type(lax.bitcast_convert_type(z, jnp.int32) | SIGN_BIT, F32)
    return jnp.maximum(z, 0.0) + jnp.log(1.0 + jnp.exp2(neg_abs)) * LOG2E


def _causal_kernel(qb, sink_ref, aq_ref, ak_ref, avt_ref, tri_ref, bq_ref, bk_ref, bvt_ref,
                   cq_ref, ck_ref, cvt_ref, bias_ref, pa_ref, pb_ref, pc_ref, oa_ref, ob_ref, oc_ref):
    del pa_ref, pb_ref, pc_ref
    n_diag = TQ // TK
    a_heads, b_heads = range(SB_HEADS), range(MLA_HEADS)

    def put(full, part, c0):
        return part if c0 == 0 else jnp.concatenate([full[:, :c0], part], axis=1)

    def run_jobs(carry, jobs):
        ca, cb = list(carry[0]), list(carry[1])
        st = [None] * len(jobs)
        out_c = {}

        def geometry(t):
            mixer, hd, start, diag = jobs[t]
            c0 = 0 if diag is None else diag * TK
            row = lax.broadcasted_iota(jnp.int32, (TK, TQ - c0), 0)
            col = lax.broadcasted_iota(jnp.int32, (TK, TQ - c0), 1)
            return mixer, hd, start, diag is not None, c0, row, col

        def window(t):
            _, slot, sub = jobs[t]
            head = slot // 2 + (slot % 2) * (SW_HEADS // SW_KV_HEADS)
            key0 = qb * TQ + sub * TQ_SW - WINDOW
            lo = max(key0, 0)
            return slot, sub, head, lo, key0 + WINDOW + TQ_SW, lo - key0

        def scores(t):
            if jobs[t][0] == "c":
                slot, sub, head, lo, hi, skip = window(t)
                q = _half_mask(cq_ref[0, sub * TQ_SW:(sub + 1) * TQ_SW, (slot // 2) * SLOT:(slot // 2 + 1) * SLOT],
                               slot % 2)
                st[t] = _nt_dot(ck_ref[0, lo:hi, :], q) + bias_ref[head, skip:, :]
                return
            mixer, hd, start, masked, c0, row, col = geometry(t)
            if mixer == "a":
                g, half = divmod(hd, 2)
                q = _half_mask(aq_ref[0, c0:, g * SLOT:(g + 1) * SLOT], half)
                st[t] = _nt_dot(ak_ref[0, pl.ds(start, TK), g * SLOT:(g + 1) * SLOT], q)
            else:
                sc = _nt_dot(bk_ref[0, pl.ds(start, TK), hd * SLOT:(hd + 1) * SLOT],
                             bq_ref[0, c0:, hd * SLOT:(hd + 1) * SLOT])
                st[t] = jnp.where(row <= col, sc, NEG) if masked else sc

        def second(t):
            if jobs[t][0] == "c":
                slot, sub, head, lo, hi, skip = window(t)
                sink = sink_ref[head] * LOG2E
                m = jnp.maximum(jnp.max(st[t], axis=0, keepdims=True), sink)
                p = jnp.exp2(st[t] - m)
                denom = jnp.sum(p, axis=0, keepdims=True) + jnp.exp2(sink - m)
                vt = cvt_ref[0, (slot % 2) * HEAD_DIM:(slot % 2 + 1) * HEAD_DIM, lo:hi]
                out_c[(sub, slot)] = _dot(vt, p.astype(BF16)) * (1.0 / denom)
                st[t] = None
                return
            mixer, hd, start, masked, c0, row, col = geometry(t)
            if mixer == "a":
                drop = _softplus2(st[t])
                if masked:
                    drop = jnp.where(row < col, drop, 0.0)
                st[t] = (st[t], _dot(tri_ref[...], drop.astype(BF16)))
            else:
                m_all, l_all, acc_all = cb[hd]
                m_old = m_all[:, c0:]
                m_new = jnp.maximum(m_old, jnp.max(st[t], axis=0, keepdims=True))
                alpha = jnp.exp2(m_old - m_new)
                p = jnp.exp2(st[t] - m_new)
                l_new = alpha * l_all[:, c0:] + jnp.sum(p, axis=0, keepdims=True)
                pv = _dot(bvt_ref[0, hd * MLA_V:(hd + 1) * MLA_V, pl.ds(start, TK)], p.astype(BF16))
                cb[hd] = (put(m_all, m_new, c0), put(l_all, l_new, c0),
                          put(acc_all, alpha * acc_all[:, c0:] + pv, c0))

        def third(t):
            if jobs[t][0] == "c":
                return
            mixer, hd, start, masked, c0, row, col = geometry(t)
            if mixer == "a":
                z, incl = st[t]
                acc, run = ca[hd]
                w = jnp.exp2(z + incl + run[:, c0:])
                if masked:
                    w = jnp.where(row < col, w, 0.0)
                pv = _dot(avt_ref[0, hd * HEAD_DIM:(hd + 1) * HEAD_DIM, pl.ds(start, TK)], w.astype(BF16))
                ca[hd] = (put(acc, acc[:, c0:] + pv, c0), put(run, run[:, c0:] + incl[0:1, :], c0))
            st[t] = None

        n = len(jobs)
        for t in range(n + 2):
            if t < n:
                scores(t)
            if 0 <= t - 1 < n:
                second(t - 1)
            if 0 <= t - 2 < n:
                third(t - 2)
        return (tuple(ca), tuple(cb)), out_c

    carry = (tuple((jnp.zeros((HEAD_DIM, TQ), F32), jnp.zeros((1, TQ), F32)) for _ in a_heads),
             tuple((jnp.full((1, TQ), NEG, F32), jnp.zeros((1, TQ), F32), jnp.zeros((MLA_V, TQ), F32))
                   for _ in b_heads))
    n_past = qb * n_diag
    past = [(j * TK, None) for j in reversed(range(n_past))]
    a_blocks = [((n_past + d) * TK, d) for d in reversed(range(n_diag))] + past
    b_blocks = [((n_past + d) * TK, d) for d in range(n_diag)] + past
    sweep = [(mixer, hd) + (a_blocks[i] if mixer == "a" else b_blocks[i])
             for i in range(len(a_blocks)) for (mixer, hd) in JOBS]
    local = [("c", slot, sub) for sub in range(TQ // TQ_SW) for slot in range(SW_HEADS)]
    jobs = []
    for job in sweep:
        jobs.append(job)
        if local:
            jobs.append(local.pop(0))
    (ca, cb), out_c = run_jobs(carry, jobs)
    for sub in range(TQ // TQ_SW):
        for g in range(SW_HEADS // 2):
            _store_pair(oc_ref, g, out_c[(sub, 2 * g)], out_c[(sub, 2 * g + 1)], sub * TQ_SW)
    for g in range(SB_HEADS // 2):
        _store_pair(oa_ref, g, ca[2 * g][0], ca[2 * g + 1][0])
    outs = [acc * (1.0 / l) for (_, l, acc) in cb]
    for g in range(MLA_HEADS // 2):
        _store_pair(ob_ref, g, outs[2 * g], outs[2 * g + 1])


def _attention(sinks, aq, ak, avt, bq, bk, bvt, cq, ck, cvt, bias):
    b, s, wa = aq.shape
    wb, wbv, wc = bq.shape[2], bvt.shape[1], cq.shape[2]
    tri = -(jnp.arange(TK)[None, :] >= jnp.arange(TK)[:, None]).astype(BF16)
    const = lambda a: pl.BlockSpec(a.shape, lambda i: (0,) * a.ndim)
    outs = tuple(jnp.zeros((b, s, w), BF16) for w in (wa, wbv, wc))
    n_in = 12
    for qb in range(s // TQ):
        nk = (qb + 1) * TQ
        qblk = lambda w, qb=qb: pl.BlockSpec((1, TQ, w), lambda i: (i, qb, 0))
        keys = lambda w, nk=nk: pl.BlockSpec((1, nk, w), lambda i: (i, 0, 0))
        keyst = lambda w, nk=nk: pl.BlockSpec((1, w, nk), lambda i: (i, 0, 0))
        outs = pl.pallas_call(
            functools.partial(_causal_kernel, qb),
            out_shape=tuple(jax.ShapeDtypeStruct(o.shape, o.dtype) for o in outs),
            grid=(b,),
            in_specs=[pl.BlockSpec(memory_space=pltpu.SMEM),
                      qblk(wa), keys(wa), keyst(wa), const(tri),
                      qblk(wb), keys(wb), keyst(wbv),
                      qblk(wc), keys(ck.shape[2]), keyst(cvt.shape[1]), const(bias)]
                     + [pl.BlockSpec(memory_space=pl.ANY)] * 3,
            out_specs=(qblk(wa), qblk(wbv), qblk(wc)),
            input_output_aliases={n_in + k: k for k in range(3)},
            compiler_params=_cparams(1),
            name=f"attention_q{qb}",
        )(sinks, aq, ak, avt, tri, bq, bk, bvt, cq, ck, cvt, bias, *outs)
    return outs


def _mlp_kernel(x_ref, mod_ref, oa_ref, ob_ref, oc_ref, wo_ref, n2g_ref, wup_ref, cw_ref, cb_ref,
                wdn_ref, out_ref, carry_ref):
    si = pl.program_id(1)
    tm = x_ref.shape[1]
    mod = mod_ref[0]
    gate1, shift2, scale2, gate2 = mod[2:3], mod[3:4], mod[4:5], mod[5:6]
    mix = jnp.concatenate([oa_ref[0], ob_ref[0], oc_ref[0]], axis=1)
    x1 = x_ref[0] + gate1 * _dot(mix, wo_ref[...])
    h2 = (_rms(x1, D_MODEL) * n2g_ref[...] * (1.0 + scale2) + shift2).astype(BF16)

    @pl.when(si == 0)
    def _():
        carry_ref[...] = jnp.zeros_like(carry_ref)

    def up(c):
        lo = c * FF_CHUNK
        return (_dot(h2, wup_ref[:, lo:lo + FF_CHUNK]),
                _dot(h2, wup_ref[:, D_FF + lo:D_FF + lo + FF_CHUNK]))

    def conv(u, col):
        cols = slice(col, col + FF_CHUNK)
        prev = carry_ref[:, cols]
        carry_ref[:, cols] = u[tm - CARRY_ROWS:, :]
        ext = jnp.concatenate([prev, u], axis=0)
        u1 = ext[CARRY_ROWS - 1:CARRY_ROWS - 1 + tm, :]
        u2 = ext[CARRY_ROWS - 2:CARRY_ROWS - 2 + tm, :]
        cw = cw_ref[:, cols]
        return u * cw[2:3] + u1 * cw[1:2] + u2 * cw[0:1] + cb_ref[:, cols]

    n_chunks = D_FF // FF_CHUNK
    ahead = [up(c) for c in range(min(UP_AHEAD, n_chunks))]
    for c in range(n_chunks):
        u_gate, u_val = ahead.pop(0)
        if c + UP_AHEAD < n_chunks:
            ahead.append(up(c + UP_AHEAD))
        gate = conv(u_gate, c * FF_CHUNK)
        val = conv(u_val, D_FF + c * FF_CHUNK)
        a = (gate * jax.nn.sigmoid(gate) * val).astype(BF16)
        part = _dot(a, wdn_ref[c * FF_CHUNK:(c + 1) * FF_CHUNK, :])
        acc = part if c == 0 else acc + part
    out_ref[0] = x1 + gate2 * acc


def _mlp(x, mods, oa, ob, oc, w_out, n2g, w_up, conv_w, conv_b, w_down):
    b, s, d = x.shape
    tm = TM_MLP
    row = lambda w: pl.BlockSpec((1, tm, w), lambda i, j: (i, j, 0))
    const = lambda a: pl.BlockSpec(a.shape, lambda i, j: (0,) * a.ndim, pipeline_mode=pl.Buffered(1))
    return pl.pallas_call(
        _mlp_kernel,
        out_shape=jax.ShapeDtypeStruct((b, s, d), F32),
        grid=(b, s // tm),
        in_specs=[row(d), pl.BlockSpec((1, 6, d), lambda i, j: (i, 0, 0)),
                  row(oa.shape[2]), row(ob.shape[2]), row(oc.shape[2]),
                  const(w_out), const(n2g), const(w_up), const(conv_w), const(conv_b), const(w_down)],
        out_specs=row(d),
        scratch_shapes=[pltpu.VMEM((CARRY_ROWS, 2 * D_FF), F32)],
        compiler_params=_cparams(2),
        name="outproj_mlp",
    )(x, mods, oa, ob, oc, w_out, n2g, w_up, conv_w, conv_b, w_down)


def _slot_gain(g, swap):
    z = jnp.zeros((SLOT - MLA_QK,), F32)
    lo, hi = g[ROPE_LO:ROPE_LO + ROPE_HALF], g[ROPE_LO + ROPE_HALF:MLA_QK]
    if swap:
        return jnp.concatenate([jnp.zeros((MLA_NOPE,), F32), hi, lo, z]).reshape(1, SLOT)
    return jnp.concatenate([g, z]).reshape(1, SLOT)


def _layout_w_in(w):
    d = w.shape[0]
    z = lambda n: jnp.zeros((d, n), w.dtype)
    kr = w[:, 1152:1184]
    swq = w[:, 1184:1568].reshape(d, SW_HEADS, HEAD_DIM)
    order = [0, 3, 1, 4, 2, 5]
    swq = swq[:, order, :].reshape(d, SW_HEADS * HEAD_DIM)
    cols = [w[:, 0:1152],
            z(MLA_NOPE), kr, z(SLOT - MLA_QK),
            z(MLA_NOPE), kr[:, ROPE_HALF:], kr[:, :ROPE_HALF], z(SLOT - MLA_QK),
            w[:, 1568:1696], swq, w[:, 1696:1824]]
    return jnp.concatenate(cols, axis=1).astype(BF16)


def _layout_w_uq(w):
    r = w.shape[0]
    w = w.reshape(r, MLA_HEADS, MLA_QK)
    nope, x1, x2 = w[..., :MLA_NOPE], w[..., MLA_NOPE:MLA_NOPE + ROPE_HALF], w[..., MLA_NOPE + ROPE_HALF:]
    z = jnp.zeros((r, MLA_HEADS, SLOT - MLA_QK), w.dtype)
    plain = jnp.concatenate([nope, x1, x2, z], axis=-1).reshape(r, MLA_HEADS * SLOT)
    swapped = jnp.concatenate([jnp.zeros_like(nope), x2, x1, z], axis=-1).reshape(r, MLA_HEADS * SLOT)
    return jnp.concatenate([plain, swapped], axis=1).astype(BF16)


def _layout_w_ukv(w):
    r = w.shape[0]
    w = w.reshape(r, MLA_HEADS, MLA_NOPE + MLA_V)
    k_nope, v = w[..., :MLA_NOPE], w[..., MLA_NOPE:]
    k_slots = jnp.concatenate([k_nope, jnp.zeros((r, MLA_HEADS, SLOT - MLA_NOPE), w.dtype)], axis=-1)
    return jnp.concatenate([k_slots.reshape(r, MLA_HEADS * SLOT), v.reshape(r, MLA_HEADS * MLA_V)],
                           axis=1).astype(BF16)


def _layout_w_out(w):
    n_ab = SB_HEADS * HEAD_DIM + MLA_HEADS * MLA_V
    sw = w[n_ab:].reshape(SW_HEADS, HEAD_DIM, w.shape[1])[jnp.array([0, 3, 1, 4, 2, 5])]
    return jnp.concatenate([w[:n_ab], sw.reshape(SW_HEADS * HEAD_DIM, w.shape[1])], axis=0).astype(BF16)


def kernel(x, c, positions, rel_table, norm1_g, norm2_g, w_ada, b_ada, w_in, mla_cq_g, w_uq, mla_ckv_g,
           w_ukv, mla_qn_g, mla_kn_g, sw_qn_g, sw_kn_g, sw_sinks, w_out, w_up, conv_w, conv_b, w_down):
    depth = w_in.shape[0]
    b = x.shape[0]
    mods = _mods(c, w_ada, b_ada).reshape(depth, b, 6, D_MODEL)
    cos_t, sin_t = _rope_tables(positions)
    bias = _window_bias(rel_table)
    row = lambda v: v.reshape(1, -1).astype(F32)
    two = lambda v: jnp.concatenate([v, v]).reshape(1, SLOT).astype(F32)
    for l in range(depth):
        qkv = _prep(x, mods[l], row(norm1_g[l]), _layout_w_in(w_in[l]), row(mla_cq_g[l]),
                    _layout_w_uq(w_uq[l]), row(mla_ckv_g[l]), _layout_w_ukv(w_ukv[l]),
                    _slot_gain(mla_qn_g[l], False), _slot_gain(mla_qn_g[l], True),
                    _slot_gain(mla_kn_g[l], False), _slot_gain(mla_kn_g[l], True),
                    two(sw_qn_g[l]), two(sw_kn_g[l]), cos_t, sin_t)
        sbq, sbk, sbvt, mq, mk, mvt, swq, swk, swvt = qkv
        o_a, o_b, o_c = _attention(sw_sinks[l], sbq, sbk, sbvt, mq, mk, mvt, swq, swk, swvt, bias)
        x = _mlp(x, mods[l], o_a, o_b, o_c, _layout_w_out(w_out[l]), row(norm2_g[l]),
                 w_up[l].astype(BF16), conv_w[l], row(conv_b[l]), w_down[l].astype(BF16))
    return x
```
